```python
import functools
import jax, jax.numpy as jnp
from jax import lax
import numpy as np

D_MODEL = 1024
BATCH = 16
SEQ = 2048
DEPTH = 1
DEC_BATCH = 128
DEC_SEQ = 8
PAST_LEN = 8192
PAGE_SIZE = 128

N_META = 16
MLA_HEADS = 8
Q_LORA = 384
KV_LORA = 256
QK_NOPE = 64
QK_ROPE = 32
V_HEAD = 64
ROPE_THETA = 10000.0
CONV_DIM = D_MODEL // 2
CONV_K = 3
D_FF = 4 * D_MODEL
Q_BLOCK = 128
LN_EPS = 1e-5
RMS_EPS = 1e-6
ALPHA = (2.0 * DEPTH) ** 0.25
BETA = (8.0 * DEPTH) ** -0.25
SM_SCALE = (QK_NOPE + QK_ROPE) ** -0.5
SPLIT_SIZES = (Q_LORA, KV_LORA, QK_ROPE, CONV_DIM, CONV_DIM, CONV_DIM, D_MODEL, D_MODEL)
D_IN = Q_LORA + KV_LORA + QK_ROPE + 3 * CONV_DIM + 2 * D_MODEL

kernel_name = "hybrid_conv_mla_deepnorm_step"


def layer_norm(x, g, b):
    xf = x.astype(jnp.float32)
    mu = jnp.mean(xf, axis=-1, keepdims=True)
    var = jnp.mean(jnp.square(xf - mu), axis=-1, keepdims=True)
    y = (xf - mu) * lax.rsqrt(var + LN_EPS) * g.astype(jnp.float32) + b.astype(jnp.float32)
    return y.astype(x.dtype)


def rms_norm(x, g):
    xf = x.astype(jnp.float32)
    y = xf * lax.rsqrt(jnp.mean(jnp.square(xf), axis=-1, keepdims=True) + RMS_EPS) * g.astype(jnp.float32)
    return y.astype(x.dtype)


def rope(x, pos):
    half = QK_ROPE // 2
    inv_freq = ROPE_THETA ** (-2.0 * jnp.arange(half, dtype=jnp.float32) / QK_ROPE)
    ang = pos.astype(jnp.float32)[:, None] * inv_freq[None, :]
    cos = jnp.cos(ang)[:, None, :]
    sin = jnp.sin(ang)[:, None, :]
    xf = x.astype(jnp.float32)
    x1, x2 = xf[..., :half], xf[..., half:]
    return jnp.concatenate([x1 * cos - x2 * sin, x2 * cos + x1 * sin], axis=-1).astype(x.dtype)


def mixer_projections(h, pos, w):
    b, s, _ = h.shape
    offsets = [int(v) for v in np.cumsum(SPLIT_SIZES)[:-1]]
    z = h @ w["w_in"]
    cq, ckv_raw, kpe_raw, gate_b, gate_c, conv_h, gate_cv, gate_ml = jnp.split(z, offsets, axis=-1)
    cq = rms_norm(cq, w["q_norm_g"])
    q = (cq @ w["w_uq"]).reshape(b, s, MLA_HEADS, QK_NOPE + QK_ROPE)
    q_nope, q_pe = q[..., :QK_NOPE], q[..., QK_NOPE:]
    q_pe = rope(q_pe, pos)
    q_lat = jnp.einsum("bshn,chn->bshc", q_nope, w["w_uk"])
    ckv = rms_norm(ckv_raw, w["kv_norm_g"])
    kpe = rope(kpe_raw[:, :, None, :], pos)[:, :, 0, :]
    u = gate_c * conv_h
    g_conv = jax.nn.sigmoid(gate_cv)
    g_mla = jax.nn.sigmoid(gate_ml)
    return q_lat, q_pe, ckv, kpe, gate_b, u, g_conv, g_mla


def short_conv(u, prev, conv_w):
    s = u.shape[1]
    full = jnp.concatenate([prev, u], axis=1)
    out = sum(conv_w[k] * full[:, k:k + s] for k in range(CONV_K))
    return out, full[:, -(CONV_K - 1):]


def attend_prompt(q_lat, q_pe, ckv, kpe):
    b, L = q_lat.shape[0], q_lat.shape[1]
    lp = -(-L // Q_BLOCK) * Q_BLOCK
    pad = lp - L
    nb = lp // Q_BLOCK
    q_lat_p = jnp.pad(q_lat, ((0, 0), (0, pad), (0, 0), (0, 0)))
    q_pe_p = jnp.pad(q_pe, ((0, 0), (0, pad), (0, 0), (0, 0)))
    ckv_p = jnp.pad(ckv, ((0, 0), (0, pad), (0, 0)))
    kpe_p = jnp.pad(kpe, ((0, 0), (0, pad), (0, 0)))
    k_pos = jnp.arange(lp)
    qb = q_lat_p.reshape(b, nb, Q_BLOCK, MLA_HEADS, KV_LORA).swapaxes(0, 1)
    pb = q_pe_p.reshape(b, nb, Q_BLOCK, MLA_HEADS, QK_ROPE).swapaxes(0, 1)
    starts = jnp.arange(nb) * Q_BLOCK

    def block(args):
        ql, qp, start = args
        q_pos = start + jnp.arange(Q_BLOCK)
        sc = (jnp.einsum("bqhc,bkc->bhqk", ql, ckv_p) + jnp.einsum("bqhr,bkr->bhqk", qp, kpe_p)).astype(jnp.float32) * SM_SCALE
        sc = jnp.where(k_pos[None, :] <= q_pos[:, None], sc, -jnp.inf)
        p = jax.nn.softmax(sc, axis=-1).astype(ckv_p.dtype)
        return jnp.einsum("bhqk,bkc->bqhc", p, ckv_p)

    o = lax.map(block, (qb, pb, starts))
    return o.swapaxes(0, 1).reshape(b, lp, MLA_HEADS, KV_LORA)[:, :L]


def attend_sample(q_lat, q_pe, ckv_new, kpe_new, ckv_pool, kpe_pool, page_table):
    db = page_table.shape[0]
    n = q_lat.shape[1]
    past_ckv = ckv_pool[page_table].reshape(db, -1, KV_LORA)
    past_kpe = kpe_pool[page_table].reshape(db, -1, QK_ROPE)
    past = past_ckv.shape[1]
    s_past = jnp.einsum("bqhc,bkc->bhqk", q_lat, past_ckv) + jnp.einsum("bqhr,bkr->bhqk", q_pe, past_kpe)
    s_new = jnp.einsum("bqhc,bkc->bhqk", q_lat, ckv_new) + jnp.einsum("bqhr,bkr->bhqk", q_pe, kpe_new)
    s_new = jnp.where(jnp.tril(jnp.ones((n, n), dtype=bool)), s_new.astype(jnp.float32), -jnp.inf)
    sc = jnp.concatenate([s_past.astype(jnp.float32), s_new], axis=-1) * SM_SCALE
    p = jax.nn.softmax(sc, axis=-1).astype(ckv_new.dtype)
    return (jnp.einsum("bhqk,bkc->bqhc", p[..., :past], past_ckv)
            + jnp.einsum("bhqk,bkc->bqhc", p[..., past:], ckv_new))


def layer_forward(h, pos, conv_prev, attend, w):
    b, s, _ = h.shape
    q_lat, q_pe, ckv, kpe, gate_b, u, g_conv, g_mla = mixer_projections(h, pos, w)
    conv_out, conv_state = short_conv(u, conv_prev, w["conv_w"])
    y_conv = (gate_b * conv_out) @ w["w_conv_out"]
    o_lat = attend(q_lat, q_pe, ckv, kpe)
    v = jnp.einsum("bshc,chv->bshv", o_lat, w["w_uv"]).reshape(b, s, MLA_HEADS * V_HEAD)
    y_mla = v @ w["w_o_mla"]
    mix = (g_conv * y_conv + g_mla * y_mla) @ w["w_o"]
    h1 = layer_norm(ALPHA * h + mix, w["ln1_g"], w["ln1_b"])
    ff = jnp.square(jax.nn.relu(h1 @ w["w_up"])) @ w["w_down"]
    h2 = layer_norm(ALPHA * h1 + ff, w["ln2_g"], w["ln2_b"])
    return h2, ckv, kpe, conv_state


def setup_inputs(seed: int = 0) -> dict:
    key = jax.random.key(seed)
    ks = jax.random.split(key, 32)
    f32 = jnp.float32
    nrm = lambda k, shape, scale: jax.random.normal(k, shape, f32) * scale
    n_pages = PAST_LEN // PAGE_SIZE
    n_used = DEC_BATCH * n_pages
    n_pool = n_used + n_used // 4
    page_table = jax.random.permutation(ks[5], n_pool)[:n_used].reshape(DEC_BATCH, n_pages).astype(jnp.int32)
    return {
        "x_prompt": nrm(ks[0], (BATCH, SEQ, D_MODEL), 1.0),
        "x_sample": nrm(ks[1], (DEC_BATCH, DEC_SEQ, D_MODEL), 1.0),
        "cache_ckv": nrm(ks[2], (DEPTH, n_pool, PAGE_SIZE, KV_LORA), 1.0),
        "cache_kpe": nrm(ks[3], (DEPTH, n_pool, PAGE_SIZE, QK_ROPE), 1.0),
        "state_conv": nrm(ks[4], (DEPTH, DEC_BATCH, CONV_K - 1, CONV_DIM), 1.0),
        "page_table": page_table,
        "meta_tokens": nrm(ks[6], (N_META, D_MODEL), 1.0),
        "ln_emb_g": 1.0 + nrm(ks[7], (D_MODEL,), 0.02),
        "ln_emb_b": nrm(ks[8], (D_MODEL,), 0.02),
        "w_in": nrm(ks[9], (DEPTH, D_MODEL, D_IN), D_MODEL ** -0.5),
        "q_norm_g": 1.0 + nrm(ks[10], (DEPTH, Q_LORA), 0.02),
        "w_uq": nrm(ks[11], (DEPTH, Q_LORA, MLA_HEADS * (QK_NOPE + QK_ROPE)), Q_LORA ** -0.5),
        "kv_norm_g": 1.0 + nrm(ks[12], (DEPTH, KV_LORA), 0.02),
        "w_uk": nrm(ks[13], (DEPTH, KV_LORA, MLA_HEADS, QK_NOPE), KV_LORA ** -0.5),
        "w_uv": nrm(ks[14], (DEPTH, KV_LORA, MLA_HEADS, V_HEAD), KV_LORA ** -0.5),
        "w_o_mla": nrm(ks[15], (DEPTH, MLA_HEADS * V_HEAD, D_MODEL), (MLA_HEADS * V_HEAD) ** -0.5),
        "conv_w": nrm(ks[16], (DEPTH, CONV_K, CONV_DIM), CONV_K ** -0.5),
        "w_conv_out": nrm(ks[17], (DEPTH, CONV_DIM, D_MODEL), CONV_DIM ** -0.5),
        "w_o": nrm(ks[18], (DEPTH, D_MODEL, D_MODEL), BETA * D_MODEL ** -0.5),
        "ln1_g": 1.0 + nrm(ks[19], (DEPTH, D_MODEL), 0.02),
        "ln1_b": nrm(ks[20], (DEPTH, D_MODEL), 0.02),
        "w_up": nrm(ks[21], (DEPTH, D_MODEL, D_FF), D_MODEL ** -0.5),
        "w_down": nrm(ks[22], (DEPTH, D_FF, D_MODEL), BETA * D_FF ** -0.5),
        "ln2_g": 1.0 + nrm(ks[23], (DEPTH, D_MODEL), 0.02),
        "ln2_b": nrm(ks[24], (DEPTH, D_MODEL), 0.02),
    }


def reference(x_prompt, x_sample, cache_ckv, cache_kpe, state_conv, page_table, meta_tokens,
              ln_emb_g, ln_emb_b, w_in, q_norm_g, w_uq, kv_norm_g, w_uk, w_uv, w_o_mla,
              conv_w, w_conv_out, w_o, ln1_g, ln1_b, w_up, w_down, ln2_g, ln2_b):
    bp, sp, _ = x_prompt.shape
    meta = jnp.broadcast_to(meta_tokens[None].astype(x_prompt.dtype), (bp, N_META, D_MODEL))
    hp = layer_norm(jnp.concatenate([meta, x_prompt], axis=1), ln_emb_g, ln_emb_b)
    hs = layer_norm(x_sample, ln_emb_g, ln_emb_b)
    pos_p = jnp.arange(N_META + sp)
    past_len = page_table.shape[1] * PAGE_SIZE
    pos_s = past_len + jnp.arange(x_sample.shape[1])
    conv_zero = jnp.zeros((bp, CONV_K - 1, CONV_DIM), dtype=hp.dtype)
    ckv_p_l, kpe_p_l, conv_p_l, ckv_s_l, kpe_s_l, conv_s_l = [], [], [], [], [], []
    for l in range(DEPTH):
        w = {
            "w_in": w_in[l], "q_norm_g": q_norm_g[l], "w_uq": w_uq[l], "kv_norm_g": kv_norm_g[l],
            "w_uk": w_uk[l], "w_uv": w_uv[l], "w_o_mla": w_o_mla[l], "conv_w": conv_w[l],
            "w_conv_out": w_conv_out[l], "w_o": w_o[l], "ln1_g": ln1_g[l], "ln1_b": ln1_b[l],
            "w_up": w_up[l], "w_down": w_down[l], "ln2_g": ln2_g[l], "ln2_b": ln2_b[l],
        }
        hp, ckv_p, kpe_p, conv_p = layer_forward(hp, pos_p, conv_zero, attend_prompt, w)
        attend_s = functools.partial(attend_sample, ckv_pool=cache_ckv[l], kpe_pool=cache_kpe[l], page_table=page_table)
        hs, ckv_s, kpe_s, conv_s = layer_forward(hs, pos_s, state_conv[l], attend_s, w)
        ckv_p_l.append(ckv_p); kpe_p_l.append(kpe_p); conv_p_l.append(conv_p)
        ckv_s_l.append(ckv_s); kpe_s_l.append(kpe_s); conv_s_l.append(conv_s)
    y_prompt = hp[:, N_META:]
    y_sample = hs
    return (y_prompt, y_sample,
            jnp.stack(ckv_p_l), jnp.stack(kpe_p_l), jnp.stack(conv_p_l),
            jnp.stack(ckv_s_l), jnp.stack(kpe_s_l), jnp.stack(conv_s_l))
```

```python
import functools

import jax
import jax.numpy as jnp
from jax import lax
from jax.experimental import pallas as pl
from jax.experimental.pallas import tpu as pltpu

D_MODEL = 1024
N_META = 16
HEADS = 8
Q_LORA = 384
KV_LORA = 256
QK_NOPE = 64
QK_ROPE = 32
V_HEAD = 64
CONV_DIM = 512
CONV_K = 3
D_FF = 4096
PAGE = 128
ROPE_THETA = 10000.0
LN_EPS = 1e-5
RMS_EPS = 1e-6
DEPTH = 1
ALPHA = (2.0 * DEPTH) ** 0.25
SM_SCALE = (QK_NOPE + QK_ROPE) ** -0.5

HEAD_PAD = 128
ROPE_LO = QK_NOPE
HALF = QK_ROPE // 2

C_CQ = 0
C_CKV = C_CQ + Q_LORA
C_KPE = C_CKV + KV_LORA
C_GB = C_KPE + HEAD_PAD
C_GC = C_GB + CONV_DIM
C_CH = C_GC + CONV_DIM
C_GCV = C_CH + CONV_DIM
C_GML = C_GCV + D_MODEL
C_END = C_GML + D_MODEL

VMEM_LIMIT = 56 * 1024 * 1024

F32 = jnp.float32
BF16 = jnp.bfloat16


def _dot(a, b):
    return jnp.dot(a, b, preferred_element_type=F32)


def _dot_nt(a, b):
    return lax.dot_general(a, b, (((1,), (1,)), ((), ())), preferred_element_type=F32)


def _layer_norm(x, g, b):
    mu = jnp.mean(x, axis=-1, keepdims=True)
    xc = x - mu
    var = jnp.mean(xc * xc, axis=-1, keepdims=True)
    return xc * lax.rsqrt(var + LN_EPS) * g + b


def _rms_norm(x, g):
    return x * lax.rsqrt(jnp.mean(x * x, axis=-1, keepdims=True) + RMS_EPS) * g


def _rope_group(x, c, a, b):
    return x * c + pltpu.roll(x, HEAD_PAD - HALF, 1) * a + pltpu.roll(x, HALF, 1) * b


def _proj_front(x_ref, lng_ref, lnb_ref, w_ref, qng_ref, wuq_ref, kvg_ref, tabs):
    qc, qa, qb, kc, ka, kb = [t[...] for t in tabs]
    h = _layer_norm(x_ref[...], lng_ref[...], lnb_ref[...])
    hb = h.astype(BF16)
    cq = _rms_norm(_dot(hb, w_ref[:, C_CQ:C_CKV]), qng_ref[...])
    q = _dot(cq.astype(BF16), wuq_ref[...])
    q_groups = [
        _rope_group(q[:, g * HEAD_PAD:(g + 1) * HEAD_PAD], qc, qa, qb) for g in range(HEADS)
    ]
    ckv = _rms_norm(_dot(hb, w_ref[:, C_CKV:C_KPE]), kvg_ref[...])
    kpe = _rope_group(_dot(hb, w_ref[:, C_KPE:C_GB]), kc, ka, kb)
    return hb, q_groups, ckv, kpe


def _gates(hb, conv_out_fn, w_ref, wco_ref, gcy_ref, gm_ref):
    gate_b = _dot(hb, w_ref[:, C_GB:C_GC])
    u = _dot(hb, w_ref[:, C_GC:C_CH]) * _dot(hb, w_ref[:, C_CH:C_GCV])
    conv_out = conv_out_fn(u)
    y_conv = _dot((gate_b * conv_out).astype(BF16), wco_ref[...])
    gcy_ref[...] = (jax.nn.sigmoid(_dot(hb, w_ref[:, C_GCV:C_GML])) * y_conv).astype(BF16)
    gm_ref[...] = jax.nn.sigmoid(_dot(hb, w_ref[:, C_GML:C_END])).astype(BF16)


def _proj_seq_kernel(x_ref, lng_ref, lnb_ref, w_ref, qng_ref, wuq_ref, kvg_ref, wuk_ref, wuv_ref,
                     cw_ref, wco_ref, qc_ref, qa_ref, qb_ref, kc_ref, ka_ref, kb_ref, init_ref,
                     q_ref, k_ref, v_ref, ckv_ref, kpe_ref, gcy_ref, gm_ref, cst_ref, ubuf):
    t = pl.program_id(0)
    b = pl.program_id(1)
    rows = x_ref.shape[0]
    hb, q_groups, ckv, kpe = _proj_front(
        x_ref, lng_ref, lnb_ref, w_ref, qng_ref, wuq_ref, kvg_ref,
        (qc_ref, qa_ref, qb_ref, kc_ref, ka_ref, kb_ref))
    for g in range(HEADS):
        q_ref[:, g * HEAD_PAD:(g + 1) * HEAD_PAD] = q_groups[g].astype(BF16)
    ckv_ref[...] = ckv
    kpe_ref[...] = kpe
    ckvb = ckv.astype(BF16)
    kn = _dot(ckvb, wuk_ref[...])
    for g in range(HEADS):
        k_ref[:, g * HEAD_PAD:(g + 1) * HEAD_PAD] = (kn[:, g * HEAD_PAD:(g + 1) * HEAD_PAD] + kpe).astype(BF16)
    v_ref[...] = _dot(ckvb, wuv_ref[...]).astype(BF16)

    def conv_out_fn(u):
        @pl.when(t == 0)
        def _():
            ubuf[6:8, :] = init_ref[...]

        @pl.when(t > 0)
        def _():
            ubuf[6:8, :] = cst_ref[b]

        ubuf[8:8 + rows, :] = u
        cst_ref[b] = u[rows - 2:rows, :]
        return (cw_ref[0:1, :] * ubuf[6:6 + rows, :] + cw_ref[1:2, :] * ubuf[7:7 + rows, :]
                + cw_ref[2:3, :] * u)

    _gates(hb, conv_out_fn, w_ref, wco_ref, gcy_ref, gm_ref)


def _proj_sample_kernel(x_ref, lng_ref, lnb_ref, w_ref, qng_ref, wuq_ref, kvg_ref, wukt_ref,
                        cw_ref, wco_ref, qc_ref, qa_ref, qb_ref, kc_ref, ka_ref, kb_ref,
                        s0_ref, s1_ref,
                        qlat_ref, qpe_ref, ckv_ref, kpe_ref, gcy_ref, gm_ref, u_ref, ubuf):
    rows = x_ref.shape[0]
    n_seq = rows // 8
    hb, q_groups, ckv, kpe = _proj_front(
        x_ref, lng_ref, lnb_ref, w_ref, qng_ref, wuq_ref, kvg_ref,
        (qc_ref, qa_ref, qb_ref, kc_ref, ka_ref, kb_ref))
    for g in range(HEADS):
        qg = q_groups[g]
        qlat_ref[:, g] = _dot(qg.astype(BF16), wukt_ref[g]).reshape(n_seq, 8, KV_LORA)
        qpe_ref[:, g] = qg[:, ROPE_LO:ROPE_LO + QK_ROPE].reshape(n_seq, 8, QK_ROPE)
    ckv_ref[...] = ckv
    kpe_ref[...] = kpe

    def conv_out_fn(u):
        u_ref[...] = u
        ubuf[8:8 + rows, :] = u
        ubuf[6:8, :] = jnp.zeros((2, CONV_DIM), F32)
        tok = lax.broadcasted_iota(jnp.int32, (rows, 1), 0) % 8
        um1 = jnp.where(tok == 0, s1_ref[...], ubuf[7:7 + rows, :])
        um2 = jnp.where(tok == 0, s0_ref[...], jnp.where(tok == 1, s1_ref[...], ubuf[6:6 + rows, :]))
        return cw_ref[0:1, :] * um2 + cw_ref[1:2, :] * um1 + cw_ref[2:3, :] * u

    _gates(hb, conv_out_fn, w_ref, wco_ref, gcy_ref, gm_ref)


def _softmax_block_update(s, v, m_ref, l_ref, acc_ref):
    m_prev = m_ref[...]
    m_new = jnp.maximum(m_prev, jnp.max(s, axis=-1, keepdims=True))
    alpha = jnp.exp(m_prev - m_new)
    p = jnp.exp(s - m_new)
    l_ref[...] = alpha * l_ref[...] + jnp.sum(p, axis=-1, keepdims=True)
    acc_ref[...] = alpha * acc_ref[...] + _dot(p.astype(BF16), v)
    m_ref[...] = m_new


def _softmax_block_init(s, v, m_ref, l_ref, acc_ref):
    m = jnp.max(s, axis=-1, keepdims=True)
    p = jnp.exp(s - m)
    m_ref[...] = m
    l_ref[...] = jnp.sum(p, axis=-1, keepdims=True)
    acc_ref[...] = _dot(p.astype(BF16), v)


def _attn_prompt_kernel(q_ref, k_ref, v_ref, km_ref, vm_ref, o_ref, m_ref, l_ref, acc_ref):
    qi = pl.program_id(1)
    tq = q_ref.shape[0]
    row = lax.broadcasted_iota(jnp.int32, (tq, tq), 0)
    col = lax.broadcasted_iota(jnp.int32, (tq, tq), 1)
    lane = lax.broadcasted_iota(jnp.int32, (tq, HEAD_PAD), 1)
    for pair in range(HEADS // 2):
        vsl = slice(pair * HEAD_PAD, (pair + 1) * HEAD_PAD)
        outs = []
        for hh in range(2):
            hsl = slice((2 * pair + hh) * HEAD_PAD, (2 * pair + hh + 1) * HEAD_PAD)
            q = q_ref[:, hsl]
            _softmax_block_init(_dot_nt(q, km_ref[:, hsl]), vm_ref[:, vsl], m_ref, l_ref, acc_ref)

            def body(kb, carry, q=q, hsl=hsl, vsl=vsl):
                k0 = pl.multiple_of(kb * tq, tq)
                s = _dot_nt(q, k_ref[pl.ds(k0, tq), hsl])
                _softmax_block_update(s, v_ref[pl.ds(k0, tq), vsl], m_ref, l_ref, acc_ref)
                return carry

            lax.fori_loop(0, qi, body, 0)
            k0 = pl.multiple_of(qi * tq, tq)
            s = _dot_nt(q, k_ref[pl.ds(k0, tq), hsl])
            s = jnp.where(col <= row, s, -jnp.inf)
            _softmax_block_update(s, v_ref[pl.ds(k0, tq), vsl], m_ref, l_ref, acc_ref)
            outs.append(acc_ref[...] / l_ref[...])
        o_ref[:, vsl] = jnp.where(lane < V_HEAD, outs[0], outs[1]).astype(BF16)


def _attn_sample_kernel(pt_ref, qlat_ref, qpe_ref, ckvn_ref, kpen_ref, *rest, pages_per_step):
    ckv_pages = rest[:pages_per_step]
    kpe_pages = rest[pages_per_step:2 * pages_per_step]
    o_ref, m_ref, l_ref, acc_ref = rest[2 * pages_per_step:]
    j = pl.program_id(1)
    n_rows = HEADS * 8
    ql = qlat_ref[0].reshape(n_rows, KV_LORA).astype(BF16)
    qp = qpe_ref[0].reshape(n_rows, QK_ROPE).astype(BF16)

    @pl.when(j == 0)
    def _():
        cn = ckvn_ref[0].astype(BF16)
        s = _dot_nt(ql, cn) + _dot_nt(qp, kpen_ref[0].astype(BF16))
        tok = lax.broadcasted_iota(jnp.int32, (n_rows, 8), 0) % 8
        key = lax.broadcasted_iota(jnp.int32, (n_rows, 8), 1)
        s = jnp.where(key <= tok, s, -jnp.inf)
        _softmax_block_init(s, cn, m_ref, l_ref, acc_ref)

    for i in range(pages_per_step):
        c = ckv_pages[i][0].astype(BF16)
        s = _dot_nt(ql, c) + _dot_nt(qp, kpe_pages[i][0].astype(BF16))
        _softmax_block_update(s, c, m_ref, l_ref, acc_ref)

    @pl.when(j == pl.num_programs(1) - 1)
    def _():
        o_ref[...] = (acc_ref[...] / l_ref[...]).reshape(HEADS, 8, KV_LORA)


def _post_kernel(*refs, sample):
    if sample:
        (olat_ref, wuvp_ref, x_ref, gcy_ref, gm_ref, lng_ref, lnb_ref, womla_ref, wo_ref,
         l1g_ref, l1b_ref, wup_ref, wdn_ref, l2g_ref, l2b_ref, y_ref) = refs
        pairs = []
        for pair in range(HEADS // 2):
            pairs.append(
                _dot(olat_ref[2 * pair].astype(BF16), wuvp_ref[2 * pair])
                + _dot(olat_ref[2 * pair + 1].astype(BF16), wuvp_ref[2 * pair + 1]))
        y_mla = sum(
            _dot(pairs[pair].astype(BF16), womla_ref[pair * HEAD_PAD:(pair + 1) * HEAD_PAD, :])
            for pair in range(HEADS // 2))
    else:
        (o_ref, x_ref, gcy_ref, gm_ref, lng_ref, lnb_ref, womla_ref, wo_ref,
         l1g_ref, l1b_ref, wup_ref, wdn_ref, l2g_ref, l2b_ref, y_ref) = refs
        y_mla = _dot(o_ref[...], womla_ref[...])
    mix_in = gcy_ref[...].astype(F32) + gm_ref[...].astype(F32) * y_mla
    mix = _dot(mix_in.astype(BF16), wo_ref[...])
    h = _layer_norm(x_ref[...], lng_ref[...], lnb_ref[...])
    h1 = _layer_norm(ALPHA * h + mix, l1g_ref[...], l1b_ref[...])
    h1b = h1.astype(BF16)
    ff = None
    chunk = D_MODEL
    for c in range(D_FF // chunk):
        a = jnp.maximum(_dot(h1b, wup_ref[:, c * chunk:(c + 1) * chunk]), 0.0)
        part = _dot((a * a).astype(BF16), wdn_ref[c * chunk:(c + 1) * chunk, :])
        ff = part if ff is None else ff + part
    y_ref[...] = _layer_norm(ALPHA * h1 + ff, l2g_ref[...], l2b_ref[...])


def _full(shape):
    n = len(shape)
    return pl.BlockSpec(shape, lambda *_: (0,) * n)


def _rope_tables(pos, scale):
    inv_freq = ROPE_THETA ** (-2.0 * jnp.arange(HALF, dtype=F32) / QK_ROPE)
    ang = pos.astype(F32)[:, None] * inv_freq[None, :]
    cos, sin = jnp.cos(ang), jnp.sin(ang)
    n = pos.shape[0]
    ones = jnp.ones((n, ROPE_LO), F32)
    z_lo = jnp.zeros((n, ROPE_LO), F32)
    z_hi = jnp.zeros((n, HEAD_PAD - ROPE_LO - QK_ROPE), F32)
    z_half = jnp.zeros((n, HALF), F32)
    c = jnp.concatenate([ones, cos, cos, z_hi + 1.0], axis=1)
    a = jnp.concatenate([z_lo, -sin, z_half, z_hi], axis=1)
    b = jnp.concatenate([z_lo, z_half, sin, z_hi], axis=1)
    return c * scale, a * scale, b * scale


def _pack_weights(w_in, w_uq, w_uk, w_uv):
    offs = [0]
    for s in (Q_LORA, KV_LORA, QK_ROPE, CONV_DIM, CONV_DIM, CONV_DIM, D_MODEL, D_MODEL):
        offs.append(offs[-1] + s)
    pieces = [w_in[:, offs[i]:offs[i + 1]] for i in range(8)]
    kpe_pad = jnp.pad(pieces[2], ((0, 0), (ROPE_LO, HEAD_PAD - ROPE_LO - QK_ROPE)))
    w_pack = jnp.concatenate(pieces[:2] + [kpe_pad] + pieces[3:], axis=1).astype(BF16)
    head_pad = HEAD_PAD - QK_NOPE - QK_ROPE
    w_uq_p = jnp.pad(w_uq.reshape(Q_LORA, HEADS, QK_NOPE + QK_ROPE), ((0, 0), (0, 0), (0, head_pad)))
    w_uq_p = w_uq_p.reshape(Q_LORA, HEADS * HEAD_PAD).astype(BF16)
    w_uk_p = jnp.pad(w_uk, ((0, 0), (0, 0), (0, HEAD_PAD - QK_NOPE)))
    w_uk_p = w_uk_p.reshape(KV_LORA, HEADS * HEAD_PAD).astype(BF16)
    w_uk_t = jnp.pad(jnp.transpose(w_uk, (1, 2, 0)), ((0, 0), (0, HEAD_PAD - QK_NOPE), (0, 0))).astype(BF16)
    w_uv_f = w_uv.reshape(KV_LORA, HEADS * V_HEAD).astype(BF16)
    w_uv_even = jnp.pad(w_uv, ((0, 0), (0, 0), (0, V_HEAD)))
    w_uv_odd = jnp.pad(w_uv, ((0, 0), (0, 0), (V_HEAD, 0)))
    is_even = (jnp.arange(HEADS) % 2 == 0)[None, :, None]
    w_uv_p = jnp.transpose(jnp.where(is_even, w_uv_even, w_uv_odd), (1, 0, 2)).astype(BF16)
    return w_pack, w_uq_p, w_uk_p, w_uk_t, w_uv_f, w_uv_p


def _proj_seq(x, tabs, init, shared, tile):
    nb, n, _ = x.shape
    nt = n // tile
    lng, lnb, w_pack, qng, w_uq_p, kvg, w_uk_p, w_uv_f, conv_w, w_co = shared
    row_spec = lambda width: pl.BlockSpec((None, tile, width), lambda t, b: (b, t, 0))
    tab_spec = pl.BlockSpec((tile, HEAD_PAD), lambda t, b: (t, 0))
    in_specs = ([row_spec(D_MODEL)] + [_full(a.shape) for a in shared] + [tab_spec] * 6 + [_full(init.shape)])
    out_shape = (
        jax.ShapeDtypeStruct((nb, n, HEADS * HEAD_PAD), BF16),
        jax.ShapeDtypeStruct((nb, n, HEADS * HEAD_PAD), BF16),
        jax.ShapeDtypeStruct((nb, n, HEADS * V_HEAD), BF16),
        jax.ShapeDtypeStruct((nb, n, KV_LORA), F32),
        jax.ShapeDtypeStruct((nb, n, HEAD_PAD), F32),
        jax.ShapeDtypeStruct((nb, n, D_MODEL), BF16),
        jax.ShapeDtypeStruct((nb, n, D_MODEL), BF16),
        jax.ShapeDtypeStruct((nb, CONV_K - 1, CONV_DIM), F32),
    )
    out_specs = (row_spec(HEADS * HEAD_PAD), row_spec(HEADS * HEAD_PAD), row_spec(HEADS * V_HEAD),
                 row_spec(KV_LORA), row_spec(HEAD_PAD), row_spec(D_MODEL), row_spec(D_MODEL),
                 _full((nb, CONV_K - 1, CONV_DIM)))
    return pl.pallas_call(
        _proj_seq_kernel,
        grid=(nt, nb),
        in_specs=in_specs,
        out_specs=out_specs,
        out_shape=out_shape,
        scratch_shapes=[pltpu.VMEM((tile + 8, CONV_DIM), F32)],
        compiler_params=pltpu.CompilerParams(
            dimension_semantics=("arbitrary", "arbitrary"), vmem_limit_bytes=VMEM_LIMIT),
        name="proj_seq",
    )(x, *shared, *tabs, init)


def _proj_sample(x, tabs, s0, s1, shared, tile):
    n = x.shape[0]
    n_seq = n // 8
    row = lambda width: pl.BlockSpec((tile, width), lambda i: (i, 0))
    seq4 = lambda width: pl.BlockSpec((tile // 8, HEADS, 8, width), lambda i: (i, 0, 0, 0))
    ins = (x, *shared, *tabs, s0, s1)
    in_specs = ([row(D_MODEL)] + [_full(a.shape) for a in shared] + [row(HEAD_PAD)] * 6
                + [row(CONV_DIM)] * 2)
    out_specs = (seq4(KV_LORA), seq4(QK_ROPE), row(KV_LORA), row(HEAD_PAD), row(D_MODEL), row(D_MODEL),
                 row(CONV_DIM))
    out_shape = (
        jax.ShapeDtypeStruct((n_seq, HEADS, 8, KV_LORA), F32),
        jax.ShapeDtypeStruct((n_seq, HEADS, 8, QK_ROPE), F32),
        jax.ShapeDtypeStruct((n, KV_LORA), F32),
        jax.ShapeDtypeStruct((n, HEAD_PAD), F32),
        jax.ShapeDtypeStruct((n, D_MODEL), BF16),
        jax.ShapeDtypeStruct((n, D_MODEL), BF16),
        jax.ShapeDtypeStruct((n, CONV_DIM), F32),
    )
    return pl.pallas_call(
        _proj_sample_kernel,
        grid=(n // tile,),
        in_specs=in_specs,
        out_specs=out_specs,
        out_shape=out_shape,
        scratch_shapes=[pltpu.VMEM((tile + 8, CONV_DIM), F32)],
        compiler_params=pltpu.CompilerParams(
            dimension_semantics=("arbitrary",), vmem_limit_bytes=VMEM_LIMIT),
        name="proj_sample",
    )(*ins)


def _attn_prompt(q, k, v, k_meta, v_meta, tq):
    nb, n, _ = q.shape
    return pl.pallas_call(
        _attn_prompt_kernel,
        grid=(nb, n // tq),
        in_specs=[
            pl.BlockSpec((None, tq, HEADS * HEAD_PAD), lambda b, i: (b, i, 0)),
            pl.BlockSpec((None, n, HEADS * HEAD_PAD), lambda b, i: (b, 0, 0)),
            pl.BlockSpec((None, n, HEADS * V_HEAD), lambda b, i: (b, 0, 0)),
            _full(k_meta.shape),
            _full(v_meta.shape),
        ],
        out_specs=pl.BlockSpec((None, tq, HEADS * V_HEAD), lambda b, i: (b, i, 0)),
        out_shape=jax.ShapeDtypeStruct((nb, n, HEADS * V_HEAD), BF16),
        scratch_shapes=[pltpu.VMEM((tq, 1), F32), pltpu.VMEM((tq, 1), F32),
                        pltpu.VMEM((tq, HEAD_PAD), F32)],
        compiler_params=pltpu.CompilerParams(
            dimension_semantics=("arbitrary", "arbitrary"), vmem_limit_bytes=VMEM_LIMIT),
        name="attn_prompt",
    )(q, k, v, k_meta, v_meta)


def _attn_sample(page_table, q_lat, q_pe, ckv_new, kpe_new, ckv_pool, kpe_pool, pages_per_step):
    n_seq, n_pages = page_table.shape
    n_groups = n_pages // pages_per_step

    def page_spec(width, i):
        def index_map(s, j, pt):
            return (pt[s * n_pages + j * pages_per_step + i], 0, 0)
        return pl.BlockSpec((1, PAGE, width), index_map)

    seq4 = lambda width: pl.BlockSpec((1, HEADS, 8, width), lambda s, j, pt: (s, 0, 0, 0))
    seq3 = lambda width: pl.BlockSpec((1, 8, width), lambda s, j, pt: (s, 0, 0))
    in_specs = ([seq4(KV_LORA), seq4(QK_ROPE), seq3(KV_LORA), seq3(QK_ROPE)]
                + [page_spec(KV_LORA, i) for i in range(pages_per_step)]
                + [page_spec(QK_ROPE, i) for i in range(pages_per_step)])
    n_rows = HEADS * 8
    grid_spec = pltpu.PrefetchScalarGridSpec(
        num_scalar_prefetch=1,
        grid=(n_seq, n_groups),
        in_specs=in_specs,
        out_specs=pl.BlockSpec((HEADS, 8, KV_LORA), lambda s, j, pt: (0, s, 0)),
        scratch_shapes=[pltpu.VMEM((n_rows, 1), F32), pltpu.VMEM((n_rows, 1), F32),
                        pltpu.VMEM((n_rows, KV_LORA), F32)],
    )
    return pl.pallas_call(
        functools.partial(_attn_sample_kernel, pages_per_step=pages_per_step),
        grid_spec=grid_spec,
        out_shape=jax.ShapeDtypeStruct((HEADS, n_seq * 8, KV_LORA), F32),
        compiler_params=pltpu.CompilerParams(
            dimension_semantics=("arbitrary", "arbitrary"), vmem_limit_bytes=VMEM_LIMIT),
        name="attn_sample",
    )(page_table.reshape(-1), q_lat, q_pe, ckv_new, kpe_new,
      *([ckv_pool] * pages_per_step), *([kpe_pool] * pages_per_step))


def _post(front, x, gcy, gm, weights, tile, sample):
    n = x.shape[0]
    row = lambda width: pl.BlockSpec((tile, width), lambda i: (i, 0))
    if sample:
        o_lat, w_uv_p = front
        front_specs = [pl.BlockSpec((HEADS, tile, KV_LORA), lambda i: (0, i, 0)), _full(w_uv_p.shape)]
    else:
        front_specs = [row(HEADS * V_HEAD)]
    in_specs = front_specs + [row(D_MODEL), row(D_MODEL), row(D_MODEL)] + [_full(w.shape) for w in weights]
    return pl.pallas_call(
        functools.partial(_post_kernel, sample=sample),
        grid=(n // tile,),
        in_specs=in_specs,
        out_specs=row(D_MODEL),
        out_shape=jax.ShapeDtypeStruct((n, D_MODEL), F32),
        compiler_params=pltpu.CompilerParams(
            dimension_semantics=("arbitrary",), vmem_limit_bytes=VMEM_LIMIT),
        name="post_sample" if sample else "post_prompt",
    )(*front, x, gcy, gm, *weights)


def kernel(x_prompt, x_sample, cache_ckv, cache_kpe, state_conv, page_table, meta_tokens, ln_emb_g, ln_emb_b, w_in, q_norm_g, w_uq, kv_norm_g, w_uk, w_uv, w_o_mla, conv_w, w_conv_out, w_o, ln1_g, ln1_b, w_up, w_down, ln2_g, ln2_b):
    assert w_in.shape[0] == DEPTH
    nb, seq, _ = x_prompt.shape
    n_seq, dec_seq, _ = x_sample.shape
    assert dec_seq == 8
    past_len = page_table.shape[1] * PAGE

    row2 = lambda a: a.reshape(1, -1).astype(F32)
    w_pack, w_uq_p, w_uk_p, w_uk_t, w_uv_f, w_uv_p = _pack_weights(w_in[0], w_uq[0], w_uk[0], w_uv[0])
    w_co = w_conv_out[0].astype(BF16)
    lng, lnb = row2(ln_emb_g), row2(ln_emb_b)
    head = (lng, lnb, w_pack, row2(q_norm_g[0]), w_uq_p, row2(kv_norm_g[0]))
    shared_seq = head + (w_uk_p, w_uv_f, conv_w[0], w_co)
    shared_sample = head + (w_uk_t, conv_w[0], w_co)
    post_w = (lng, lnb, w_o_mla[0].astype(BF16), w_o[0].astype(BF16), row2(ln1_g[0]), row2(ln1_b[0]),
              w_up[0].astype(BF16), w_down[0].astype(BF16), row2(ln2_g[0]), row2(ln2_b[0]))

    def tables(pos):
        return _rope_tables(pos, SM_SCALE) + _rope_tables(pos, 1.0)

    zero_state = jnp.zeros((CONV_K - 1, CONV_DIM), F32)
    _, k_meta, v_meta, ckv_meta, kpe_meta, _, _, conv_meta = _proj_seq(
        meta_tokens[None], tables(jnp.arange(N_META)), zero_state, shared_seq, N_META)

    q, k, v, ckv_p, kpe_p, gcy_p, gm_p, conv_p = _proj_seq(
        x_prompt, tables(N_META + jnp.arange(seq)), conv_meta[0], shared_seq, 512)
    o_p = _attn_prompt(q, k, v, k_meta[0], v_meta[0], 256)
    n_p = nb * seq
    y_prompt = _post((o_p.reshape(n_p, -1),), x_prompt.reshape(n_p, D_MODEL), gcy_p.reshape(n_p, D_MODEL),
                     gm_p.reshape(n_p, D_MODEL), post_w, 512, sample=False).reshape(nb, seq, D_MODEL)

    n_s = n_seq * dec_seq
    tabs_s = tuple(jnp.tile(t, (n_seq, 1)) for t in tables(past_len + jnp.arange(dec_seq)))
    s0 = jnp.repeat(state_conv[0, :, 0], dec_seq, axis=0)
    s1 = jnp.repeat(state_conv[0, :, 1], dec_seq, axis=0)
    xs = x_sample.reshape(n_s, D_MODEL)
    q_lat, q_pe, ckv_s, kpe_s128, gcy_s, gm_s, u_s = _proj_sample(xs, tabs_s, s0, s1, shared_sample, 256)
    kpe_s = kpe_s128[:, ROPE_LO:ROPE_LO + QK_ROPE]
    o_lat = _attn_sample(page_table, q_lat, q_pe, ckv_s.reshape(n_seq, dec_seq, KV_LORA),
                         kpe_s.reshape(n_seq, dec_seq, QK_ROPE), cache_ckv[0], cache_kpe[0], 8)
    y_sample = _post((o_lat, w_uv_p), xs, gcy_s, gm_s, post_w, 512, sample=True).reshape(n_seq, dec_seq, D_MODEL)

    rope_lanes = slice(ROPE_LO, ROPE_LO + QK_ROPE)
    new_ckv_prompt = jnp.concatenate([jnp.broadcast_to(ckv_meta, (nb, N_META, KV_LORA)), ckv_p], axis=1)[None]
    new_kpe_prompt = jnp.concatenate(
        [jnp.broadcast_to(kpe_meta[..., rope_lanes], (nb, N_META, QK_ROPE)), kpe_p[..., rope_lanes]], axis=1)[None]
    new_conv_sample = u_s.reshape(n_seq, dec_seq, CONV_DIM)[:, dec_seq - (CONV_K - 1):][None]
    return (y_prompt, y_sample, new_ckv_prompt, new_kpe_prompt, conv_p[None],
            ckv_s.reshape(n_seq, dec_seq, KV_LORA)[None], kpe_s.reshape(n_seq, dec_seq, QK_ROPE)[None],
            new_conv_sample)
```

```python
import functools

import jax
import jax.numpy as jnp
from jax import lax
from jax.experimental import pallas as pl
from jax.experimental.pallas import tpu as pltpu

D_MODEL = 1024
N_META = 16
HEADS = 8
Q_LORA = 384
KV_LORA = 256
QK_NOPE = 64
QK_ROPE = 32
V_HEAD = 64
CONV_DIM = 512
CONV_K = 3
D_FF = 4096
PAGE = 128
ROPE_THETA = 10000.0
LN_EPS = 1e-5
RMS_EPS = 1e-6
DEPTH = 1
ALPHA = (2.0 * DEPTH) ** 0.25
SM_SCALE = (QK_NOPE + QK_ROPE) ** -0.5

HEAD_PAD = 128
ROPE_LO = QK_NOPE
HALF = QK_ROPE // 2

C_CQ = 0
C_CKV = C_CQ + Q_LORA
C_KPE = C_CKV + KV_LORA
C_GB = C_KPE + HEAD_PAD
C_GC = C_GB + CONV_DIM
C_CH = C_GC + CONV_DIM
C_GCV = C_CH + CONV_DIM
C_GML = C_GCV + D_MODEL
C_END = C_GML + D_MODEL

VMEM_LIMIT = 56 * 1024 * 1024

F32 = jnp.float32
BF16 = jnp.bfloat16


def _dot(a, b):
    return jnp.dot(a, b, preferred_element_type=F32)


def _dot_nt(a, b):
    return lax.dot_general(a, b, (((1,), (1,)), ((), ())), preferred_element_type=F32)


def _layer_norm(x, g, b):
    mu = jnp.mean(x, axis=-1, keepdims=True)
    xc = x - mu
    var = jnp.mean(xc * xc, axis=-1, keepdims=True)
    return xc * lax.rsqrt(var + LN_EPS) * g + b


def _rms_norm(x, g):
    return x * lax.rsqrt(jnp.mean(x * x, axis=-1, keepdims=True) + RMS_EPS) * g


def _rope_group(x, c, a, b):
    return x * c + pltpu.roll(x, HEAD_PAD - HALF, 1) * a + pltpu.roll(x, HALF, 1) * b


def _proj_front(x_ref, lng_ref, lnb_ref, w_ref, qng_ref, wuq_ref, kvg_ref, tabs):
    qc, qa, qb, kc, ka, kb = [t[...] for t in tabs]
    h = _layer_norm(x_ref[...], lng_ref[...], lnb_ref[...])
    hb = h.astype(BF16)
    cq = _rms_norm(_dot(hb, w_ref[:, C_CQ:C_CKV]), qng_ref[...])
    q = _dot(cq.astype(BF16), wuq_ref[...])
    q_groups = [
        _rope_group(q[:, g * HEAD_PAD:(g + 1) * HEAD_PAD], qc, qa, qb) for g in range(HEADS)
    ]
    ckv = _rms_norm(_dot(hb, w_ref[:, C_CKV:C_KPE]), kvg_ref[...])
    kpe = _rope_group(_dot(hb, w_ref[:, C_KPE:C_GB]), kc, ka, kb)
    return hb, q_groups, ckv, kpe


def _gates(hb, conv_out_fn, w_ref, wco_ref, gcy_ref, gm_ref):
    gate_b = _dot(hb, w_ref[:, C_GB:C_GC])
    u = _dot(hb, w_ref[:, C_GC:C_CH]) * _dot(hb, w_ref[:, C_CH:C_GCV])
    conv_out = conv_out_fn(u)
    y_conv = _dot((gate_b * conv_out).astype(BF16), wco_ref[...])
    gcy_ref[...] = (jax.nn.sigmoid(_dot(hb, w_ref[:, C_GCV:C_GML])) * y_conv).astype(BF16)
    gm_ref[...] = jax.nn.sigmoid(_dot(hb, w_ref[:, C_GML:C_END])).astype(BF16)


def _proj_seq_kernel(x_ref, lng_ref, lnb_ref, w_ref, qng_ref, wuq_ref, kvg_ref, wuk_ref, wuvt_ref,
                     cw_ref, wco_ref, qc_ref, qa_ref, qb_ref, kc_ref, ka_ref, kb_ref, init_ref,
                     q_ref, k_ref, vt_ref, ckv_ref, kpe_ref, gcy_ref, gm_ref, cst_ref, ubuf):
    t = pl.program_id(0)
    b = pl.program_id(1)
    rows = x_ref.shape[0]
    hb, q_groups, ckv, kpe = _proj_front(
        x_ref, lng_ref, lnb_ref, w_ref, qng_ref, wuq_ref, kvg_ref,
        (qc_ref, qa_ref, qb_ref, kc_ref, ka_ref, kb_ref))
    for g in range(HEADS):
        q_ref[:, g * HEAD_PAD:(g + 1) * HEAD_PAD] = q_groups[g].astype(BF16)
    ckv_ref[...] = ckv
    kpe_ref[...] = kpe
    ckvb = ckv.astype(BF16)
    kn = _dot(ckvb, wuk_ref[...])
    for g in range(HEADS):
        k_ref[:, g * HEAD_PAD:(g + 1) * HEAD_PAD] = (kn[:, g * HEAD_PAD:(g + 1) * HEAD_PAD] + kpe).astype(BF16)
    vt_ref[...] = _dot_nt(wuvt_ref[...], ckvb).astype(BF16)

    def conv_out_fn(u):
        @pl.when(t == 0)
        def _():
            ubuf[6:8, :] = init_ref[...]

        @pl.when(t > 0)
        def _():
            ubuf[6:8, :] = cst_ref[b]

        ubuf[8:8 + rows, :] = u
        cst_ref[b] = u[rows - 2:rows, :]
        return (cw_ref[0:1, :] * ubuf[6:6 + rows, :] + cw_ref[1:2, :] * ubuf[7:7 + rows, :]
                + cw_ref[2:3, :] * u)

    _gates(hb, conv_out_fn, w_ref, wco_ref, gcy_ref, gm_ref)


def _proj_sample_kernel(x_ref, lng_ref, lnb_ref, w_ref, qng_ref, wuq_ref, kvg_ref, wukt_ref,
                        cw_ref, wco_ref, qc_ref, qa_ref, qb_ref, kc_ref, ka_ref, kb_ref,
                        s0_ref, s1_ref,
                        qlat_ref, qpe_ref, ckv_ref, kpe_ref, gcy_ref, gm_ref, u_ref, ubuf):
    rows = x_ref.shape[0]
    n_seq = rows // 8
    hb, q_groups, ckv, kpe = _proj_front(
        x_ref, lng_ref, lnb_ref, w_ref, qng_ref, wuq_ref, kvg_ref,
        (qc_ref, qa_ref, qb_ref, kc_ref, ka_ref, kb_ref))
    for g in range(HEADS):
        qg = q_groups[g]
        qlat_ref[:, g] = _dot(qg.astype(BF16), wukt_ref[g]).reshape(n_seq, 8, KV_LORA)
        qpe_ref[:, g] = qg[:, ROPE_LO:ROPE_LO + QK_ROPE].reshape(n_seq, 8, QK_ROPE)
    ckv_ref[...] = ckv
    kpe_ref[...] = kpe

    def conv_out_fn(u):
        u_ref[...] = u
        ubuf[8:8 + rows, :] = u
        ubuf[6:8, :] = jnp.zeros((2, CONV_DIM), F32)
        tok = lax.broadcasted_iota(jnp.int32, (rows, 1), 0) % 8
        um1 = jnp.where(tok == 0, s1_ref[...], ubuf[7:7 + rows, :])
        um2 = jnp.where(tok == 0, s0_ref[...], jnp.where(tok == 1, s1_ref[...], ubuf[6:6 + rows, :]))
        return cw_ref[0:1, :] * um2 + cw_ref[1:2, :] * um1 + cw_ref[2:3, :] * u

    _gates(hb, conv_out_fn, w_ref, wco_ref, gcy_ref, gm_ref)


def _attn_prompt_kernel(q_ref, k_ref, vt_ref, km_ref, vmt_ref, o_ref, m_ref, l_ref, acc_ref):
    qi = pl.program_id(1)
    tq = q_ref.shape[0]
    hsl = lambda h: slice(h * HEAD_PAD, (h + 1) * HEAD_PAD)
    vsl = lambda h: slice(h * V_HEAD, (h + 1) * V_HEAD)

    scores = [_dot_nt(km_ref[:, hsl(h)], q_ref[:, hsl(h)]) for h in range(HEADS)]
    probs = []
    for h in range(HEADS):
        m = jnp.max(scores[h], axis=0, keepdims=True)
        p = jnp.exp(scores[h] - m)
        m_ref[h] = m
        l_ref[h] = jnp.sum(p, axis=0, keepdims=True)
        probs.append(p.astype(BF16))
    for h in range(HEADS):
        acc_ref[h] = _dot(vmt_ref[vsl(h), :], probs[h])

    def key_block(kb, mask):
        keys = pl.ds(pl.multiple_of(kb * tq, tq), tq)
        scores = [_dot_nt(k_ref[keys, hsl(h)], q_ref[:, hsl(h)]) for h in range(HEADS)]
        alphas, probs = [], []
        for h in range(HEADS):
            s = scores[h] if mask is None else jnp.where(mask, scores[h], -jnp.inf)
            m_prev = m_ref[h]
            m_new = jnp.maximum(m_prev, jnp.max(s, axis=0, keepdims=True))
            alpha = jnp.exp(m_prev - m_new)
            p = jnp.exp(s - m_new)
            l_ref[h] = alpha * l_ref[h] + jnp.sum(p, axis=0, keepdims=True)
            m_ref[h] = m_new
            alphas.append(alpha)
            probs.append(p.astype(BF16))
        for h in range(HEADS):
            acc_ref[h] = alphas[h] * acc_ref[h] + _dot(vt_ref[vsl(h), keys], probs[h])

    def body(kb, carry):
        key_block(kb, None)
        return carry

    lax.fori_loop(0, qi, body, 0)
    key_idx = lax.broadcasted_iota(jnp.int32, (tq, tq), 0)
    query_idx = lax.broadcasted_iota(jnp.int32, (tq, tq), 1)
    key_block(qi, key_idx <= query_idx)
    o_t = jnp.concatenate([acc_ref[h] / l_ref[h] for h in range(HEADS)], axis=0)
    o_ref[...] = o_t.T.astype(BF16)


def _attn_sample_kernel(pt_ref, qlat_ref, qpe_ref, ckvn_ref, kpen_ref, ckv_hbm, kpet_hbm, o_ref,
                        cbuf, kbuf, csem, ksem, *, n_pages):
    s = pl.program_id(0)
    slot = s % 2

    def page_copies(seq, slot_, p):
        pg = pt_ref[seq * n_pages + p]
        tokens = pl.ds(pl.multiple_of(p * PAGE, PAGE), PAGE)
        return (pltpu.make_async_copy(ckv_hbm.at[pg], cbuf.at[slot_, tokens], csem.at[slot_]),
                pltpu.make_async_copy(kpet_hbm.at[pg], kbuf.at[slot_, :, tokens], ksem.at[slot_]))

    def for_each_page(seq, slot_, fn):
        def body(p, carry):
            for copy in page_copies(seq, slot_, p):
                fn(copy)
            return carry
        lax.fori_loop(0, n_pages, body, 0)

    @pl.when(s == 0)
    def _():
        for_each_page(0, 0, lambda copy: copy.start())

    @pl.when(s + 1 < pl.num_programs(0))
    def _():
        for_each_page(s + 1, 1 - slot, lambda copy: copy.start())

    for_each_page(s, slot, lambda copy: copy.wait())

    n_rows = HEADS * 8
    ql = qlat_ref[0].reshape(n_rows, KV_LORA).astype(BF16)
    qp = qpe_ref[0].reshape(n_rows, QK_ROPE).astype(BF16)
    cn = ckvn_ref[0].astype(BF16)
    s_new = _dot_nt(ql, cn) + _dot_nt(qp, kpen_ref[0].astype(BF16))
    tok = lax.broadcasted_iota(jnp.int32, (n_rows, 8), 0) % 8
    key = lax.broadcasted_iota(jnp.int32, (n_rows, 8), 1)
    s_new = jnp.where(key <= tok, s_new, -jnp.inf)

    c = cbuf[slot].astype(BF16)
    s_past = _dot_nt(ql, c) + _dot(qp, kbuf[slot].astype(BF16))
    m = jnp.maximum(jnp.max(s_past, axis=-1, keepdims=True), jnp.max(s_new, axis=-1, keepdims=True))
    p_past = jnp.exp(s_past - m)
    p_new = jnp.exp(s_new - m)
    denom = jnp.sum(p_past, axis=-1, keepdims=True) + jnp.sum(p_new, axis=-1, keepdims=True)
    acc = _dot(p_past.astype(BF16), c) + _dot(p_new.astype(BF16), cn)
    o_ref[...] = (acc / denom).reshape(HEADS, 8, KV_LORA)


def _post_kernel(*refs, sample):
    if sample:
        (olat_ref, wuvp_ref, x_ref, gcy_ref, gm_ref, lng_ref, lnb_ref, womla_ref, wo_ref,
         l1g_ref, l1b_ref, wup_ref, wdn_ref, l2g_ref, l2b_ref, y_ref) = refs
        pairs = []
        for pair in range(HEADS // 2):
            pairs.append(
                _dot(olat_ref[2 * pair].astype(BF16), wuvp_ref[2 * pair])
                + _dot(olat_ref[2 * pair + 1].astype(BF16), wuvp_ref[2 * pair + 1]))
        y_mla = sum(
            _dot(pairs[pair].astype(BF16), womla_ref[pair * HEAD_PAD:(pair + 1) * HEAD_PAD, :])
            for pair in range(HEADS // 2))
    else:
        (o_ref, x_ref, gcy_ref, gm_ref, lng_ref, lnb_ref, womla_ref, wo_ref,
         l1g_ref, l1b_ref, wup_ref, wdn_ref, l2g_ref, l2b_ref, y_ref) = refs
        y_mla = _dot(o_ref[...], womla_ref[...])
    mix_in = gcy_ref[...].astype(F32) + gm_ref[...].astype(F32) * y_mla
    mix = _dot(mix_in.astype(BF16), wo_ref[...])
    h = _layer_norm(x_ref[...], lng_ref[...], lnb_ref[...])
    h1 = _layer_norm(ALPHA * h + mix, l1g_ref[...], l1b_ref[...])
    h1b = h1.astype(BF16)
    ff = None
    chunk = D_MODEL
    for c in range(D_FF // chunk):
        a = jnp.maximum(_dot(h1b, wup_ref[:, c * chunk:(c + 1) * chunk]), 0.0)
        part = _dot((a * a).astype(BF16), wdn_ref[c * chunk:(c + 1) * chunk, :])
        ff = part if ff is None else ff + part
    y_ref[...] = _layer_norm(ALPHA * h1 + ff, l2g_ref[...], l2b_ref[...])


def _full(shape):
    n = len(shape)
    return pl.BlockSpec(shape, lambda *_: (0,) * n)


def _rope_tables(pos, scale):
    inv_freq = ROPE_THETA ** (-2.0 * jnp.arange(HALF, dtype=F32) / QK_ROPE)
    ang = pos.astype(F32)[:, None] * inv_freq[None, :]
    cos, sin = jnp.cos(ang), jnp.sin(ang)
    n = pos.shape[0]
    ones = jnp.ones((n, ROPE_LO), F32)
    z_lo = jnp.zeros((n, ROPE_LO), F32)
    z_hi = jnp.zeros((n, HEAD_PAD - ROPE_LO - QK_ROPE), F32)
    z_half = jnp.zeros((n, HALF), F32)
    c = jnp.concatenate([ones, cos, cos, z_hi + 1.0], axis=1)
    a = jnp.concatenate([z_lo, -sin, z_half, z_hi], axis=1)
    b = jnp.concatenate([z_lo, z_half, sin, z_hi], axis=1)
    return c * scale, a * scale, b * scale


def _pack_weights(w_in, w_uq, w_uk, w_uv):
    offs = [0]
    for s in (Q_LORA, KV_LORA, QK_ROPE, CONV_DIM, CONV_DIM, CONV_DIM, D_MODEL, D_MODEL):
        offs.append(offs[-1] + s)
    pieces = [w_in[:, offs[i]:offs[i + 1]] for i in range(8)]
    kpe_pad = jnp.pad(pieces[2], ((0, 0), (ROPE_LO, HEAD_PAD - ROPE_LO - QK_ROPE)))
    w_pack = jnp.concatenate(pieces[:2] + [kpe_pad] + pieces[3:], axis=1).astype(BF16)
    head_pad = HEAD_PAD - QK_NOPE - QK_ROPE
    w_uq_p = jnp.pad(w_uq.reshape(Q_LORA, HEADS, QK_NOPE + QK_ROPE), ((0, 0), (0, 0), (0, head_pad)))
    w_uq_p = w_uq_p.reshape(Q_LORA, HEADS * HEAD_PAD).astype(BF16)
    w_uk_p = jnp.pad(w_uk, ((0, 0), (0, 0), (0, HEAD_PAD - QK_NOPE)))
    w_uk_p = w_uk_p.reshape(KV_LORA, HEADS * HEAD_PAD).astype(BF16)
    w_uk_t = jnp.pad(jnp.transpose(w_uk, (1, 2, 0)), ((0, 0), (0, HEAD_PAD - QK_NOPE), (0, 0))).astype(BF16)
    w_uv_t = w_uv.reshape(KV_LORA, HEADS * V_HEAD).T.astype(BF16)
    w_uv_even = jnp.pad(w_uv, ((0, 0), (0, 0), (0, V_HEAD)))
    w_uv_odd = jnp.pad(w_uv, ((0, 0), (0, 0), (V_HEAD, 0)))
    is_even = (jnp.arange(HEADS) % 2 == 0)[None, :, None]
    w_uv_p = jnp.transpose(jnp.where(is_even, w_uv_even, w_uv_odd), (1, 0, 2)).astype(BF16)
    return w_pack, w_uq_p, w_uk_p, w_uk_t, w_uv_t, w_uv_p


def _proj_seq(x, tabs, init, shared, tile):
    nb, n, _ = x.shape
    nt = n // tile
    row_spec = lambda width: pl.BlockSpec((None, tile, width), lambda t, b: (b, t, 0))
    tab_spec = pl.BlockSpec((tile, HEAD_PAD), lambda t, b: (t, 0))
    in_specs = ([row_spec(D_MODEL)] + [_full(a.shape) for a in shared] + [tab_spec] * 6 + [_full(init.shape)])
    out_shape = (
        jax.ShapeDtypeStruct((nb, n, HEADS * HEAD_PAD), BF16),
        jax.ShapeDtypeStruct((nb, n, HEADS * HEAD_PAD), BF16),
        jax.ShapeDtypeStruct((nb, HEADS * V_HEAD, n), BF16),
        jax.ShapeDtypeStruct((nb, n, KV_LORA), F32),
        jax.ShapeDtypeStruct((nb, n, HEAD_PAD), F32),
        jax.ShapeDtypeStruct((nb, n, D_MODEL), BF16),
        jax.ShapeDtypeStruct((nb, n, D_MODEL), BF16),
        jax.ShapeDtypeStruct((nb, CONV_K - 1, CONV_DIM), F32),
    )
    vt_spec = pl.BlockSpec((None, HEADS * V_HEAD, tile), lambda t, b: (b, 0, t))
    out_specs = (row_spec(HEADS * HEAD_PAD), row_spec(HEADS * HEAD_PAD), vt_spec,
                 row_spec(KV_LORA), row_spec(HEAD_PAD), row_spec(D_MODEL), row_spec(D_MODEL),
                 _full((nb, CONV_K - 1, CONV_DIM)))
    return pl.pallas_call(
        _proj_seq_kernel,
        grid=(nt, nb),
        in_specs=in_specs,
        out_specs=out_specs,
        out_shape=out_shape,
        scratch_shapes=[pltpu.VMEM((tile + 8, CONV_DIM), F32)],
        compiler_params=pltpu.CompilerParams(
            dimension_semantics=("arbitrary", "arbitrary"), vmem_limit_bytes=VMEM_LIMIT),
        name="proj_seq",
    )(x, *shared, *tabs, init)


def _proj_sample(x, tabs, s0, s1, shared, tile):
    n = x.shape[0]
    n_seq = n // 8
    row = lambda width: pl.BlockSpec((tile, width), lambda i: (i, 0))
    seq4 = lambda width: pl.BlockSpec((tile // 8, HEADS, 8, width), lambda i: (i, 0, 0, 0))
    ins = (x, *shared, *tabs, s0, s1)
    in_specs = ([row(D_MODEL)] + [_full(a.shape) for a in shared] + [row(HEAD_PAD)] * 6
                + [row(CONV_DIM)] * 2)
    out_specs = (seq4(KV_LORA), seq4(QK_ROPE), row(KV_LORA), row(HEAD_PAD), row(D_MODEL), row(D_MODEL),
                 row(CONV_DIM))
    out_shape = (
        jax.ShapeDtypeStruct((n_seq, HEADS, 8, KV_LORA), F32),
        jax.ShapeDtypeStruct((n_seq, HEADS, 8, QK_ROPE), F32),
        jax.ShapeDtypeStruct((n, KV_LORA), F32),
        jax.ShapeDtypeStruct((n, HEAD_PAD), F32),
        jax.ShapeDtypeStruct((n, D_MODEL), BF16),
        jax.ShapeDtypeStruct((n, D_MODEL), BF16),
        jax.ShapeDtypeStruct((n, CONV_DIM), F32),
    )
    return pl.pallas_call(
        _proj_sample_kernel,
        grid=(n // tile,),
        in_specs=in_specs,
        out_specs=out_specs,
        out_shape=out_shape,
        scratch_shapes=[pltpu.VMEM((tile + 8, CONV_DIM), F32)],
        compiler_params=pltpu.CompilerParams(
            dimension_semantics=("arbitrary",), vmem_limit_bytes=VMEM_LIMIT),
        name="proj_sample",
    )(*ins)


def _attn_prompt(q, k, v_t, k_meta, v_meta_t, tq):
    nb, n, _ = q.shape
    return pl.pallas_call(
        _attn_prompt_kernel,
        grid=(nb, n // tq),
        in_specs=[
            pl.BlockSpec((None, tq, HEADS * HEAD_PAD), lambda b, i: (b, i, 0)),
            pl.BlockSpec((None, n, HEADS * HEAD_PAD), lambda b, i: (b, 0, 0)),
            pl.BlockSpec((None, HEADS * V_HEAD, n), lambda b, i: (b, 0, 0)),
            _full(k_meta.shape),
            _full(v_meta_t.shape),
        ],
        out_specs=pl.BlockSpec((None, tq, HEADS * V_HEAD), lambda b, i: (b, i, 0)),
        out_shape=jax.ShapeDtypeStruct((nb, n, HEADS * V_HEAD), BF16),
        scratch_shapes=[pltpu.VMEM((HEADS, 1, tq), F32), pltpu.VMEM((HEADS, 1, tq), F32),
                        pltpu.VMEM((HEADS, V_HEAD, tq), F32)],
        compiler_params=pltpu.CompilerParams(
            dimension_semantics=("arbitrary", "arbitrary"), vmem_limit_bytes=VMEM_LIMIT),
        name="attn_prompt",
    )(q, k, v_t, k_meta, v_meta_t)


def _attn_sample(page_table, q_lat, q_pe, ckv_new, kpe_new, ckv_pool, kpe_pool_t):
    n_seq, n_pages = page_table.shape
    seq4 = lambda width: pl.BlockSpec((1, HEADS, 8, width), lambda s, pt: (s, 0, 0, 0))
    seq3 = lambda width: pl.BlockSpec((1, 8, width), lambda s, pt: (s, 0, 0))
    hbm = pl.BlockSpec(memory_space=pl.ANY)
    grid_spec = pltpu.PrefetchScalarGridSpec(
        num_scalar_prefetch=1,
        grid=(n_seq,),
        in_specs=[seq4(KV_LORA), seq4(QK_ROPE), seq3(KV_LORA), seq3(QK_ROPE), hbm, hbm],
        out_specs=pl.BlockSpec((HEADS, 8, KV_LORA), lambda s, pt: (0, s, 0)),
        scratch_shapes=[pltpu.VMEM((2, n_pages * PAGE, KV_LORA), F32),
                        pltpu.VMEM((2, QK_ROPE, n_pages * PAGE), F32),
                        pltpu.SemaphoreType.DMA((2,)), pltpu.SemaphoreType.DMA((2,))],
    )
    return pl.pallas_call(
        functools.partial(_attn_sample_kernel, n_pages=n_pages),
        grid_spec=grid_spec,
        out_shape=jax.ShapeDtypeStruct((HEADS, n_seq * 8, KV_LORA), F32),
        compiler_params=pltpu.CompilerParams(
            dimension_semantics=("arbitrary",), vmem_limit_bytes=VMEM_LIMIT),
        name="attn_sample",
    )(page_table.reshape(-1), q_lat, q_pe, ckv_new, kpe_new, ckv_pool, kpe_pool_t)


def _post(front, x, gcy, gm, weights, tile, sample):
    n = x.shape[0]
    row = lambda width: pl.BlockSpec((tile, width), lambda i: (i, 0))
    if sample:
        o_lat, w_uv_p = front
        front_specs = [pl.BlockSpec((HEADS, tile, KV_LORA), lambda i: (0, i, 0)), _full(w_uv_p.shape)]
    else:
        front_specs = [row(HEADS * V_HEAD)]
    in_specs = front_specs + [row(D_MODEL), row(D_MODEL), row(D_MODEL)] + [_full(w.shape) for w in weights]
    return pl.pallas_call(
        functools.partial(_post_kernel, sample=sample),
        grid=(n // tile,),
        in_specs=in_specs,
        out_specs=row(D_MODEL),
        out_shape=jax.ShapeDtypeStruct((n, D_MODEL), F32),
        compiler_params=pltpu.CompilerParams(
            dimension_semantics=("arbitrary",), vmem_limit_bytes=VMEM_LIMIT),
        name="post_sample" if sample else "post_prompt",
    )(*front, x, gcy, gm, *weights)


def kernel(x_prompt, x_sample, cache_ckv, cache_kpe, state_conv, page_table, meta_tokens, ln_emb_g, ln_emb_b, w_in, q_norm_g, w_uq, kv_norm_g, w_uk, w_uv, w_o_mla, conv_w, w_conv_out, w_o, ln1_g, ln1_b, w_up, w_down, ln2_g, ln2_b):
    assert w_in.shape[0] == DEPTH
    nb, seq, _ = x_prompt.shape
    n_seq, dec_seq, _ = x_sample.shape
    assert dec_seq == 8
    past_len = page_table.shape[1] * PAGE

    row2 = lambda a: a.reshape(1, -1).astype(F32)
    w_pack, w_uq_p, w_uk_p, w_uk_t, w_uv_t, w_uv_p = _pack_weights(w_in[0], w_uq[0], w_uk[0], w_uv[0])
    w_co = w_conv_out[0].astype(BF16)
    lng, lnb = row2(ln_emb_g), row2(ln_emb_b)
    head = (lng, lnb, w_pack, row2(q_norm_g[0]), w_uq_p, row2(kv_norm_g[0]))
    shared_seq = head + (w_uk_p, w_uv_t, conv_w[0], w_co)
    shared_sample = head + (w_uk_t, conv_w[0], w_co)
    post_w = (lng, lnb, w_o_mla[0].astype(BF16), w_o[0].astype(BF16), row2(ln1_g[0]), row2(ln1_b[0]),
              w_up[0].astype(BF16), w_down[0].astype(BF16), row2(ln2_g[0]), row2(ln2_b[0]))

    def tables(pos):
        return _rope_tables(pos, SM_SCALE) + _rope_tables(pos, 1.0)

    zero_state = jnp.zeros((CONV_K - 1, CONV_DIM), F32)
    _, k_meta, vt_meta, ckv_meta, kpe_meta, _, _, conv_meta = _proj_seq(
        meta_tokens[None], tables(jnp.arange(N_META)), zero_state, shared_seq, N_META)

    q, k, v_t, ckv_p, kpe_p, gcy_p, gm_p, conv_p = _proj_seq(
        x_prompt, tables(N_META + jnp.arange(seq)), conv_meta[0], shared_seq, 512)
    o_p = _attn_prompt(q, k, v_t, k_meta[0], vt_meta[0], 256)
    n_p = nb * seq
    y_prompt = _post((o_p.reshape(n_p, -1),), x_prompt.reshape(n_p, D_MODEL), gcy_p.reshape(n_p, D_MODEL),
                     gm_p.reshape(n_p, D_MODEL), post_w, 512, sample=False).reshape(nb, seq, D_MODEL)

    n_s = n_seq * dec_seq
    tabs_s = tuple(jnp.tile(t, (n_seq, 1)) for t in tables(past_len + jnp.arange(dec_seq)))
    s0 = jnp.repeat(state_conv[0, :, 0], dec_seq, axis=0)
    s1 = jnp.repeat(state_conv[0, :, 1], dec_seq, axis=0)
    xs = x_sample.reshape(n_s, D_MODEL)
    q_lat, q_pe, ckv_s, kpe_s128, gcy_s, gm_s, u_s = _proj_sample(xs, tabs_s, s0, s1, shared_sample, 256)
    kpe_s = kpe_s128[:, ROPE_LO:ROPE_LO + QK_ROPE]
    o_lat = _attn_sample(page_table, q_lat, q_pe, ckv_s.reshape(n_seq, dec_seq, KV_LORA),
                         kpe_s.reshape(n_seq, dec_seq, QK_ROPE), cache_ckv[0],
                         jnp.swapaxes(cache_kpe[0], 1, 2))
    y_sample = _post((o_lat, w_uv_p), xs, gcy_s, gm_s, post_w, 512, sample=True).reshape(n_seq, dec_seq, D_MODEL)

    rope_lanes = slice(ROPE_LO, ROPE_LO + QK_ROPE)
    new_ckv_prompt = jnp.concatenate([jnp.broadcast_to(ckv_meta, (nb, N_META, KV_LORA)), ckv_p], axis=1)[None]
    new_kpe_prompt = jnp.concatenate(
        [jnp.broadcast_to(kpe_meta[..., rope_lanes], (nb, N_META, QK_ROPE)), kpe_p[..., rope_lanes]], axis=1)[None]
    new_conv_sample = u_s.reshape(n_seq, dec_seq, CONV_DIM)[:, dec_seq - (CONV_K - 1):][None]
    return (y_prompt, y_sample, new_ckv_prompt, new_kpe_prompt, conv_p[None],
            ckv_s.reshape(n_seq, dec_seq, KV_LORA)[None], kpe_s.reshape(n_seq, dec_seq, QK_ROPE)[None],
            new_conv_sample)
```

```python
import functools

import jax
import jax.numpy as jnp
from jax import lax
from jax.experimental import pallas as pl
from jax.experimental.pallas import tpu as pltpu

D_MODEL = 1024
N_META = 16
HEADS = 8
Q_LORA = 384
KV_LORA = 256
QK_NOPE = 64
QK_ROPE = 32
V_HEAD = 64
CONV_DIM = 512
CONV_K = 3
D_FF = 4096
PAGE = 128
ROPE_THETA = 10000.0
LN_EPS = 1e-5
RMS_EPS = 1e-6
DEPTH = 1
ALPHA = (2.0 * DEPTH) ** 0.25
SM_SCALE = (QK_NOPE + QK_ROPE) ** -0.5
LOG2_E = 1.4426950408889634

HEAD_PAD = 128
ROPE_LO = QK_NOPE
HALF = QK_ROPE // 2

C_CQ = 0
C_CKV = C_CQ + Q_LORA
C_KPE = C_CKV + KV_LORA
C_GB = C_KPE + HEAD_PAD
C_GC = C_GB + CONV_DIM
C_CH = C_GC + CONV_DIM
C_GCV = C_CH + CONV_DIM
C_GML = C_GCV + D_MODEL
C_END = C_GML + D_MODEL

VMEM_LIMIT = 56 * 1024 * 1024

F32 = jnp.float32
BF16 = jnp.bfloat16


def _dot(a, b):
    return jnp.dot(a, b, preferred_element_type=F32)


def _dot_nt(a, b):
    return lax.dot_general(a, b, (((1,), (1,)), ((), ())), preferred_element_type=F32)


def _layer_norm(x, g, b):
    mu = jnp.mean(x, axis=-1, keepdims=True)
    xc = x - mu
    var = jnp.mean(xc * xc, axis=-1, keepdims=True)
    return xc * lax.rsqrt(var + LN_EPS) * g + b


def _rms_norm(x, g):
    return x * lax.rsqrt(jnp.mean(x * x, axis=-1, keepdims=True) + RMS_EPS) * g


def _rope_group(x, c, a, b):
    return x * c + pltpu.roll(x, HEAD_PAD - HALF, 1) * a + pltpu.roll(x, HALF, 1) * b


def _proj_front(x_ref, lng_ref, lnb_ref, w_ref, qng_ref, wuq_ref, kvg_ref, tabs):
    qc, qa, qb, kc, ka, kb = [t[...] for t in tabs]
    h = _layer_norm(x_ref[...], lng_ref[...], lnb_ref[...])
    hb = h.astype(BF16)
    cq = _rms_norm(_dot(hb, w_ref[:, C_CQ:C_CKV]), qng_ref[...])
    q = _dot(cq.astype(BF16), wuq_ref[...])
    q_groups = [
        _rope_group(q[:, g * HEAD_PAD:(g + 1) * HEAD_PAD], qc, qa, qb) for g in range(HEADS)
    ]
    ckv = _rms_norm(_dot(hb, w_ref[:, C_CKV:C_KPE]), kvg_ref[...])
    kpe = _rope_group(_dot(hb, w_ref[:, C_KPE:C_GB]), kc, ka, kb)
    return hb, q_groups, ckv, kpe


def _gates(hb, conv_out_fn, w_ref, wco_ref, gcy_ref, gm_ref):
    gate_b = _dot(hb, w_ref[:, C_GB:C_GC])
    u = _dot(hb, w_ref[:, C_GC:C_CH]) * _dot(hb, w_ref[:, C_CH:C_GCV])
    conv_out = conv_out_fn(u)
    y_conv = _dot((gate_b * conv_out).astype(BF16), wco_ref[...])
    gcy_ref[...] = (jax.nn.sigmoid(_dot(hb, w_ref[:, C_GCV:C_GML])) * y_conv).astype(BF16)
    gm_ref[...] = jax.nn.sigmoid(_dot(hb, w_ref[:, C_GML:C_END])).astype(BF16)


def _proj_seq_kernel(x_ref, lng_ref, lnb_ref, w_ref, qng_ref, wuq_ref, kvg_ref, wuk_ref, wuvt_ref,
                     cw_ref, wco_ref, qc_ref, qa_ref, qb_ref, kc_ref, ka_ref, kb_ref, init_ref,
                     q_ref, k_ref, vt_ref, ckv_ref, kpe_ref, gcy_ref, gm_ref, cst_ref, ubuf):
    t = pl.program_id(0)
    b = pl.program_id(1)
    rows = x_ref.shape[0]
    hb, q_groups, ckv, kpe = _proj_front(
        x_ref, lng_ref, lnb_ref, w_ref, qng_ref, wuq_ref, kvg_ref,
        (qc_ref, qa_ref, qb_ref, kc_ref, ka_ref, kb_ref))
    for g in range(HEADS):
        q_ref[:, g * HEAD_PAD:(g + 1) * HEAD_PAD] = q_groups[g].astype(BF16)
    ckv_ref[...] = ckv
    kpe_ref[...] = kpe
    ckvb = ckv.astype(BF16)
    kn = _dot(ckvb, wuk_ref[...])
    for g in range(HEADS):
        k_ref[:, g * HEAD_PAD:(g + 1) * HEAD_PAD] = (kn[:, g * HEAD_PAD:(g + 1) * HEAD_PAD] + kpe).astype(BF16)
    vt_ref[...] = _dot_nt(wuvt_ref[...], ckvb).astype(BF16)

    def conv_out_fn(u):
        @pl.when(t == 0)
        def _():
            ubuf[6:8, :] = init_ref[...]

        @pl.when(t > 0)
        def _():
            ubuf[6:8, :] = cst_ref[b]

        ubuf[8:8 + rows, :] = u
        cst_ref[b] = u[rows - 2:rows, :]
        return (cw_ref[0:1, :] * ubuf[6:6 + rows, :] + cw_ref[1:2, :] * ubuf[7:7 + rows, :]
                + cw_ref[2:3, :] * u)

    _gates(hb, conv_out_fn, w_ref, wco_ref, gcy_ref, gm_ref)


def _proj_sample_kernel(x_ref, lng_ref, lnb_ref, w_ref, qng_ref, wuq_ref, kvg_ref, wukt_ref,
                        cw_ref, wco_ref, qc_ref, qa_ref, qb_ref, kc_ref, ka_ref, kb_ref,
                        s0_ref, s1_ref,
                        qlat_ref, qpe_ref, ckv_ref, kpe_ref, gcy_ref, gm_ref, u_ref, ubuf):
    rows = x_ref.shape[0]
    n_seq = rows // 8
    hb, q_groups, ckv, kpe = _proj_front(
        x_ref, lng_ref, lnb_ref, w_ref, qng_ref, wuq_ref, kvg_ref,
        (qc_ref, qa_ref, qb_ref, kc_ref, ka_ref, kb_ref))
    for g in range(HEADS):
        qg = q_groups[g]
        qlat_ref[:, g] = _dot(qg.astype(BF16), wukt_ref[g]).reshape(n_seq, 8, KV_LORA)
        qpe_ref[:, g] = qg[:, ROPE_LO:ROPE_LO + QK_ROPE].reshape(n_seq, 8, QK_ROPE)
    ckv_ref[...] = ckv
    kpe_ref[...] = kpe

    def conv_out_fn(u):
        u_ref[...] = u
        ubuf[8:8 + rows, :] = u
        ubuf[6:8, :] = jnp.zeros((2, CONV_DIM), F32)
        tok = lax.broadcasted_iota(jnp.int32, (rows, 1), 0) % 8
        um1 = jnp.where(tok == 0, s1_ref[...], ubuf[7:7 + rows, :])
        um2 = jnp.where(tok == 0, s0_ref[...], jnp.where(tok == 1, s1_ref[...], ubuf[6:6 + rows, :]))
        return cw_ref[0:1, :] * um2 + cw_ref[1:2, :] * um1 + cw_ref[2:3, :] * u

    _gates(hb, conv_out_fn, w_ref, wco_ref, gcy_ref, gm_ref)


def _attn_prompt_kernel(q_ref, k_ref, vt_ref, km_ref, vmt_ref, o_ref, m_ref, acc_ref):
    qi = pl.program_id(1)
    tq = q_ref.shape[0]
    hsl = lambda h: slice(h * HEAD_PAD, (h + 1) * HEAD_PAD)
    vsl = lambda h: slice(h * V_HEAD, (h + 1) * V_HEAD)

    def values_and_ones(v_t):
        return jnp.concatenate([v_t, jnp.ones((16, v_t.shape[1]), BF16)], axis=0)


    diag = pl.ds(pl.multiple_of(qi * tq, tq), tq)
    key_idx = lax.broadcasted_iota(jnp.int32, (tq, tq), 0)
    query_idx = lax.broadcasted_iota(jnp.int32, (tq, tq), 1)
    causal = key_idx <= query_idx
    scores = [_dot_nt(jnp.concatenate([k_ref[diag, hsl(h)], km_ref[:, hsl(h)]], axis=0), q_ref[:, hsl(h)])
              for h in range(HEADS)]
    probs = []
    for h in range(HEADS):
        s_diag = jnp.where(causal, scores[h][0:tq], -jnp.inf)
        s_meta = scores[h][tq:tq + N_META]
        m = jnp.maximum(jnp.max(s_diag, axis=0, keepdims=True), jnp.max(s_meta, axis=0, keepdims=True))
        m_ref[h] = m
        probs.append((jnp.exp2(s_diag - m).astype(BF16), jnp.exp2(s_meta - m).astype(BF16)))
    for h in range(HEADS):
        acc_ref[h] = (_dot(values_and_ones(vt_ref[vsl(h), diag]), probs[h][0])
                      + _dot(values_and_ones(vmt_ref[vsl(h), :]), probs[h][1]))

    def visible_keys(k0, n_keys):
        keys = pl.ds(pl.multiple_of(k0, tq), n_keys)
        scores = [_dot_nt(k_ref[keys, hsl(h)], q_ref[:, hsl(h)]) for h in range(HEADS)]
        alphas, probs = [], []
        for h in range(HEADS):
            m_prev = m_ref[h]
            m_new = jnp.maximum(m_prev, jnp.max(scores[h], axis=0, keepdims=True))
            m_ref[h] = m_new
            alphas.append(jnp.exp2(m_prev - m_new))
            probs.append(jnp.exp2(scores[h] - m_new).astype(BF16))
        for h in range(HEADS):
            acc_ref[h] = alphas[h] * acc_ref[h] + _dot(values_and_ones(vt_ref[vsl(h), keys]), probs[h])

    def pair_body(i, carry):
        visible_keys(i * (2 * tq), 2 * tq)
        return carry

    lax.fori_loop(0, qi // 2, pair_body, 0)

    @pl.when(qi % 2 == 1)
    def _():
        visible_keys((qi - 1) * tq, tq)

    o_t = jnp.concatenate(
        [acc_ref[h, 0:V_HEAD] / acc_ref[h, V_HEAD:V_HEAD + 1] for h in range(HEADS)], axis=0)
    o_ref[...] = o_t.T.astype(BF16)


def _attn_sample_kernel(pt_ref, qlat_ref, qpe_ref, ckvn_ref, kpen_ref, ckv_hbm, kpet_hbm, o_ref,
                        cbuf, kbuf, csem, ksem, *, n_pages):
    s = pl.program_id(0)
    slot = s % 2

    def page_copies(seq, slot_, p):
        pg = pt_ref[seq * n_pages + p]
        tokens = pl.ds(p * PAGE, PAGE)
        return (pltpu.make_async_copy(ckv_hbm.at[pg], cbuf.at[slot_, tokens], csem.at[slot_]),
                pltpu.make_async_copy(kpet_hbm.at[pg], kbuf.at[slot_, :, tokens], ksem.at[slot_]))

    def for_each_page(seq, slot_, fn):
        for p in range(n_pages):
            for copy in page_copies(seq, slot_, p):
                fn(copy)

    @pl.when(s == 0)
    def _():
        for_each_page(0, 0, lambda copy: copy.start())

    @pl.when(s + 1 < pl.num_programs(0))
    def _():
        for_each_page(s + 1, 1 - slot, lambda copy: copy.start())

    for_each_page(s, slot, lambda copy: copy.wait())

    n_rows = HEADS * 8
    ql = qlat_ref[0].reshape(n_rows, KV_LORA).astype(BF16)
    qp = qpe_ref[0].reshape(n_rows, QK_ROPE).astype(BF16)
    cn = ckvn_ref[0].astype(BF16)
    s_new = _dot_nt(ql, cn) + _dot_nt(qp, kpen_ref[0].astype(BF16))
    tok = lax.broadcasted_iota(jnp.int32, (n_rows, 8), 0) % 8
    key = lax.broadcasted_iota(jnp.int32, (n_rows, 8), 1)
    s_new = jnp.where(key <= tok, s_new, -jnp.inf)

    c = cbuf[slot].astype(BF16)
    s_past = _dot_nt(ql, c) + _dot(qp, kbuf[slot].astype(BF16))
    m = jnp.maximum(jnp.max(s_past, axis=-1, keepdims=True), jnp.max(s_new, axis=-1, keepdims=True))
    p_past = jnp.exp(s_past - m)
    p_new = jnp.exp(s_new - m)
    denom = jnp.sum(p_past, axis=-1, keepdims=True) + jnp.sum(p_new, axis=-1, keepdims=True)
    acc = _dot(p_past.astype(BF16), c) + _dot(p_new.astype(BF16), cn)
    o_ref[...] = (acc / denom).reshape(HEADS, 8, KV_LORA)


def _post_kernel(*refs, sample):
    if sample:
        (olat_ref, wuvp_ref, x_ref, gcy_ref, gm_ref, lng_ref, lnb_ref, womla_ref, wo_ref,
         l1g_ref, l1b_ref, wup_ref, wdn_ref, l2g_ref, l2b_ref, y_ref) = refs
        pairs = []
        for pair in range(HEADS // 2):
            pairs.append(
                _dot(olat_ref[2 * pair].astype(BF16), wuvp_ref[2 * pair])
                + _dot(olat_ref[2 * pair + 1].astype(BF16), wuvp_ref[2 * pair + 1]))
        y_mla = sum(
            _dot(pairs[pair].astype(BF16), womla_ref[pair * HEAD_PAD:(pair + 1) * HEAD_PAD, :])
            for pair in range(HEADS // 2))
    else:
        (o_ref, x_ref, gcy_ref, gm_ref, lng_ref, lnb_ref, womla_ref, wo_ref,
         l1g_ref, l1b_ref, wup_ref, wdn_ref, l2g_ref, l2b_ref, y_ref) = refs
        y_mla = _dot(o_ref[...], womla_ref[...])
    mix_in = gcy_ref[...].astype(F32) + gm_ref[...].astype(F32) * y_mla
    mix = _dot(mix_in.astype(BF16), wo_ref[...])
    h = _layer_norm(x_ref[...], lng_ref[...], lnb_ref[...])
    h1 = _layer_norm(ALPHA * h + mix, l1g_ref[...], l1b_ref[...])
    h1b = h1.astype(BF16)
    ff = None
    chunk = D_MODEL
    for c in range(D_FF // chunk):
        a = jnp.maximum(_dot(h1b, wup_ref[:, c * chunk:(c + 1) * chunk]), 0.0)
        part = _dot((a * a).astype(BF16), wdn_ref[c * chunk:(c + 1) * chunk, :])
        ff = part if ff is None else ff + part
    y_ref[...] = _layer_norm(ALPHA * h1 + ff, l2g_ref[...], l2b_ref[...])


def _full(shape):
    n = len(shape)
    return pl.BlockSpec(shape, lambda *_: (0,) * n)


def _rope_tables(pos, scale):
    inv_freq = ROPE_THETA ** (-2.0 * jnp.arange(HALF, dtype=F32) / QK_ROPE)
    ang = pos.astype(F32)[:, None] * inv_freq[None, :]
    cos, sin = jnp.cos(ang), jnp.sin(ang)
    n = pos.shape[0]
    ones = jnp.ones((n, ROPE_LO), F32)
    z_lo = jnp.zeros((n, ROPE_LO), F32)
    z_hi = jnp.zeros((n, HEAD_PAD - ROPE_LO - QK_ROPE), F32)
    z_half = jnp.zeros((n, HALF), F32)
    c = jnp.concatenate([ones, cos, cos, z_hi + 1.0], axis=1)
    a = jnp.concatenate([z_lo, -sin, z_half, z_hi], axis=1)
    b = jnp.concatenate([z_lo, z_half, sin, z_hi], axis=1)
    return c * scale, a * scale, b * scale


def _pack_weights(w_in, w_uq, w_uk, w_uv):
    offs = [0]
    for s in (Q_LORA, KV_LORA, QK_ROPE, CONV_DIM, CONV_DIM, CONV_DIM, D_MODEL, D_MODEL):
        offs.append(offs[-1] + s)
    pieces = [w_in[:, offs[i]:offs[i + 1]] for i in range(8)]
    kpe_pad = jnp.pad(pieces[2], ((0, 0), (ROPE_LO, HEAD_PAD - ROPE_LO - QK_ROPE)))
    w_pack = jnp.concatenate(pieces[:2] + [kpe_pad] + pieces[3:], axis=1).astype(BF16)
    head_pad = HEAD_PAD - QK_NOPE - QK_ROPE
    w_uq_p = jnp.pad(w_uq.reshape(Q_LORA, HEADS, QK_NOPE + QK_ROPE), ((0, 0), (0, 0), (0, head_pad)))
    w_uq_p = w_uq_p.reshape(Q_LORA, HEADS * HEAD_PAD).astype(BF16)
    w_uk_p = jnp.pad(w_uk, ((0, 0), (0, 0), (0, HEAD_PAD - QK_NOPE)))
    w_uk_p = w_uk_p.reshape(KV_LORA, HEADS * HEAD_PAD).astype(BF16)
    w_uk_t = jnp.pad(jnp.transpose(w_uk, (1, 2, 0)), ((0, 0), (0, HEAD_PAD - QK_NOPE), (0, 0))).astype(BF16)
    w_uv_t = w_uv.reshape(KV_LORA, HEADS * V_HEAD).T.astype(BF16)
    w_uv_even = jnp.pad(w_uv, ((0, 0), (0, 0), (0, V_HEAD)))
    w_uv_odd = jnp.pad(w_uv, ((0, 0), (0, 0), (V_HEAD, 0)))
    is_even = (jnp.arange(HEADS) % 2 == 0)[None, :, None]
    w_uv_p = jnp.transpose(jnp.where(is_even, w_uv_even, w_uv_odd), (1, 0, 2)).astype(BF16)
    return w_pack, w_uq_p, w_uk_p, w_uk_t, w_uv_t, w_uv_p


def _proj_seq(x, tabs, init, shared, tile):
    nb, n, _ = x.shape
    nt = n // tile
    row_spec = lambda width: pl.BlockSpec((None, tile, width), lambda t, b: (b, t, 0))
    tab_spec = pl.BlockSpec((tile, HEAD_PAD), lambda t, b: (t, 0))
    in_specs = ([row_spec(D_MODEL)] + [_full(a.shape) for a in shared] + [tab_spec] * 6 + [_full(init.shape)])
    out_shape = (
        jax.ShapeDtypeStruct((nb, n, HEADS * HEAD_PAD), BF16),
        jax.ShapeDtypeStruct((nb, n, HEADS * HEAD_PAD), BF16),
        jax.ShapeDtypeStruct((nb, HEADS * V_HEAD, n), BF16),
        jax.ShapeDtypeStruct((nb, n, KV_LORA), F32),
        jax.ShapeDtypeStruct((nb, n, HEAD_PAD), F32),
        jax.ShapeDtypeStruct((nb, n, D_MODEL), BF16),
        jax.ShapeDtypeStruct((nb, n, D_MODEL), BF16),
        jax.ShapeDtypeStruct((nb, CONV_K - 1, CONV_DIM), F32),
    )
    vt_spec = pl.BlockSpec((None, HEADS * V_HEAD, tile), lambda t, b: (b, 0, t))
    out_specs = (row_spec(HEADS * HEAD_PAD), row_spec(HEADS * HEAD_PAD), vt_spec,
                 row_spec(KV_LORA), row_spec(HEAD_PAD), row_spec(D_MODEL), row_spec(D_MODEL),
                 _full((nb, CONV_K - 1, CONV_DIM)))
    return pl.pallas_call(
        _proj_seq_kernel,
        grid=(nt, nb),
        in_specs=in_specs,
        out_specs=out_specs,
        out_shape=out_shape,
        scratch_shapes=[pltpu.VMEM((tile + 8, CONV_DIM), F32)],
        compiler_params=pltpu.CompilerParams(
            dimension_semantics=("arbitrary", "arbitrary"), vmem_limit_bytes=VMEM_LIMIT),
        name="proj_seq",
    )(x, *shared, *tabs, init)


def _proj_sample(x, tabs, s0, s1, shared, tile):
    n = x.shape[0]
    n_seq = n // 8
    row = lambda width: pl.BlockSpec((tile, width), lambda i: (i, 0))
    seq4 = lambda width: pl.BlockSpec((tile // 8, HEADS, 8, width), lambda i: (i, 0, 0, 0))
    ins = (x, *shared, *tabs, s0, s1)
    in_specs = ([row(D_MODEL)] + [_full(a.shape) for a in shared] + [row(HEAD_PAD)] * 6
                + [row(CONV_DIM)] * 2)
    out_specs = (seq4(KV_LORA), seq4(QK_ROPE), row(KV_LORA), row(HEAD_PAD), row(D_MODEL), row(D_MODEL),
                 row(CONV_DIM))
    out_shape = (
        jax.ShapeDtypeStruct((n_seq, HEADS, 8, KV_LORA), F32),
        jax.ShapeDtypeStruct((n_seq, HEADS, 8, QK_ROPE), F32),
        jax.ShapeDtypeStruct((n, KV_LORA), F32),
        jax.ShapeDtypeStruct((n, HEAD_PAD), F32),
        jax.ShapeDtypeStruct((n, D_MODEL), BF16),
        jax.ShapeDtypeStruct((n, D_MODEL), BF16),
        jax.ShapeDtypeStruct((n, CONV_DIM), F32),
    )
    return pl.pallas_call(
        _proj_sample_kernel,
        grid=(n // tile,),
        in_specs=in_specs,
        out_specs=out_specs,
        out_shape=out_shape,
        scratch_shapes=[pltpu.VMEM((tile + 8, CONV_DIM), F32)],
        compiler_params=pltpu.CompilerParams(
            dimension_semantics=("arbitrary",), vmem_limit_bytes=VMEM_LIMIT),
        name="proj_sample",
    )(*ins)


def _attn_prompt(q, k, v_t, k_meta, v_meta_t, tq):
    nb, n, _ = q.shape
    return pl.pallas_call(
        _attn_prompt_kernel,
        grid=(nb, n // tq),
        in_specs=[
            pl.BlockSpec((None, tq, HEADS * HEAD_PAD), lambda b, i: (b, i, 0)),
            pl.BlockSpec((None, n, HEADS * HEAD_PAD), lambda b, i: (b, 0, 0)),
            pl.BlockSpec((None, HEADS * V_HEAD, n), lambda b, i: (b, 0, 0)),
            _full(k_meta.shape),
            _full(v_meta_t.shape),
        ],
        out_specs=pl.BlockSpec((None, tq, HEADS * V_HEAD), lambda b, i: (b, i, 0)),
        out_shape=jax.ShapeDtypeStruct((nb, n, HEADS * V_HEAD), BF16),
        scratch_shapes=[pltpu.VMEM((HEADS, 1, tq), F32), pltpu.VMEM((HEADS, V_HEAD + 16, tq), F32)],
        compiler_params=pltpu.CompilerParams(
            dimension_semantics=("arbitrary", "arbitrary"), vmem_limit_bytes=VMEM_LIMIT),
        name="attn_prompt",
    )(q, k, v_t, k_meta, v_meta_t)


def _attn_sample(page_table, q_lat, q_pe, ckv_new, kpe_new, ckv_pool, kpe_pool_t):
    n_seq, n_pages = page_table.shape
    seq4 = lambda width: pl.BlockSpec((1, HEADS, 8, width), lambda s, pt: (s, 0, 0, 0))
    seq3 = lambda width: pl.BlockSpec((1, 8, width), lambda s, pt: (s, 0, 0))
    hbm = pl.BlockSpec(memory_space=pl.ANY)
    grid_spec = pltpu.PrefetchScalarGridSpec(
        num_scalar_prefetch=1,
        grid=(n_seq,),
        in_specs=[seq4(KV_LORA), seq4(QK_ROPE), seq3(KV_LORA), seq3(QK_ROPE), hbm, hbm],
        out_specs=pl.BlockSpec((HEADS, 8, KV_LORA), lambda s, pt: (0, s, 0)),
        scratch_shapes=[pltpu.VMEM((2, n_pages * PAGE, KV_LORA), F32),
                        pltpu.VMEM((2, QK_ROPE, n_pages * PAGE), F32),
                        pltpu.SemaphoreType.DMA((2,)), pltpu.SemaphoreType.DMA((2,))],
    )
    return pl.pallas_call(
        functools.partial(_attn_sample_kernel, n_pages=n_pages),
        grid_spec=grid_spec,
        out_shape=jax.ShapeDtypeStruct((HEADS, n_seq * 8, KV_LORA), F32),
        compiler_params=pltpu.CompilerParams(
            dimension_semantics=("arbitrary",), vmem_limit_bytes=VMEM_LIMIT),
        name="attn_sample",
    )(page_table.reshape(-1), q_lat, q_pe, ckv_new, kpe_new, ckv_pool, kpe_pool_t)


def _post(front, x, gcy, gm, weights, tile, sample):
    n = x.shape[0]
    row = lambda width: pl.BlockSpec((tile, width), lambda i: (i, 0))
    if sample:
        o_lat, w_uv_p = front
        front_specs = [pl.BlockSpec((HEADS, tile, KV_LORA), lambda i: (0, i, 0)), _full(w_uv_p.shape)]
    else:
        front_specs = [row(HEADS * V_HEAD)]
    in_specs = front_specs + [row(D_MODEL), row(D_MODEL), row(D_MODEL)] + [_full(w.shape) for w in weights]
    return pl.pallas_call(
        functools.partial(_post_kernel, sample=sample),
        grid=(n // tile,),
        in_specs=in_specs,
        out_specs=row(D_MODEL),
        out_shape=jax.ShapeDtypeStruct((n, D_MODEL), F32),
        compiler_params=pltpu.CompilerParams(
            dimension_semantics=("arbitrary",), vmem_limit_bytes=VMEM_LIMIT),
        name="post_sample" if sample else "post_prompt",
    )(*front, x, gcy, gm, *weights)


def kernel(x_prompt, x_sample, cache_ckv, cache_kpe, state_conv, page_table, meta_tokens, ln_emb_g, ln_emb_b, w_in, q_norm_g, w_uq, kv_norm_g, w_uk, w_uv, w_o_mla, conv_w, w_conv_out, w_o, ln1_g, ln1_b, w_up, w_down, ln2_g, ln2_b):
    assert w_in.shape[0] == DEPTH
    nb, seq, _ = x_prompt.shape
    n_seq, dec_seq, _ = x_sample.shape
    assert dec_seq == 8
    past_len = page_table.shape[1] * PAGE

    row2 = lambda a: a.reshape(1, -1).astype(F32)
    w_pack, w_uq_p, w_uk_p, w_uk_t, w_uv_t, w_uv_p = _pack_weights(w_in[0], w_uq[0], w_uk[0], w_uv[0])
    w_co = w_conv_out[0].astype(BF16)
    lng, lnb = row2(ln_emb_g), row2(ln_emb_b)
    head = (lng, lnb, w_pack, row2(q_norm_g[0]), w_uq_p, row2(kv_norm_g[0]))
    shared_seq = head + (w_uk_p, w_uv_t, conv_w[0], w_co)
    shared_sample = head + (w_uk_t, conv_w[0], w_co)
    post_w = (lng, lnb, w_o_mla[0].astype(BF16), w_o[0].astype(BF16), row2(ln1_g[0]), row2(ln1_b[0]),
              w_up[0].astype(BF16), w_down[0].astype(BF16), row2(ln2_g[0]), row2(ln2_b[0]))

    def tables(pos, q_scale):
        return _rope_tables(pos, q_scale) + _rope_tables(pos, 1.0)

    zero_state = jnp.zeros((CONV_K - 1, CONV_DIM), F32)
    _, k_meta, vt_meta, ckv_meta, kpe_meta, _, _, conv_meta = _proj_seq(
        meta_tokens[None], tables(jnp.arange(N_META), SM_SCALE * LOG2_E), zero_state, shared_seq, N_META)

    q, k, v_t, ckv_p, kpe_p, gcy_p, gm_p, conv_p = _proj_seq(
        x_prompt, tables(N_META + jnp.arange(seq), SM_SCALE * LOG2_E), conv_meta[0], shared_seq, 512)
    o_p = _attn_prompt(q, k, v_t, k_meta[0], vt_meta[0], 256)
    n_p = nb * seq
    y_prompt = _post((o_p.reshape(n_p, -1),), x_prompt.reshape(n_p, D_MODEL), gcy_p.reshape(n_p, D_MODEL),
                     gm_p.reshape(n_p, D_MODEL), post_w, 512, sample=False).reshape(nb, seq, D_MODEL)

    n_s = n_seq * dec_seq
    tabs_s = tuple(jnp.tile(t, (n_seq, 1)) for t in tables(past_len + jnp.arange(dec_seq), SM_SCALE))
    s0 = jnp.repeat(state_conv[0, :, 0], dec_seq, axis=0)
    s1 = jnp.repeat(state_conv[0, :, 1], dec_seq, axis=0)
    xs = x_sample.reshape(n_s, D_MODEL)
    q_lat, q_pe, ckv_s, kpe_s128, gcy_s, gm_s, u_s = _proj_sample(xs, tabs_s, s0, s1, shared_sample, 256)
    kpe_s = kpe_s128[:, ROPE_LO:ROPE_LO + QK_ROPE]
    o_lat = _attn_sample(page_table, q_lat, q_pe, ckv_s.reshape(n_seq, dec_seq, KV_LORA),
                         kpe_s.reshape(n_seq, dec_seq, QK_ROPE), cache_ckv[0],
                         jnp.swapaxes(cache_kpe[0], 1, 2))
    y_sample = _post((o_lat, w_uv_p), xs, gcy_s, gm_s, post_w, 512, sample=True).reshape(n_seq, dec_seq, D_MODEL)

    rope_lanes = slice(ROPE_LO, ROPE_LO + QK_ROPE)
    new_ckv_prompt = jnp.concatenate([jnp.broadcast_to(ckv_meta, (nb, N_META, KV_LORA)), ckv_p], axis=1)[None]
    new_kpe_prompt = jnp.concatenate(
        [jnp.broadcast_to(kpe_meta[..., rope_lanes], (nb, N_META, QK_ROPE)), kpe_p[..., rope_lanes]], axis=1)[None]
    new_conv_sample = u_s.reshape(n_seq, dec_seq, CONV_DIM)[:, dec_seq - (CONV_K - 1):][None]
    return (y_prompt, y_sample, new_ckv_prompt, new_kpe_prompt, conv_p[None],
            ckv_s.reshape(n_seq, dec_seq, KV_LORA)[None], kpe_s.reshape(n_seq, dec_seq, QK_ROPE)[None],
            new_conv_sample)
```

```python
import functools

import jax
import jax.numpy as jnp
from jax import lax
from jax.experimental import pallas as pl
from jax.experimental.pallas import tpu as pltpu

D_MODEL = 1024
N_META = 16
HEADS = 8
Q_LORA = 384
KV_LORA = 256
QK_NOPE = 64
QK_ROPE = 32
V_HEAD = 64
CONV_DIM = 512
CONV_K = 3
D_FF = 4096
PAGE = 128
ROPE_THETA = 10000.0
LN_EPS = 1e-5
RMS_EPS = 1e-6
DEPTH = 1
ALPHA = (2.0 * DEPTH) ** 0.25
SM_SCALE = (QK_NOPE + QK_ROPE) ** -0.5
LOG2_E = 1.4426950408889634

SAMPLE_CHUNKS = 4
HEAD_PAD = 128
ROPE_LO = QK_NOPE
HALF = QK_ROPE // 2

C_CQ = 0
C_CKV = C_CQ + Q_LORA
C_KPE = C_CKV + KV_LORA
C_GB = C_KPE + HEAD_PAD
C_GC = C_GB + CONV_DIM
C_CH = C_GC + CONV_DIM
C_GCV = C_CH + CONV_DIM
C_GML = C_GCV + D_MODEL
C_END = C_GML + D_MODEL

VMEM_LIMIT = 56 * 1024 * 1024

F32 = jnp.float32
BF16 = jnp.bfloat16


def _dot(a, b):
    return jnp.dot(a, b, preferred_element_type=F32)


def _dot_nt(a, b):
    return lax.dot_general(a, b, (((1,), (1,)), ((), ())), preferred_element_type=F32)


def _layer_norm(x, g, b):
    mu = jnp.mean(x, axis=-1, keepdims=True)
    xc = x - mu
    var = jnp.mean(xc * xc, axis=-1, keepdims=True)
    return xc * lax.rsqrt(var + LN_EPS) * g + b


def _rms_norm(x, g):
    return x * lax.rsqrt(jnp.mean(x * x, axis=-1, keepdims=True) + RMS_EPS) * g


def _rope_group(x, c, a, b):
    return x * c + pltpu.roll(x, HEAD_PAD - HALF, 1) * a + pltpu.roll(x, HALF, 1) * b


def _proj_front(x_ref, lng_ref, lnb_ref, w_ref, qng_ref, wuq_ref, kvg_ref, tabs):
    qc, qa, qb, kc, ka, kb = [t[...] for t in tabs]
    h = _layer_norm(x_ref[...], lng_ref[...], lnb_ref[...])
    hb = h.astype(BF16)
    cq = _rms_norm(_dot(hb, w_ref[:, C_CQ:C_CKV]), qng_ref[...])
    q = _dot(cq.astype(BF16), wuq_ref[...])
    q_groups = [
        _rope_group(q[:, g * HEAD_PAD:(g + 1) * HEAD_PAD], qc, qa, qb) for g in range(HEADS)
    ]
    ckv = _rms_norm(_dot(hb, w_ref[:, C_CKV:C_KPE]), kvg_ref[...])
    kpe = _rope_group(_dot(hb, w_ref[:, C_KPE:C_GB]), kc, ka, kb)
    return hb, q_groups, ckv, kpe


def _gates(hb, conv_out_fn, w_ref, wco_ref, gcy_ref, gm_ref):
    gate_b = _dot(hb, w_ref[:, C_GB:C_GC])
    u = _dot(hb, w_ref[:, C_GC:C_CH]) * _dot(hb, w_ref[:, C_CH:C_GCV])
    conv_out = conv_out_fn(u)
    y_conv = _dot((gate_b * conv_out).astype(BF16), wco_ref[...])
    gcy_ref[...] = (jax.nn.sigmoid(_dot(hb, w_ref[:, C_GCV:C_GML])) * y_conv).astype(BF16)
    gm_ref[...] = jax.nn.sigmoid(_dot(hb, w_ref[:, C_GML:C_END])).astype(BF16)


def _proj_seq_kernel(x_ref, lng_ref, lnb_ref, w_ref, qng_ref, wuq_ref, kvg_ref, wuk_ref, wuvt_ref,
                     cw_ref, wco_ref, qc_ref, qa_ref, qb_ref, kc_ref, ka_ref, kb_ref, init_ref,
                     q_ref, k_ref, vt_ref, ckv_ref, kpe_ref, gcy_ref, gm_ref, cst_ref, ubuf):
    t = pl.program_id(0)
    b = pl.program_id(1)
    rows = x_ref.shape[0]
    hb, q_groups, ckv, kpe = _proj_front(
        x_ref, lng_ref, lnb_ref, w_ref, qng_ref, wuq_ref, kvg_ref,
        (qc_ref, qa_ref, qb_ref, kc_ref, ka_ref, kb_ref))
    for g in range(HEADS):
        q_ref[:, g * HEAD_PAD:(g + 1) * HEAD_PAD] = q_groups[g].astype(BF16)
    ckv_ref[...] = ckv
    kpe_ref[...] = kpe
    ckvb = ckv.astype(BF16)
    kn = _dot(ckvb, wuk_ref[...])
    for g in range(HEADS):
        k_ref[:, g * HEAD_PAD:(g + 1) * HEAD_PAD] = (kn[:, g * HEAD_PAD:(g + 1) * HEAD_PAD] + kpe).astype(BF16)
    vt_ref[...] = _dot_nt(wuvt_ref[...], ckvb).astype(BF16)

    def conv_out_fn(u):
        @pl.when(t == 0)
        def _():
            ubuf[6:8, :] = init_ref[...]

        @pl.when(t > 0)
        def _():
            ubuf[6:8, :] = cst_ref[b]

        ubuf[8:8 + rows, :] = u
        cst_ref[b] = u[rows - 2:rows, :]
        return (cw_ref[0:1, :] * ubuf[6:6 + rows, :] + cw_ref[1:2, :] * ubuf[7:7 + rows, :]
                + cw_ref[2:3, :] * u)

    _gates(hb, conv_out_fn, w_ref, wco_ref, gcy_ref, gm_ref)


def _proj_sample_kernel(x_ref, lng_ref, lnb_ref, w_ref, qng_ref, wuq_ref, kvg_ref, wukt_ref,
                        cw_ref, wco_ref, qc_ref, qa_ref, qb_ref, kc_ref, ka_ref, kb_ref,
                        s0_ref, s1_ref,
                        qlat_ref, qpe_ref, ckv_ref, kpe_ref, gcy_ref, gm_ref, u_ref, ubuf):
    rows = x_ref.shape[0]
    n_seq = rows // 8
    hb, q_groups, ckv, kpe = _proj_front(
        x_ref, lng_ref, lnb_ref, w_ref, qng_ref, wuq_ref, kvg_ref,
        (qc_ref, qa_ref, qb_ref, kc_ref, ka_ref, kb_ref))
    for g in range(HEADS):
        qg = q_groups[g]
        qlat_ref[:, g] = _dot(qg.astype(BF16), wukt_ref[g]).reshape(n_seq, 8, KV_LORA)
        qpe_ref[:, g] = qg[:, ROPE_LO:ROPE_LO + QK_ROPE].reshape(n_seq, 8, QK_ROPE)
    ckv_ref[...] = ckv
    kpe_ref[...] = kpe

    def conv_out_fn(u):
        u_ref[...] = u
        ubuf[8:8 + rows, :] = u
        ubuf[6:8, :] = jnp.zeros((2, CONV_DIM), F32)
        tok = lax.broadcasted_iota(jnp.int32, (rows, 1), 0) % 8
        um1 = jnp.where(tok == 0, s1_ref[...], ubuf[7:7 + rows, :])
        um2 = jnp.where(tok == 0, s0_ref[...], jnp.where(tok == 1, s1_ref[...], ubuf[6:6 + rows, :]))
        return cw_ref[0:1, :] * um2 + cw_ref[1:2, :] * um1 + cw_ref[2:3, :] * u

    _gates(hb, conv_out_fn, w_ref, wco_ref, gcy_ref, gm_ref)


def _attn_prompt_kernel(q_ref, k_ref, vt_ref, km_ref, vmt_ref, o_ref, m_ref, acc_ref):
    qi = pl.program_id(1)
    tq = q_ref.shape[0]
    hsl = lambda h: slice(h * HEAD_PAD, (h + 1) * HEAD_PAD)
    vsl = lambda h: slice(h * V_HEAD, (h + 1) * V_HEAD)

    def values_and_ones(v_t):
        return jnp.concatenate([v_t, jnp.ones((16, v_t.shape[1]), BF16)], axis=0)


    diag = pl.ds(pl.multiple_of(qi * tq, tq), tq)
    key_idx = lax.broadcasted_iota(jnp.int32, (tq, tq), 0)
    query_idx = lax.broadcasted_iota(jnp.int32, (tq, tq), 1)
    causal = key_idx <= query_idx
    scores = [_dot_nt(jnp.concatenate([k_ref[diag, hsl(h)], km_ref[:, hsl(h)]], axis=0), q_ref[:, hsl(h)])
              for h in range(HEADS)]
    probs = []
    for h in range(HEADS):
        s_diag = jnp.where(causal, scores[h][0:tq], -jnp.inf)
        s_meta = scores[h][tq:tq + N_META]
        m = jnp.maximum(jnp.max(s_diag, axis=0, keepdims=True), jnp.max(s_meta, axis=0, keepdims=True))
        m_ref[h] = m
        probs.append((jnp.exp2(s_diag - m).astype(BF16), jnp.exp2(s_meta - m).astype(BF16)))
    for h in range(HEADS):
        acc_ref[h] = (_dot(values_and_ones(vt_ref[vsl(h), diag]), probs[h][0])
                      + _dot(values_and_ones(vmt_ref[vsl(h), :]), probs[h][1]))

    def visible_keys(k0, n_keys):
        keys = pl.ds(pl.multiple_of(k0, tq), n_keys)
        scores = [_dot_nt(k_ref[keys, hsl(h)], q_ref[:, hsl(h)]) for h in range(HEADS)]
        alphas, probs = [], []
        for h in range(HEADS):
            m_prev = m_ref[h]
            m_new = jnp.maximum(m_prev, jnp.max(scores[h], axis=0, keepdims=True))
            m_ref[h] = m_new
            alphas.append(jnp.exp2(m_prev - m_new))
            probs.append(jnp.exp2(scores[h] - m_new).astype(BF16))
        for h in range(HEADS):
            acc_ref[h] = alphas[h] * acc_ref[h] + _dot(values_and_ones(vt_ref[vsl(h), keys]), probs[h])

    def pair_body(i, carry):
        visible_keys(i * (2 * tq), 2 * tq)
        return carry

    lax.fori_loop(0, qi // 2, pair_body, 0)

    @pl.when(qi % 2 == 1)
    def _():
        visible_keys((qi - 1) * tq, tq)

    o_t = jnp.concatenate(
        [acc_ref[h, 0:V_HEAD] / acc_ref[h, V_HEAD:V_HEAD + 1] for h in range(HEADS)], axis=0)
    o_ref[...] = o_t.T.astype(BF16)


def _attn_sample_kernel(pt_ref, qlat_ref, qpe_ref, ckvn_ref, kpen_ref, ckv_hbm, kpet_hbm, o_ref,
                        cbuf, kbuf, csem, ksem, *, n_pages):
    s = pl.program_id(0)
    slot = s % 2

    def page_copies(seq, slot_, p):
        pg = pt_ref[seq * n_pages + p]
        tokens = pl.ds(p * PAGE, PAGE)
        return (pltpu.make_async_copy(ckv_hbm.at[pg], cbuf.at[slot_, tokens], csem.at[slot_]),
                pltpu.make_async_copy(kpet_hbm.at[pg], kbuf.at[slot_, :, tokens], ksem.at[slot_]))

    def for_each_page(seq, slot_, fn):
        for p in range(n_pages):
            for copy in page_copies(seq, slot_, p):
                fn(copy)

    @pl.when(s == 0)
    def _():
        for_each_page(0, 0, lambda copy: copy.start())

    @pl.when(s + 1 < pl.num_programs(0))
    def _():
        for_each_page(s + 1, 1 - slot, lambda copy: copy.start())

    for_each_page(s, slot, lambda copy: copy.wait())

    n_rows = HEADS * 8
    ql = qlat_ref[0].reshape(n_rows, KV_LORA).astype(BF16)
    qp = qpe_ref[0].reshape(n_rows, QK_ROPE).astype(BF16)
    cn = ckvn_ref[0].astype(BF16)
    s_new = _dot_nt(ql, cn) + _dot_nt(qp, kpen_ref[0].astype(BF16))
    tok = lax.broadcasted_iota(jnp.int32, (n_rows, 8), 0) % 8
    key = lax.broadcasted_iota(jnp.int32, (n_rows, 8), 1)
    s_new = jnp.where(key <= tok, s_new, -jnp.inf)

    ql_pad = jnp.concatenate([ql, jnp.zeros((HEAD_PAD - n_rows, KV_LORA), BF16)], axis=0)
    chunk_keys = (n_pages // SAMPLE_CHUNKS) * PAGE

    def chunk_scores(i):
        keys = slice(i * chunk_keys, (i + 1) * chunk_keys)
        c = cbuf[slot, keys, :].astype(BF16)
        s = _dot_nt(c, ql_pad).T[0:n_rows] + _dot(qp, kbuf[slot, :, keys].astype(BF16))
        return c, s

    m = jnp.max(s_new, axis=-1, keepdims=True)
    p_new = jnp.exp(s_new - m)
    denom = jnp.sum(p_new, axis=-1, keepdims=True)
    acc = _dot(p_new.astype(BF16), cn)
    pending = chunk_scores(0)
    for i in range(SAMPLE_CHUNKS):
        c, s = pending
        if i + 1 < SAMPLE_CHUNKS:
            pending = chunk_scores(i + 1)
        m_new = jnp.maximum(m, jnp.max(s, axis=-1, keepdims=True))
        alpha = jnp.exp(m - m_new)
        p = jnp.exp(s - m_new)
        denom = alpha * denom + jnp.sum(p, axis=-1, keepdims=True)
        acc = alpha * acc + _dot(p.astype(BF16), c)
        m = m_new
    o_ref[...] = (acc / denom).reshape(HEADS, 8, KV_LORA)


def _post_kernel(*refs, sample):
    if sample:
        (olat_ref, wuvp_ref, x_ref, gcy_ref, gm_ref, lng_ref, lnb_ref, womla_ref, wo_ref,
         l1g_ref, l1b_ref, wup_ref, wdn_ref, l2g_ref, l2b_ref, y_ref) = refs
        pairs = []
        for pair in range(HEADS // 2):
            pairs.append(
                _dot(olat_ref[2 * pair].astype(BF16), wuvp_ref[2 * pair])
                + _dot(olat_ref[2 * pair + 1].astype(BF16), wuvp_ref[2 * pair + 1]))
        y_mla = sum(
            _dot(pairs[pair].astype(BF16), womla_ref[pair * HEAD_PAD:(pair + 1) * HEAD_PAD, :])
            for pair in range(HEADS // 2))
    else:
        (o_ref, x_ref, gcy_ref, gm_ref, lng_ref, lnb_ref, womla_ref, wo_ref,
         l1g_ref, l1b_ref, wup_ref, wdn_ref, l2g_ref, l2b_ref, y_ref) = refs
        y_mla = _dot(o_ref[...], womla_ref[...])
    rows = x_ref.shape[0]
    halves = [slice(0, rows // 2), slice(rows // 2, rows)]
    mixes = []
    for r in halves:
        mix_in = gcy_ref[r, :].astype(F32) + gm_ref[r, :].astype(F32) * y_mla[r, :]
        mixes.append(_dot(mix_in.astype(BF16), wo_ref[...]))
    h1s = []
    for r, mix in zip(halves, mixes):
        h = _layer_norm(x_ref[r, :], lng_ref[...], lnb_ref[...])
        h1s.append(_layer_norm(ALPHA * h + mix, l1g_ref[...], l1b_ref[...]))
    h1bs = [h1.astype(BF16) for h1 in h1s]
    ffs = [None, None]
    chunk = D_MODEL
    for c in range(D_FF // chunk):
        acts = [jnp.maximum(_dot(h1b, wup_ref[:, c * chunk:(c + 1) * chunk]), 0.0) for h1b in h1bs]
        for i, a in enumerate(acts):
            part = _dot((a * a).astype(BF16), wdn_ref[c * chunk:(c + 1) * chunk, :])
            ffs[i] = part if ffs[i] is None else ffs[i] + part
    for r, h1, ff in zip(halves, h1s, ffs):
        y_ref[r, :] = _layer_norm(ALPHA * h1 + ff, l2g_ref[...], l2b_ref[...])


def _full(shape):
    n = len(shape)
    return pl.BlockSpec(shape, lambda *_: (0,) * n)


def _rope_tables(pos, scale):
    inv_freq = ROPE_THETA ** (-2.0 * jnp.arange(HALF, dtype=F32) / QK_ROPE)
    ang = pos.astype(F32)[:, None] * inv_freq[None, :]
    cos, sin = jnp.cos(ang), jnp.sin(ang)
    n = pos.shape[0]
    ones = jnp.ones((n, ROPE_LO), F32)
    z_lo = jnp.zeros((n, ROPE_LO), F32)
    z_hi = jnp.zeros((n, HEAD_PAD - ROPE_LO - QK_ROPE), F32)
    z_half = jnp.zeros((n, HALF), F32)
    c = jnp.concatenate([ones, cos, cos, z_hi + 1.0], axis=1)
    a = jnp.concatenate([z_lo, -sin, z_half, z_hi], axis=1)
    b = jnp.concatenate([z_lo, z_half, sin, z_hi], axis=1)
    return c * scale, a * scale, b * scale


def _pack_weights(w_in, w_uq, w_uk, w_uv):
    offs = [0]
    for s in (Q_LORA, KV_LORA, QK_ROPE, CONV_DIM, CONV_DIM, CONV_DIM, D_MODEL, D_MODEL):
        offs.append(offs[-1] + s)
    pieces = [w_in[:, offs[i]:offs[i + 1]] for i in range(8)]
    kpe_pad = jnp.pad(pieces[2], ((0, 0), (ROPE_LO, HEAD_PAD - ROPE_LO - QK_ROPE)))
    w_pack = jnp.concatenate(pieces[:2] + [kpe_pad] + pieces[3:], axis=1).astype(BF16)
    head_pad = HEAD_PAD - QK_NOPE - QK_ROPE
    w_uq_p = jnp.pad(w_uq.reshape(Q_LORA, HEADS, QK_NOPE + QK_ROPE), ((0, 0), (0, 0), (0, head_pad)))
    w_uq_p = w_uq_p.reshape(Q_LORA, HEADS * HEAD_PAD).astype(BF16)
    w_uk_p = jnp.pad(w_uk, ((0, 0), (0, 0), (0, HEAD_PAD - QK_NOPE)))
    w_uk_p = w_uk_p.reshape(KV_LORA, HEADS * HEAD_PAD).astype(BF16)
    w_uk_t = jnp.pad(jnp.transpose(w_uk, (1, 2, 0)), ((0, 0), (0, HEAD_PAD - QK_NOPE), (0, 0))).astype(BF16)
    w_uv_t = w_uv.reshape(KV_LORA, HEADS * V_HEAD).T.astype(BF16)
    w_uv_even = jnp.pad(w_uv, ((0, 0), (0, 0), (0, V_HEAD)))
    w_uv_odd = jnp.pad(w_uv, ((0, 0), (0, 0), (V_HEAD, 0)))
    is_even = (jnp.arange(HEADS) % 2 == 0)[None, :, None]
    w_uv_p = jnp.transpose(jnp.where(is_even, w_uv_even, w_uv_odd), (1, 0, 2)).astype(BF16)
    return w_pack, w_uq_p, w_uk_p, w_uk_t, w_uv_t, w_uv_p


def _proj_seq(x, tabs, init, shared, tile):
    nb, n, _ = x.shape
    nt = n // tile
    row_spec = lambda width: pl.BlockSpec((None, tile, width), lambda t, b: (b, t, 0))
    tab_spec = pl.BlockSpec((tile, HEAD_PAD), lambda t, b: (t, 0))
    in_specs = ([row_spec(D_MODEL)] + [_full(a.shape) for a in shared] + [tab_spec] * 6 + [_full(init.shape)])
    out_shape = (
        jax.ShapeDtypeStruct((nb, n, HEADS * HEAD_PAD), BF16),
        jax.ShapeDtypeStruct((nb, n, HEADS * HEAD_PAD), BF16),
        jax.ShapeDtypeStruct((nb, HEADS * V_HEAD, n), BF16),
        jax.ShapeDtypeStruct((nb, n, KV_LORA), F32),
        jax.ShapeDtypeStruct((nb, n, HEAD_PAD), F32),
        jax.ShapeDtypeStruct((nb, n, D_MODEL), BF16),
        jax.ShapeDtypeStruct((nb, n, D_MODEL), BF16),
        jax.ShapeDtypeStruct((nb, CONV_K - 1, CONV_DIM), F32),
    )
    vt_spec = pl.BlockSpec((None, HEADS * V_HEAD, tile), lambda t, b: (b, 0, t))
    out_specs = (row_spec(HEADS * HEAD_PAD), row_spec(HEADS * HEAD_PAD), vt_spec,
                 row_spec(KV_LORA), row_spec(HEAD_PAD), row_spec(D_MODEL), row_spec(D_MODEL),
                 _full((nb, CONV_K - 1, CONV_DIM)))
    return pl.pallas_call(
        _proj_seq_kernel,
        grid=(nt, nb),
        in_specs=in_specs,
        out_specs=out_specs,
        out_shape=out_shape,
        scratch_shapes=[pltpu.VMEM((tile + 8, CONV_DIM), F32)],
        compiler_params=pltpu.CompilerParams(
            dimension_semantics=("arbitrary", "arbitrary"), vmem_limit_bytes=VMEM_LIMIT),
        name="proj_seq",
    )(x, *shared, *tabs, init)


def _proj_sample(x, tabs, s0, s1, shared, tile):
    n = x.shape[0]
    n_seq = n // 8
    row = lambda width: pl.BlockSpec((tile, width), lambda i: (i, 0))
    seq4 = lambda width: pl.BlockSpec((tile // 8, HEADS, 8, width), lambda i: (i, 0, 0, 0))
    ins = (x, *shared, *tabs, s0, s1)
    in_specs = ([row(D_MODEL)] + [_full(a.shape) for a in shared] + [row(HEAD_PAD)] * 6
                + [row(CONV_DIM)] * 2)
    out_specs = (seq4(KV_LORA), seq4(QK_ROPE), row(KV_LORA), row(HEAD_PAD), row(D_MODEL), row(D_MODEL),
                 row(CONV_DIM))
    out_shape = (
        jax.ShapeDtypeStruct((n_seq, HEADS, 8, KV_LORA), F32),
        jax.ShapeDtypeStruct((n_seq, HEADS, 8, QK_ROPE), F32),
        jax.ShapeDtypeStruct((n, KV_LORA), F32),
        jax.ShapeDtypeStruct((n, HEAD_PAD), F32),
        jax.ShapeDtypeStruct((n, D_MODEL), BF16),
        jax.ShapeDtypeStruct((n, D_MODEL), BF16),
        jax.ShapeDtypeStruct((n, CONV_DIM), F32),
    )
    return pl.pallas_call(
        _proj_sample_kernel,
        grid=(n // tile,),
        in_specs=in_specs,
        out_specs=out_specs,
        out_shape=out_shape,
        scratch_shapes=[pltpu.VMEM((tile + 8, CONV_DIM), F32)],
        compiler_params=pltpu.CompilerParams(
            dimension_semantics=("arbitrary",), vmem_limit_bytes=VMEM_LIMIT),
        name="proj_sample",
    )(*ins)


def _attn_prompt(q, k, v_t, k_meta, v_meta_t, tq):
    nb, n, _ = q.shape
    return pl.pallas_call(
        _attn_prompt_kernel,
        grid=(nb, n // tq),
        in_specs=[
            pl.BlockSpec((None, tq, HEADS * HEAD_PAD), lambda b, i: (b, i, 0)),
            pl.BlockSpec((None, n, HEADS * HEAD_PAD), lambda b, i: (b, 0, 0)),
            pl.BlockSpec((None, HEADS * V_HEAD, n), lambda b, i: (b, 0, 0)),
            _full(k_meta.shape),
            _full(v_meta_t.shape),
        ],
        out_specs=pl.BlockSpec((None, tq, HEADS * V_HEAD), lambda b, i: (b, i, 0)),
        out_shape=jax.ShapeDtypeStruct((nb, n, HEADS * V_HEAD), BF16),
        scratch_shapes=[pltpu.VMEM((HEADS, 1, tq), F32), pltpu.VMEM((HEADS, V_HEAD + 16, tq), F32)],
        compiler_params=pltpu.CompilerParams(
            dimension_semantics=("arbitrary", "arbitrary"), vmem_limit_bytes=VMEM_LIMIT),
        name="attn_prompt",
    )(q, k, v_t, k_meta, v_meta_t)


def _attn_sample(page_table, q_lat, q_pe, ckv_new, kpe_new, ckv_pool, kpe_pool_t):
    n_seq, n_pages = page_table.shape
    seq4 = lambda width: pl.BlockSpec((1, HEADS, 8, width), lambda s, pt: (s, 0, 0, 0))
    seq3 = lambda width: pl.BlockSpec((1, 8, width), lambda s, pt: (s, 0, 0))
    hbm = pl.BlockSpec(memory_space=pl.ANY)
    grid_spec = pltpu.PrefetchScalarGridSpec(
        num_scalar_prefetch=1,
        grid=(n_seq,),
        in_specs=[seq4(KV_LORA), seq4(QK_ROPE), seq3(KV_LORA), seq3(QK_ROPE), hbm, hbm],
        out_specs=pl.BlockSpec((HEADS, 8, KV_LORA), lambda s, pt: (0, s, 0)),
        scratch_shapes=[pltpu.VMEM((2, n_pages * PAGE, KV_LORA), F32),
                        pltpu.VMEM((2, QK_ROPE, n_pages * PAGE), F32),
                        pltpu.SemaphoreType.DMA((2,)), pltpu.SemaphoreType.DMA((2,))],
    )
    return pl.pallas_call(
        functools.partial(_attn_sample_kernel, n_pages=n_pages),
        grid_spec=grid_spec,
        out_shape=jax.ShapeDtypeStruct((HEADS, n_seq * 8, KV_LORA), F32),
        compiler_params=pltpu.CompilerParams(
            dimension_semantics=("arbitrary",), vmem_limit_bytes=VMEM_LIMIT),
        name="attn_sample",
    )(page_table.reshape(-1), q_lat, q_pe, ckv_new, kpe_new, ckv_pool, kpe_pool_t)


def _post(front, x, gcy, gm, weights, tile, sample):
    n = x.shape[0]
    row = lambda width: pl.BlockSpec((tile, width), lambda i: (i, 0))
    if sample:
        o_lat, w_uv_p = front
        front_specs = [pl.BlockSpec((HEADS, tile, KV_LORA), lambda i: (0, i, 0)), _full(w_uv_p.shape)]
    else:
        front_specs = [row(HEADS * V_HEAD)]
    in_specs = front_specs + [row(D_MODEL), row(D_MODEL), row(D_MODEL)] + [_full(w.shape) for w in weights]
    return pl.pallas_call(
        functools.partial(_post_kernel, sample=sample),
        grid=(n // tile,),
        in_specs=in_specs,
        out_specs=row(D_MODEL),
        out_shape=jax.ShapeDtypeStruct((n, D_MODEL), F32),
        compiler_params=pltpu.CompilerParams(
            dimension_semantics=("arbitrary",), vmem_limit_bytes=VMEM_LIMIT),
        name="post_sample" if sample else "post_prompt",
    )(*front, x, gcy, gm, *weights)


def kernel(x_prompt, x_sample, cache_ckv, cache_kpe, state_conv, page_table, meta_tokens, ln_emb_g, ln_emb_b, w_in, q_norm_g, w_uq, kv_norm_g, w_uk, w_uv, w_o_mla, conv_w, w_conv_out, w_o, ln1_g, ln1_b, w_up, w_down, ln2_g, ln2_b):
    assert w_in.shape[0] == DEPTH
    nb, seq, _ = x_prompt.shape
    n_seq, dec_seq, _ = x_sample.shape
    assert dec_seq == 8
    past_len = page_table.shape[1] * PAGE

    row2 = lambda a: a.reshape(1, -1).astype(F32)
    w_pack, w_uq_p, w_uk_p, w_uk_t, w_uv_t, w_uv_p = _pack_weights(w_in[0], w_uq[0], w_uk[0], w_uv[0])
    w_co = w_conv_out[0].astype(BF16)
    lng, lnb = row2(ln_emb_g), row2(ln_emb_b)
    head = (lng, lnb, w_pack, row2(q_norm_g[0]), w_uq_p, row2(kv_norm_g[0]))
    shared_seq = head + (w_uk_p, w_uv_t, conv_w[0], w_co)
    shared_sample = head + (w_uk_t, conv_w[0], w_co)
    post_w = (lng, lnb, w_o_mla[0].astype(BF16), w_o[0].astype(BF16), row2(ln1_g[0]), row2(ln1_b[0]),
              w_up[0].astype(BF16), w_down[0].astype(BF16), row2(ln2_g[0]), row2(ln2_b[0]))

    def tables(pos, q_scale):
        return _rope_tables(pos, q_scale) + _rope_tables(pos, 1.0)

    zero_state = jnp.zeros((CONV_K - 1, CONV_DIM), F32)
    _, k_meta, vt_meta, ckv_meta, kpe_meta, _, _, conv_meta = _proj_seq(
        meta_tokens[None], tables(jnp.arange(N_META), SM_SCALE * LOG2_E), zero_state, shared_seq, N_META)

    q, k, v_t, ckv_p, kpe_p, gcy_p, gm_p, conv_p = _proj_seq(
        x_prompt, tables(N_META + jnp.arange(seq), SM_SCALE * LOG2_E), conv_meta[0], shared_seq, 512)
    o_p = _attn_prompt(q, k, v_t, k_meta[0], vt_meta[0], 256)
    n_p = nb * seq
    y_prompt = _post((o_p.reshape(n_p, -1),), x_prompt.reshape(n_p, D_MODEL), gcy_p.reshape(n_p, D_MODEL),
                     gm_p.reshape(n_p, D_MODEL), post_w, 512, sample=False).reshape(nb, seq, D_MODEL)

    n_s = n_seq * dec_seq
    tabs_s = tuple(jnp.tile(t, (n_seq, 1)) for t in tables(past_len + jnp.arange(dec_seq), SM_SCALE))
    s0 = jnp.repeat(state_conv[0, :, 0], dec_seq, axis=0)
    s1 = jnp.repeat(state_conv[0, :, 1], dec_seq, axis=0)
    xs = x_sample.reshape(n_s, D_MODEL)
    q_lat, q_pe, ckv_s, kpe_s128, gcy_s, gm_s, u_s = _proj_sample(xs, tabs_s, s0, s1, shared_sample, 256)
    kpe_s = kpe_s128[:, ROPE_LO:ROPE_LO + QK_ROPE]
    o_lat = _attn_sample(page_table, q_lat, q_pe, ckv_s.reshape(n_seq, dec_seq, KV_LORA),
                         kpe_s.reshape(n_seq, dec_seq, QK_ROPE), cache_ckv[0],
                         jnp.swapaxes(cache_kpe[0], 1, 2))
    y_sample = _post((o_lat, w_uv_p), xs, gcy_s, gm_s, post_w, 512, sample=True).reshape(n_seq, dec_seq, D_MODEL)

    rope_lanes = slice(ROPE_LO, ROPE_LO + QK_ROPE)
    new_ckv_prompt = jnp.concatenate([jnp.broadcast_to(ckv_meta, (nb, N_META, KV_LORA)), ckv_p], axis=1)[None]
    new_kpe_prompt = jnp.concatenate(
        [jnp.broadcast_to(kpe_meta[..., rope_lanes], (nb, N_META, QK_ROPE)), kpe_p[..., rope_lanes]], axis=1)[None]
    new_conv_sample = u_s.reshape(n_seq, dec_seq, CONV_DIM)[:, dec_seq - (CONV_K - 1):][None]
    return (y_prompt, y_sample, new_ckv_prompt, new_kpe_prompt, conv_p[None],
            ckv_s.reshape(n_seq, dec_seq, KV_LORA)[None], kpe_s.reshape(n_seq, dec_seq, QK_ROPE)[None],
            new_conv_sample)
```

```python
import functools

import jax
import jax.numpy as jnp
from jax import lax
from jax.experimental import pallas as pl
from jax.experimental.pallas import tpu as pltpu

D_MODEL = 1024
N_META = 16
HEADS = 8
Q_LORA = 384
KV_LORA = 256
QK_NOPE = 64
QK_ROPE = 32
V_HEAD = 64
CONV_DIM = 512
CONV_K = 3
D_FF = 4096
PAGE = 128
ROPE_THETA = 10000.0
LN_EPS = 1e-5
RMS_EPS = 1e-6
DEPTH = 1
ALPHA = (2.0 * DEPTH) ** 0.25
SM_SCALE = (QK_NOPE + QK_ROPE) ** -0.5
LOG2_E = 1.4426950408889634

SAMPLE_CHUNKS = 4
HEAD_PAD = 128
ROPE_LO = QK_NOPE
HALF = QK_ROPE // 2

C_CQ = 0
C_CKV = C_CQ + Q_LORA
C_KPE = C_CKV + KV_LORA
C_GB = C_KPE + HEAD_PAD
C_GC = C_GB + CONV_DIM
C_CH = C_GC + CONV_DIM
C_GCV = C_CH + CONV_DIM
C_GML = C_GCV + D_MODEL
C_END = C_GML + D_MODEL

VMEM_LIMIT = 56 * 1024 * 1024

F32 = jnp.float32
BF16 = jnp.bfloat16


def _dot(a, b):
    return jnp.dot(a, b, preferred_element_type=F32)


def _dot_nt(a, b):
    return lax.dot_general(a, b, (((1,), (1,)), ((), ())), preferred_element_type=F32)


def _layer_norm(x, g, b):
    mu = jnp.mean(x, axis=-1, keepdims=True)
    xc = x - mu
    var = jnp.mean(xc * xc, axis=-1, keepdims=True)
    return xc * lax.rsqrt(var + LN_EPS) * g + b


def _rms_norm(x, g):
    return x * lax.rsqrt(jnp.mean(x * x, axis=-1, keepdims=True) + RMS_EPS) * g


def _rope_group(x, c, a, b):
    return x * c + pltpu.roll(x, HEAD_PAD - HALF, 1) * a + pltpu.roll(x, HALF, 1) * b


def _proj_front(x_ref, lng_ref, lnb_ref, w_ref, qng_ref, wuq_ref, kvg_ref, tabs):
    qc, qa, qb, kc, ka, kb = [t[...] for t in tabs]
    h = _layer_norm(x_ref[...], lng_ref[...], lnb_ref[...])
    hb = h.astype(BF16)
    cq = _rms_norm(_dot(hb, w_ref[:, C_CQ:C_CKV]), qng_ref[...])
    q = _dot(cq.astype(BF16), wuq_ref[...])
    q_groups = [
        _rope_group(q[:, g * HEAD_PAD:(g + 1) * HEAD_PAD], qc, qa, qb) for g in range(HEADS)
    ]
    ckv = _rms_norm(_dot(hb, w_ref[:, C_CKV:C_KPE]), kvg_ref[...])
    kpe = _rope_group(_dot(hb, w_ref[:, C_KPE:C_GB]), kc, ka, kb)
    return hb, q_groups, ckv, kpe


def _gates(hb, conv_out_fn, w_ref, wco_ref, gcy_ref, gm_ref):
    gate_b = _dot(hb, w_ref[:, C_GB:C_GC])
    u = _dot(hb, w_ref[:, C_GC:C_CH]) * _dot(hb, w_ref[:, C_CH:C_GCV])
    conv_out = conv_out_fn(u)
    y_conv = _dot((gate_b * conv_out).astype(BF16), wco_ref[...])
    gcy_ref[...] = (jax.nn.sigmoid(_dot(hb, w_ref[:, C_GCV:C_GML])) * y_conv).astype(BF16)
    gm_ref[...] = jax.nn.sigmoid(_dot(hb, w_ref[:, C_GML:C_END])).astype(BF16)


def _proj_seq_kernel(x_ref, lng_ref, lnb_ref, w_ref, qng_ref, wuq_ref, kvg_ref, wuk_ref, wuvt_ref,
                     cw_ref, wco_ref, qc_ref, qa_ref, qb_ref, kc_ref, ka_ref, kb_ref, init_ref, ckv_base_ref,
                     q_ref, k_ref, vt_ref, ckv_ref, kpet_ref, gcy_ref, gm_ref, cst_ref, ubuf):
    del ckv_base_ref
    t = pl.program_id(0)
    b = pl.program_id(1)
    rows = x_ref.shape[0]

    @pl.when(t == 0)
    def _():
        ubuf[6:8, :] = init_ref[...]

    @pl.when(t > 0)
    def _():
        ubuf[6:8, :] = cst_ref[b]

    hb = _layer_norm(x_ref[...], lng_ref[...], lnb_ref[...]).astype(BF16)
    z_cq = _dot(hb, w_ref[:, C_CQ:C_CKV])
    z_ckv = _dot(hb, w_ref[:, C_CKV:C_KPE])
    z_kpe = _dot(hb, w_ref[:, C_KPE:C_GB])
    gate_b = _dot(hb, w_ref[:, C_GB:C_GC])
    gate_c = _dot(hb, w_ref[:, C_GC:C_CH])
    conv_h = _dot(hb, w_ref[:, C_CH:C_GCV])
    z_gcv = _dot(hb, w_ref[:, C_GCV:C_GML])
    z_gml = _dot(hb, w_ref[:, C_GML:C_END])
    q = _dot(_rms_norm(z_cq, qng_ref[...]).astype(BF16), wuq_ref[...])
    ckv = _rms_norm(z_ckv, kvg_ref[...])
    ckvb = ckv.astype(BF16)
    kn = _dot(ckvb, wuk_ref[...])
    v_t = _dot_nt(wuvt_ref[...], ckvb)
    u = gate_c * conv_h
    ubuf[8:8 + rows, :] = u
    cst_ref[b] = u[rows - 2:rows, :]
    conv_out = (cw_ref[0:1, :] * ubuf[6:6 + rows, :] + cw_ref[1:2, :] * ubuf[7:7 + rows, :]
                + cw_ref[2:3, :] * u)
    y_conv = _dot((gate_b * conv_out).astype(BF16), wco_ref[...])

    kpe = _rope_group(z_kpe, kc_ref[...], ka_ref[...], kb_ref[...])
    for g in range(HEADS):
        group = slice(g * HEAD_PAD, (g + 1) * HEAD_PAD)
        q_ref[:, group] = _rope_group(q[:, group], qc_ref[...], qa_ref[...], qb_ref[...]).astype(BF16)
        k_ref[:, group] = (kn[:, group] + kpe).astype(BF16)
    ckv_ref[...] = ckv.reshape(ckv_ref.shape)
    kpet_ref[...] = kpe.T[ROPE_LO:ROPE_LO + QK_ROPE, :]
    vt_ref[...] = v_t.astype(BF16)
    gcy_ref[...] = (jax.nn.sigmoid(z_gcv) * y_conv).astype(BF16)
    gm_ref[...] = jax.nn.sigmoid(z_gml).astype(BF16)


def _proj_sample_kernel(x_ref, lng_ref, lnb_ref, w_ref, qng_ref, wuq_ref, kvg_ref, wukt_ref,
                        cw_ref, wco_ref, qc_ref, qa_ref, qb_ref, kc_ref, ka_ref, kb_ref,
                        s0_ref, s1_ref,
                        qlat_ref, qpe_ref, ckv_ref, kpe_ref, gcy_ref, gm_ref, u_ref, ubuf):
    rows = x_ref.shape[0]
    n_seq = rows // 8
    hb, q_groups, ckv, kpe = _proj_front(
        x_ref, lng_ref, lnb_ref, w_ref, qng_ref, wuq_ref, kvg_ref,
        (qc_ref, qa_ref, qb_ref, kc_ref, ka_ref, kb_ref))
    for g in range(HEADS):
        qg = q_groups[g]
        qlat_ref[:, g] = _dot(qg.astype(BF16), wukt_ref[g]).reshape(n_seq, 8, KV_LORA)
        qpe_ref[:, g] = qg[:, ROPE_LO:ROPE_LO + QK_ROPE].reshape(n_seq, 8, QK_ROPE)
    ckv_ref[...] = ckv
    kpe_ref[...] = kpe

    def conv_out_fn(u):
        u_ref[...] = u
        ubuf[8:8 + rows, :] = u
        ubuf[6:8, :] = jnp.zeros((2, CONV_DIM), F32)
        tok = lax.broadcasted_iota(jnp.int32, (rows, 1), 0) % 8
        um1 = jnp.where(tok == 0, s1_ref[...], ubuf[7:7 + rows, :])
        um2 = jnp.where(tok == 0, s0_ref[...], jnp.where(tok == 1, s1_ref[...], ubuf[6:6 + rows, :]))
        return cw_ref[0:1, :] * um2 + cw_ref[1:2, :] * um1 + cw_ref[2:3, :] * u

    _gates(hb, conv_out_fn, w_ref, wco_ref, gcy_ref, gm_ref)


def _attn_prompt_kernel(q_ref, k_ref, vt_ref, km_ref, vmt_ref, o_ref, m_ref, acc_ref):
    qi = pl.program_id(1)
    tq = q_ref.shape[0]
    hsl = lambda h: slice(h * HEAD_PAD, (h + 1) * HEAD_PAD)
    vsl = lambda h: slice(h * V_HEAD, (h + 1) * V_HEAD)

    def values_and_ones(v_t):
        return jnp.concatenate([v_t, jnp.ones((16, v_t.shape[1]), BF16)], axis=0)


    diag = pl.ds(pl.multiple_of(qi * tq, tq), tq)
    key_idx = lax.broadcasted_iota(jnp.int32, (tq, tq), 0)
    query_idx = lax.broadcasted_iota(jnp.int32, (tq, tq), 1)
    causal = key_idx <= query_idx
    scores = [_dot_nt(jnp.concatenate([k_ref[diag, hsl(h)], km_ref[:, hsl(h)]], axis=0), q_ref[:, hsl(h)])
              for h in range(HEADS)]
    probs = []
    for h in range(HEADS):
        s_diag = jnp.where(causal, scores[h][0:tq], -jnp.inf)
        s_meta = scores[h][tq:tq + N_META]
        m = jnp.maximum(jnp.max(s_diag, axis=0, keepdims=True), jnp.max(s_meta, axis=0, keepdims=True))
        m_ref[h] = m
        probs.append((jnp.exp2(s_diag - m).astype(BF16), jnp.exp2(s_meta - m).astype(BF16)))
    for h in range(HEADS):
        acc_ref[h] = (_dot(values_and_ones(vt_ref[vsl(h), diag]), probs[h][0])
                      + _dot(values_and_ones(vmt_ref[vsl(h), :]), probs[h][1]))

    def visible_keys(k0, n_keys):
        keys = pl.ds(pl.multiple_of(k0, tq), n_keys)
        scores = [_dot_nt(k_ref[keys, hsl(h)], q_ref[:, hsl(h)]) for h in range(HEADS)]
        alphas, probs = [], []
        for h in range(HEADS):
            m_prev = m_ref[h]
            m_new = jnp.maximum(m_prev, jnp.max(scores[h], axis=0, keepdims=True))
            m_ref[h] = m_new
            alphas.append(jnp.exp2(m_prev - m_new))
            probs.append(jnp.exp2(scores[h] - m_new).astype(BF16))
        for h in range(HEADS):
            acc_ref[h] = alphas[h] * acc_ref[h] + _dot(values_and_ones(vt_ref[vsl(h), keys]), probs[h])

    def pair_body(i, carry):
        visible_keys(i * (2 * tq), 2 * tq)
        return carry

    lax.fori_loop(0, qi // 2, pair_body, 0)

    @pl.when(qi % 2 == 1)
    def _():
        visible_keys((qi - 1) * tq, tq)

    o_t = jnp.concatenate(
        [acc_ref[h, 0:V_HEAD] / acc_ref[h, V_HEAD:V_HEAD + 1] for h in range(HEADS)], axis=0)
    o_ref[...] = o_t.T.astype(BF16)


def _attn_sample_kernel(pt_ref, qlat_ref, qpe_ref, ckvn_ref, kpen_ref, ckv_hbm, kpet_hbm, o_ref,
                        cbuf, kbuf, csem, ksem, *, n_pages):
    s = pl.program_id(0)
    slot = s % 2

    def page_copies(seq, slot_, p):
        pg = pt_ref[seq * n_pages + p]
        tokens = pl.ds(p * PAGE, PAGE)
        return (pltpu.make_async_copy(ckv_hbm.at[pg], cbuf.at[slot_, tokens], csem.at[slot_]),
                pltpu.make_async_copy(kpet_hbm.at[pg], kbuf.at[slot_, :, tokens], ksem.at[slot_]))

    def for_each_page(seq, slot_, fn):
        for p in range(n_pages):
            for copy in page_copies(seq, slot_, p):
                fn(copy)

    @pl.when(s == 0)
    def _():
        for_each_page(0, 0, lambda copy: copy.start())

    @pl.when(s + 1 < pl.num_programs(0))
    def _():
        for_each_page(s + 1, 1 - slot, lambda copy: copy.start())

    for_each_page(s, slot, lambda copy: copy.wait())

    n_rows = HEADS * 8
    ql = qlat_ref[0].reshape(n_rows, KV_LORA).astype(BF16)
    qp = qpe_ref[0].reshape(n_rows, QK_ROPE).astype(BF16)
    cn = ckvn_ref[0].astype(BF16)
    s_new = _dot_nt(ql, cn) + _dot_nt(qp, kpen_ref[0].astype(BF16))
    tok = lax.broadcasted_iota(jnp.int32, (n_rows, 8), 0) % 8
    key = lax.broadcasted_iota(jnp.int32, (n_rows, 8), 1)
    s_new = jnp.where(key <= tok, s_new, -jnp.inf)

    ql_pad = jnp.concatenate([ql, jnp.zeros((HEAD_PAD - n_rows, KV_LORA), BF16)], axis=0)
    chunk_keys = (n_pages // SAMPLE_CHUNKS) * PAGE

    def chunk_scores(i):
        keys = slice(i * chunk_keys, (i + 1) * chunk_keys)
        c = cbuf[slot, keys, :].astype(BF16)
        s = _dot_nt(c, ql_pad).T[0:n_rows] + _dot(qp, kbuf[slot, :, keys].astype(BF16))
        return c, s

    m = jnp.max(s_new, axis=-1, keepdims=True)
    p_new = jnp.exp(s_new - m)
    denom = jnp.sum(p_new, axis=-1, keepdims=True)
    acc = _dot(p_new.astype(BF16), cn)
    pending = chunk_scores(0)
    for i in range(SAMPLE_CHUNKS):
        c, s = pending
        if i + 1 < SAMPLE_CHUNKS:
            pending = chunk_scores(i + 1)
        m_new = jnp.maximum(m, jnp.max(s, axis=-1, keepdims=True))
        alpha = jnp.exp(m - m_new)
        p = jnp.exp(s - m_new)
        denom = alpha * denom + jnp.sum(p, axis=-1, keepdims=True)
        acc = alpha * acc + _dot(p.astype(BF16), c)
        m = m_new
    o_ref[...] = (acc / denom).reshape(HEADS, 8, KV_LORA)


def _post_kernel(*refs, sample):
    if sample:
        (olat_ref, wuvp_ref, x_ref, gcy_ref, gm_ref, lng_ref, lnb_ref, womla_ref, wo_ref,
         l1g_ref, l1b_ref, wup_ref, wdn_ref, l2g_ref, l2b_ref, y_ref) = refs
        pairs = []
        for pair in range(HEADS // 2):
            pairs.append(
                _dot(olat_ref[2 * pair].astype(BF16), wuvp_ref[2 * pair])
                + _dot(olat_ref[2 * pair + 1].astype(BF16), wuvp_ref[2 * pair + 1]))
        y_mla = sum(
            _dot(pairs[pair].astype(BF16), womla_ref[pair * HEAD_PAD:(pair + 1) * HEAD_PAD, :])
            for pair in range(HEADS // 2))
    else:
        (o_ref, x_ref, gcy_ref, gm_ref, lng_ref, lnb_ref, womla_ref, wo_ref,
         l1g_ref, l1b_ref, wup_ref, wdn_ref, l2g_ref, l2b_ref, y_ref) = refs
        y_mla = _dot(o_ref[...], womla_ref[...])
    rows = x_ref.shape[0]
    halves = [slice(0, rows // 2), slice(rows // 2, rows)]
    mixes = []
    for r in halves:
        mix_in = gcy_ref[r, :].astype(F32) + gm_ref[r, :].astype(F32) * y_mla[r, :]
        mixes.append(_dot(mix_in.astype(BF16), wo_ref[...]))
    h1s = []
    for r, mix in zip(halves, mixes):
        h = _layer_norm(x_ref[r, :], lng_ref[...], lnb_ref[...])
        h1s.append(_layer_norm(ALPHA * h + mix, l1g_ref[...], l1b_ref[...]))
    h1bs = [h1.astype(BF16) for h1 in h1s]
    ffs = [None, None]
    chunk = D_MODEL
    for c in range(D_FF // chunk):
        acts = [jnp.maximum(_dot(h1b, wup_ref[:, c * chunk:(c + 1) * chunk]), 0.0) for h1b in h1bs]
        for i, a in enumerate(acts):
            part = _dot((a * a).astype(BF16), wdn_ref[c * chunk:(c + 1) * chunk, :])
            ffs[i] = part if ffs[i] is None else ffs[i] + part
    for r, h1, ff in zip(halves, h1s, ffs):
        y_ref[r, :] = _layer_norm(ALPHA * h1 + ff, l2g_ref[...], l2b_ref[...])


def _full(shape):
    n = len(shape)
    return pl.BlockSpec(shape, lambda *_: (0,) * n)


def _rope_tables(pos, scale):
    inv_freq = ROPE_THETA ** (-2.0 * jnp.arange(HALF, dtype=F32) / QK_ROPE)
    ang = pos.astype(F32)[:, None] * inv_freq[None, :]
    cos, sin = jnp.cos(ang), jnp.sin(ang)
    n = pos.shape[0]
    ones = jnp.ones((n, ROPE_LO), F32)
    z_lo = jnp.zeros((n, ROPE_LO), F32)
    z_hi = jnp.zeros((n, HEAD_PAD - ROPE_LO - QK_ROPE), F32)
    z_half = jnp.zeros((n, HALF), F32)
    c = jnp.concatenate([ones, cos, cos, z_hi + 1.0], axis=1)
    a = jnp.concatenate([z_lo, -sin, z_half, z_hi], axis=1)
    b = jnp.concatenate([z_lo, z_half, sin, z_hi], axis=1)
    return c * scale, a * scale, b * scale


def _pack_weights(w_in, w_uq, w_uk, w_uv):
    offs = [0]
    for s in (Q_LORA, KV_LORA, QK_ROPE, CONV_DIM, CONV_DIM, CONV_DIM, D_MODEL, D_MODEL):
        offs.append(offs[-1] + s)
    pieces = [w_in[:, offs[i]:offs[i + 1]] for i in range(8)]
    kpe_pad = jnp.pad(pieces[2], ((0, 0), (ROPE_LO, HEAD_PAD - ROPE_LO - QK_ROPE)))
    w_pack = jnp.concatenate(pieces[:2] + [kpe_pad] + pieces[3:], axis=1).astype(BF16)
    head_pad = HEAD_PAD - QK_NOPE - QK_ROPE
    w_uq_p = jnp.pad(w_uq.reshape(Q_LORA, HEADS, QK_NOPE + QK_ROPE), ((0, 0), (0, 0), (0, head_pad)))
    w_uq_p = w_uq_p.reshape(Q_LORA, HEADS * HEAD_PAD).astype(BF16)
    w_uk_p = jnp.pad(w_uk, ((0, 0), (0, 0), (0, HEAD_PAD - QK_NOPE)))
    w_uk_p = w_uk_p.reshape(KV_LORA, HEADS * HEAD_PAD).astype(BF16)
    w_uk_t = jnp.pad(jnp.transpose(w_uk, (1, 2, 0)), ((0, 0), (0, HEAD_PAD - QK_NOPE), (0, 0))).astype(BF16)
    w_uv_t = w_uv.reshape(KV_LORA, HEADS * V_HEAD).T.astype(BF16)
    w_uv_even = jnp.pad(w_uv, ((0, 0), (0, 0), (0, V_HEAD)))
    w_uv_odd = jnp.pad(w_uv, ((0, 0), (0, 0), (V_HEAD, 0)))
    is_even = (jnp.arange(HEADS) % 2 == 0)[None, :, None]
    w_uv_p = jnp.transpose(jnp.where(is_even, w_uv_even, w_uv_odd), (1, 0, 2)).astype(BF16)
    return w_pack, w_uq_p, w_uk_p, w_uk_t, w_uv_t, w_uv_p


def _proj_seq(x, tabs, init, shared, tile, ckv_base):
    nb, n, _ = x.shape
    row_offset = ckv_base.shape[1] - n
    nt = n // tile
    row_spec = lambda width: pl.BlockSpec((None, tile, width), lambda t, b: (b, t, 0))
    tab_spec = pl.BlockSpec((tile, HEAD_PAD), lambda t, b: (t, 0))
    ckv_spec = pl.BlockSpec((pl.Element(1), pl.Element(tile), pl.Element(KV_LORA)),
                            lambda t, b: (b, pl.multiple_of(row_offset + t * tile, 8), 0))
    in_specs = ([row_spec(D_MODEL)] + [_full(a.shape) for a in shared] + [tab_spec] * 6 + [_full(init.shape)]
                + [pl.BlockSpec(memory_space=pl.ANY)])
    ins = (x, *shared, *tabs, init, ckv_base)
    out_shape = (
        jax.ShapeDtypeStruct((nb, n, HEADS * HEAD_PAD), BF16),
        jax.ShapeDtypeStruct((nb, n, HEADS * HEAD_PAD), BF16),
        jax.ShapeDtypeStruct((nb, HEADS * V_HEAD, n), BF16),
        jax.ShapeDtypeStruct(ckv_base.shape, F32),
        jax.ShapeDtypeStruct((nb, QK_ROPE, n), F32),
        jax.ShapeDtypeStruct((nb, n, D_MODEL), BF16),
        jax.ShapeDtypeStruct((nb, n, D_MODEL), BF16),
        jax.ShapeDtypeStruct((nb, CONV_K - 1, CONV_DIM), F32),
    )
    col_spec = lambda height: pl.BlockSpec((None, height, tile), lambda t, b: (b, 0, t))
    out_specs = (row_spec(HEADS * HEAD_PAD), row_spec(HEADS * HEAD_PAD), col_spec(HEADS * V_HEAD),
                 ckv_spec, col_spec(QK_ROPE), row_spec(D_MODEL), row_spec(D_MODEL),
                 _full((nb, CONV_K - 1, CONV_DIM)))
    return pl.pallas_call(
        _proj_seq_kernel,
        grid=(nt, nb),
        in_specs=in_specs,
        out_specs=out_specs,
        out_shape=out_shape,
        scratch_shapes=[pltpu.VMEM((tile + 8, CONV_DIM), F32)],
        input_output_aliases={len(ins) - 1: 3},
        compiler_params=pltpu.CompilerParams(
            dimension_semantics=("arbitrary", "arbitrary"), vmem_limit_bytes=VMEM_LIMIT),
        name="proj_seq",
    )(*ins)


def _proj_sample(x, tabs, s0, s1, shared, tile):
    n = x.shape[0]
    n_seq = n // 8
    row = lambda width: pl.BlockSpec((tile, width), lambda i: (i, 0))
    seq4 = lambda width: pl.BlockSpec((tile // 8, HEADS, 8, width), lambda i: (i, 0, 0, 0))
    ins = (x, *shared, *tabs, s0, s1)
    in_specs = ([row(D_MODEL)] + [_full(a.shape) for a in shared] + [row(HEAD_PAD)] * 6
                + [row(CONV_DIM)] * 2)
    out_specs = (seq4(KV_LORA), seq4(QK_ROPE), row(KV_LORA), row(HEAD_PAD), row(D_MODEL), row(D_MODEL),
                 row(CONV_DIM))
    out_shape = (
        jax.ShapeDtypeStruct((n_seq, HEADS, 8, KV_LORA), F32),
        jax.ShapeDtypeStruct((n_seq, HEADS, 8, QK_ROPE), F32),
        jax.ShapeDtypeStruct((n, KV_LORA), F32),
        jax.ShapeDtypeStruct((n, HEAD_PAD), F32),
        jax.ShapeDtypeStruct((n, D_MODEL), BF16),
        jax.ShapeDtypeStruct((n, D_MODEL), BF16),
        jax.ShapeDtypeStruct((n, CONV_DIM), F32),
    )
    return pl.pallas_call(
        _proj_sample_kernel,
        grid=(n // tile,),
        in_specs=in_specs,
        out_specs=out_specs,
        out_shape=out_shape,
        scratch_shapes=[pltpu.VMEM((tile + 8, CONV_DIM), F32)],
        compiler_params=pltpu.CompilerParams(
            dimension_semantics=("arbitrary",), vmem_limit_bytes=VMEM_LIMIT),
        name="proj_sample",
    )(*ins)


def _attn_prompt(q, k, v_t, k_meta, v_meta_t, tq):
    nb, n, _ = q.shape
    return pl.pallas_call(
        _attn_prompt_kernel,
        grid=(nb, n // tq),
        in_specs=[
            pl.BlockSpec((None, tq, HEADS * HEAD_PAD), lambda b, i: (b, i, 0)),
            pl.BlockSpec((None, n, HEADS * HEAD_PAD), lambda b, i: (b, 0, 0)),
            pl.BlockSpec((None, HEADS * V_HEAD, n), lambda b, i: (b, 0, 0)),
            _full(k_meta.shape),
            _full(v_meta_t.shape),
        ],
        out_specs=pl.BlockSpec((None, tq, HEADS * V_HEAD), lambda b, i: (b, i, 0)),
        out_shape=jax.ShapeDtypeStruct((nb, n, HEADS * V_HEAD), BF16),
        scratch_shapes=[pltpu.VMEM((HEADS, 1, tq), F32), pltpu.VMEM((HEADS, V_HEAD + 16, tq), F32)],
        compiler_params=pltpu.CompilerParams(
            dimension_semantics=("arbitrary", "arbitrary"), vmem_limit_bytes=VMEM_LIMIT),
        name="attn_prompt",
    )(q, k, v_t, k_meta, v_meta_t)


def _attn_sample(page_table, q_lat, q_pe, ckv_new, kpe_new, ckv_pool, kpe_pool_t):
    n_seq, n_pages = page_table.shape
    seq4 = lambda width: pl.BlockSpec((1, HEADS, 8, width), lambda s, pt: (s, 0, 0, 0))
    seq3 = lambda width: pl.BlockSpec((1, 8, width), lambda s, pt: (s, 0, 0))
    hbm = pl.BlockSpec(memory_space=pl.ANY)
    grid_spec = pltpu.PrefetchScalarGridSpec(
        num_scalar_prefetch=1,
        grid=(n_seq,),
        in_specs=[seq4(KV_LORA), seq4(QK_ROPE), seq3(KV_LORA), seq3(QK_ROPE), hbm, hbm],
        out_specs=pl.BlockSpec((HEADS, 8, KV_LORA), lambda s, pt: (0, s, 0)),
        scratch_shapes=[pltpu.VMEM((2, n_pages * PAGE, KV_LORA), F32),
                        pltpu.VMEM((2, QK_ROPE, n_pages * PAGE), F32),
                        pltpu.SemaphoreType.DMA((2,)), pltpu.SemaphoreType.DMA((2,))],
    )
    return pl.pallas_call(
        functools.partial(_attn_sample_kernel, n_pages=n_pages),
        grid_spec=grid_spec,
        out_shape=jax.ShapeDtypeStruct((HEADS, n_seq * 8, KV_LORA), F32),
        compiler_params=pltpu.CompilerParams(
            dimension_semantics=("arbitrary",), vmem_limit_bytes=VMEM_LIMIT),
        name="attn_sample",
    )(page_table.reshape(-1), q_lat, q_pe, ckv_new, kpe_new, ckv_pool, kpe_pool_t)


def _post(front, x, gcy, gm, weights, tile, sample):
    n = x.shape[0]
    row = lambda width: pl.BlockSpec((tile, width), lambda i: (i, 0))
    if sample:
        o_lat, w_uv_p = front
        front_specs = [pl.BlockSpec((HEADS, tile, KV_LORA), lambda i: (0, i, 0)), _full(w_uv_p.shape)]
    else:
        front_specs = [row(HEADS * V_HEAD)]
    in_specs = front_specs + [row(D_MODEL), row(D_MODEL), row(D_MODEL)] + [_full(w.shape) for w in weights]
    return pl.pallas_call(
        functools.partial(_post_kernel, sample=sample),
        grid=(n // tile,),
        in_specs=in_specs,
        out_specs=row(D_MODEL),
        out_shape=jax.ShapeDtypeStruct((n, D_MODEL), F32),
        compiler_params=pltpu.CompilerParams(
            dimension_semantics=("arbitrary",), vmem_limit_bytes=VMEM_LIMIT),
        name="post_sample" if sample else "post_prompt",
    )(*front, x, gcy, gm, *weights)


def kernel(x_prompt, x_sample, cache_ckv, cache_kpe, state_conv, page_table, meta_tokens, ln_emb_g, ln_emb_b, w_in, q_norm_g, w_uq, kv_norm_g, w_uk, w_uv, w_o_mla, conv_w, w_conv_out, w_o, ln1_g, ln1_b, w_up, w_down, ln2_g, ln2_b):
    assert w_in.shape[0] == DEPTH
    nb, seq, _ = x_prompt.shape
    n_seq, dec_seq, _ = x_sample.shape
    assert dec_seq == 8
    past_len = page_table.shape[1] * PAGE

    row2 = lambda a: a.reshape(1, -1).astype(F32)
    w_pack, w_uq_p, w_uk_p, w_uk_t, w_uv_t, w_uv_p = _pack_weights(w_in[0], w_uq[0], w_uk[0], w_uv[0])
    w_co = w_conv_out[0].astype(BF16)
    lng, lnb = row2(ln_emb_g), row2(ln_emb_b)
    head = (lng, lnb, w_pack, row2(q_norm_g[0]), w_uq_p, row2(kv_norm_g[0]))
    shared_seq = head + (w_uk_p, w_uv_t, conv_w[0], w_co)
    shared_sample = head + (w_uk_t, conv_w[0], w_co)
    post_w = (lng, lnb, w_o_mla[0].astype(BF16), w_o[0].astype(BF16), row2(ln1_g[0]), row2(ln1_b[0]),
              w_up[0].astype(BF16), w_down[0].astype(BF16), row2(ln2_g[0]), row2(ln2_b[0]))

    def tables(pos, q_scale):
        return _rope_tables(pos, q_scale) + _rope_tables(pos, 1.0)

    zero_state = jnp.zeros((CONV_K - 1, CONV_DIM), F32)
    _, k_meta, vt_meta, ckv_meta, kpet_meta, _, _, conv_meta = _proj_seq(
        meta_tokens[None], tables(jnp.arange(N_META), SM_SCALE * LOG2_E), zero_state, shared_seq, N_META,
        jnp.zeros((1, N_META, KV_LORA), F32))

    q, k, v_t, ckv_p, kpet_p, gcy_p, gm_p, conv_p = _proj_seq(
        x_prompt, tables(N_META + jnp.arange(seq), SM_SCALE * LOG2_E), conv_meta[0], shared_seq, 512,
        jnp.pad(jnp.broadcast_to(ckv_meta, (nb, N_META, KV_LORA)), ((0, 0), (0, seq), (0, 0))))
    o_p = _attn_prompt(q, k, v_t, k_meta[0], vt_meta[0], 256)
    n_p = nb * seq
    y_prompt = _post((o_p.reshape(n_p, -1),), x_prompt.reshape(n_p, D_MODEL), gcy_p.reshape(n_p, D_MODEL),
                     gm_p.reshape(n_p, D_MODEL), post_w, 512, sample=False).reshape(nb, seq, D_MODEL)

    n_s = n_seq * dec_seq
    tabs_s = tuple(jnp.tile(t, (n_seq, 1)) for t in tables(past_len + jnp.arange(dec_seq), SM_SCALE))
    s0 = jnp.repeat(state_conv[0, :, 0], dec_seq, axis=0)
    s1 = jnp.repeat(state_conv[0, :, 1], dec_seq, axis=0)
    xs = x_sample.reshape(n_s, D_MODEL)
    q_lat, q_pe, ckv_s, kpe_s128, gcy_s, gm_s, u_s = _proj_sample(xs, tabs_s, s0, s1, shared_sample, 256)
    kpe_s = kpe_s128[:, ROPE_LO:ROPE_LO + QK_ROPE]
    o_lat = _attn_sample(page_table, q_lat, q_pe, ckv_s.reshape(n_seq, dec_seq, KV_LORA),
                         kpe_s.reshape(n_seq, dec_seq, QK_ROPE), cache_ckv[0],
                         jnp.swapaxes(cache_kpe[0], 1, 2))
    y_sample = _post((o_lat, w_uv_p), xs, gcy_s, gm_s, post_w, 512, sample=True).reshape(n_seq, dec_seq, D_MODEL)

    new_kpe_prompt = jnp.swapaxes(
        jnp.concatenate([jnp.broadcast_to(kpet_meta, (nb, QK_ROPE, N_META)), kpet_p], axis=2), 1, 2)[None]
    new_conv_sample = u_s.reshape(n_seq, dec_seq, CONV_DIM)[:, dec_seq - (CONV_K - 1):][None]
    return (y_prompt, y_sample, ckv_p[None], new_kpe_prompt, conv_p[None],
            ckv_s.reshape(n_seq, dec_seq, KV_LORA)[None], kpe_s.reshape(n_seq, dec_seq, QK_ROPE)[None],
            new_conv_sample)
```

```python
import functools

import jax
import jax.numpy as jnp
from jax import lax
from jax.experimental import pallas as pl
from jax.experimental.pallas import tpu as pltpu

D_MODEL = 1024
N_META = 16
HEADS = 8
Q_LORA = 384
KV_LORA = 256
QK_NOPE = 64
QK_ROPE = 32
V_HEAD = 64
CONV_DIM = 512
CONV_K = 3
D_FF = 4096
PAGE = 128
ROPE_THETA = 10000.0
LN_EPS = 1e-5
RMS_EPS = 1e-6
DEPTH = 1
ALPHA = (2.0 * DEPTH) ** 0.25
SM_SCALE = (QK_NOPE + QK_ROPE) ** -0.5
LOG2_E = 1.4426950408889634

SAMPLE_CHUNKS = 4
SOFTMAX_LAG = 2
VALUES_LAG = 1
HEAD_PAD = 128
ROPE_LO = QK_NOPE
HALF = QK_ROPE // 2

C_CQ = 0
C_CKV = C_CQ + Q_LORA
C_KPE = C_CKV + KV_LORA
C_GB = C_KPE + HEAD_PAD
C_GC = C_GB + CONV_DIM
C_CH = C_GC + CONV_DIM
C_GCV = C_CH + CONV_DIM
C_GML = C_GCV + D_MODEL
C_END = C_GML + D_MODEL

VMEM_LIMIT = 56 * 1024 * 1024

F32 = jnp.float32
BF16 = jnp.bfloat16


def _dot(a, b):
    return jnp.dot(a, b, preferred_element_type=F32)


def _dot_nt(a, b):
    return lax.dot_general(a, b, (((1,), (1,)), ((), ())), preferred_element_type=F32)


def _layer_norm(x, g, b):
    mu = jnp.mean(x, axis=-1, keepdims=True)
    xc = x - mu
    var = jnp.mean(xc * xc, axis=-1, keepdims=True)
    return xc * lax.rsqrt(var + LN_EPS) * g + b


def _rms_norm(x, g):
    return x * lax.rsqrt(jnp.mean(x * x, axis=-1, keepdims=True) + RMS_EPS) * g


def _rope_group(x, c, a, b):
    return x * c + pltpu.roll(x, HEAD_PAD - HALF, 1) * a + pltpu.roll(x, HALF, 1) * b


def _proj_front(x_ref, lng_ref, lnb_ref, w_ref, qng_ref, wuq_ref, kvg_ref, tabs):
    qc, qa, qb, kc, ka, kb = [t[...] for t in tabs]
    h = _layer_norm(x_ref[...], lng_ref[...], lnb_ref[...])
    hb = h.astype(BF16)
    cq = _rms_norm(_dot(hb, w_ref[:, C_CQ:C_CKV]), qng_ref[...])
    q = _dot(cq.astype(BF16), wuq_ref[...])
    q_groups = [
        _rope_group(q[:, g * HEAD_PAD:(g + 1) * HEAD_PAD], qc, qa, qb) for g in range(HEADS)
    ]
    ckv = _rms_norm(_dot(hb, w_ref[:, C_CKV:C_KPE]), kvg_ref[...])
    kpe = _rope_group(_dot(hb, w_ref[:, C_KPE:C_GB]), kc, ka, kb)
    return hb, q_groups, ckv, kpe


def _gates(hb, conv_out_fn, w_ref, wco_ref, gcy_ref, gm_ref):
    gate_b = _dot(hb, w_ref[:, C_GB:C_GC])
    u = _dot(hb, w_ref[:, C_GC:C_CH]) * _dot(hb, w_ref[:, C_CH:C_GCV])
    conv_out = conv_out_fn(u)
    y_conv = _dot((gate_b * conv_out).astype(BF16), wco_ref[...])
    gcy_ref[...] = (jax.nn.sigmoid(_dot(hb, w_ref[:, C_GCV:C_GML])) * y_conv).astype(BF16)
    gm_ref[...] = jax.nn.sigmoid(_dot(hb, w_ref[:, C_GML:C_END])).astype(BF16)


def _proj_seq_kernel(x_ref, lng_ref, lnb_ref, w_ref, qng_ref, wuq_ref, kvg_ref, wuk_ref, wuvt_ref,
                     cw_ref, wco_ref, qc_ref, qa_ref, qb_ref, kc_ref, ka_ref, kb_ref, init_ref, ckv_base_ref,
                     q_ref, k_ref, vt_ref, ckv_ref, kpet_ref, gcy_ref, gm_ref, cst_ref, ubuf):
    del ckv_base_ref
    t = pl.program_id(0)
    b = pl.program_id(1)
    rows = x_ref.shape[0]

    @pl.when(t == 0)
    def _():
        ubuf[6:8, :] = init_ref[...]

    @pl.when(t > 0)
    def _():
        ubuf[6:8, :] = cst_ref[b]

    hb = _layer_norm(x_ref[...], lng_ref[...], lnb_ref[...]).astype(BF16)
    z_cq = _dot(hb, w_ref[:, C_CQ:C_CKV])
    z_ckv = _dot(hb, w_ref[:, C_CKV:C_KPE])
    z_kpe = _dot(hb, w_ref[:, C_KPE:C_GB])
    gate_b = _dot(hb, w_ref[:, C_GB:C_GC])
    gate_c = _dot(hb, w_ref[:, C_GC:C_CH])
    conv_h = _dot(hb, w_ref[:, C_CH:C_GCV])
    z_gcv = _dot(hb, w_ref[:, C_GCV:C_GML])
    z_gml = _dot(hb, w_ref[:, C_GML:C_END])
    q = _dot(_rms_norm(z_cq, qng_ref[...]).astype(BF16), wuq_ref[...])
    ckv = _rms_norm(z_ckv, kvg_ref[...])
    ckvb = ckv.astype(BF16)
    kn = _dot(ckvb, wuk_ref[...])
    v_t = _dot_nt(wuvt_ref[...], ckvb)
    u = gate_c * conv_h
    ubuf[8:8 + rows, :] = u
    cst_ref[b] = u[rows - 2:rows, :]
    conv_out = (cw_ref[0:1, :] * ubuf[6:6 + rows, :] + cw_ref[1:2, :] * ubuf[7:7 + rows, :]
                + cw_ref[2:3, :] * u)
    y_conv = _dot((gate_b * conv_out).astype(BF16), wco_ref[...])

    kpe = _rope_group(z_kpe, kc_ref[...], ka_ref[...], kb_ref[...])
    for g in range(HEADS):
        group = slice(g * HEAD_PAD, (g + 1) * HEAD_PAD)
        q_ref[:, group] = _rope_group(q[:, group], qc_ref[...], qa_ref[...], qb_ref[...]).astype(BF16)
        k_ref[:, group] = (kn[:, group] + kpe).astype(BF16)
    ckv_ref[...] = ckv.reshape(ckv_ref.shape)
    kpet_ref[...] = kpe.T[ROPE_LO:ROPE_LO + QK_ROPE, :]
    vt_ref[...] = v_t.astype(BF16)
    gcy_ref[...] = (jax.nn.sigmoid(z_gcv) * y_conv).astype(BF16)
    gm_ref[...] = jax.nn.sigmoid(z_gml).astype(BF16)


def _proj_sample_kernel(x_ref, lng_ref, lnb_ref, w_ref, qng_ref, wuq_ref, kvg_ref, wukt_ref,
                        cw_ref, wco_ref, qc_ref, qa_ref, qb_ref, kc_ref, ka_ref, kb_ref,
                        s0_ref, s1_ref,
                        qlat_ref, qpe_ref, ckv_ref, kpe_ref, gcy_ref, gm_ref, u_ref, ubuf):
    rows = x_ref.shape[0]
    n_seq = rows // 8
    hb, q_groups, ckv, kpe = _proj_front(
        x_ref, lng_ref, lnb_ref, w_ref, qng_ref, wuq_ref, kvg_ref,
        (qc_ref, qa_ref, qb_ref, kc_ref, ka_ref, kb_ref))
    for g in range(HEADS):
        qg = q_groups[g]
        qlat_ref[:, g] = _dot(qg.astype(BF16), wukt_ref[g]).reshape(n_seq, 8, KV_LORA)
        qpe_ref[:, g] = qg[:, ROPE_LO:ROPE_LO + QK_ROPE].reshape(n_seq, 8, QK_ROPE)
    ckv_ref[...] = ckv
    kpe_ref[...] = kpe

    def conv_out_fn(u):
        u_ref[...] = u
        ubuf[8:8 + rows, :] = u
        ubuf[6:8, :] = jnp.zeros((2, CONV_DIM), F32)
        tok = lax.broadcasted_iota(jnp.int32, (rows, 1), 0) % 8
        um1 = jnp.where(tok == 0, s1_ref[...], ubuf[7:7 + rows, :])
        um2 = jnp.where(tok == 0, s0_ref[...], jnp.where(tok == 1, s1_ref[...], ubuf[6:6 + rows, :]))
        return cw_ref[0:1, :] * um2 + cw_ref[1:2, :] * um1 + cw_ref[2:3, :] * u

    _gates(hb, conv_out_fn, w_ref, wco_ref, gcy_ref, gm_ref)


def _attn_prompt_kernel(q_ref, k_ref, vt_ref, km_ref, vmt_ref, o_ref, m_ref, acc_ref):
    qi = pl.program_id(1)
    tq = q_ref.shape[0]
    hsl = lambda h: slice(h * HEAD_PAD, (h + 1) * HEAD_PAD)
    vsl = lambda h: slice(h * V_HEAD, (h + 1) * V_HEAD)

    def values_and_ones(v_t):
        return jnp.concatenate([v_t, jnp.ones((16, v_t.shape[1]), BF16)], axis=0)


    diag = pl.ds(pl.multiple_of(qi * tq, tq), tq)
    key_idx = lax.broadcasted_iota(jnp.int32, (tq, tq), 0)
    query_idx = lax.broadcasted_iota(jnp.int32, (tq, tq), 1)
    causal = key_idx <= query_idx
    scores = [_dot_nt(jnp.concatenate([k_ref[diag, hsl(h)], km_ref[:, hsl(h)]], axis=0), q_ref[:, hsl(h)])
              for h in range(HEADS)]
    probs = []
    for h in range(HEADS):
        s_diag = jnp.where(causal, scores[h][0:tq], -jnp.inf)
        s_meta = scores[h][tq:tq + N_META]
        m = jnp.maximum(jnp.max(s_diag, axis=0, keepdims=True), jnp.max(s_meta, axis=0, keepdims=True))
        m_ref[h] = m
        probs.append((jnp.exp2(s_diag - m).astype(BF16), jnp.exp2(s_meta - m).astype(BF16)))
    for h in range(HEADS):
        acc_ref[h] = (_dot(values_and_ones(vt_ref[vsl(h), diag]), probs[h][0])
                      + _dot(values_and_ones(vmt_ref[vsl(h), :]), probs[h][1]))

    def visible_keys(k0, n_keys):
        keys = pl.ds(pl.multiple_of(k0, tq), n_keys)
        scores, alphas, probs = {}, {}, {}

        def softmax(h):
            m_prev = m_ref[h]
            m_new = jnp.maximum(m_prev, jnp.max(scores[h], axis=0, keepdims=True))
            m_ref[h] = m_new
            alphas[h] = jnp.exp2(m_prev - m_new)
            probs[h] = jnp.exp2(scores[h] - m_new).astype(BF16)

        def weighted_values(h):
            acc_ref[h] = alphas[h] * acc_ref[h] + _dot(values_and_ones(vt_ref[vsl(h), keys]), probs[h])

        for step in range(HEADS + SOFTMAX_LAG + VALUES_LAG):
            if step < HEADS:
                scores[step] = _dot_nt(k_ref[keys, hsl(step)], q_ref[:, hsl(step)])
            if 0 <= step - SOFTMAX_LAG < HEADS:
                softmax(step - SOFTMAX_LAG)
            if 0 <= step - SOFTMAX_LAG - VALUES_LAG < HEADS:
                weighted_values(step - SOFTMAX_LAG - VALUES_LAG)

    def pair_body(i, carry):
        visible_keys(i * (2 * tq), 2 * tq)
        return carry

    lax.fori_loop(0, qi // 2, pair_body, 0)

    @pl.when(qi % 2 == 1)
    def _():
        visible_keys((qi - 1) * tq, tq)

    o_t = jnp.concatenate(
        [acc_ref[h, 0:V_HEAD] / acc_ref[h, V_HEAD:V_HEAD + 1] for h in range(HEADS)], axis=0)
    o_ref[...] = o_t.T.astype(BF16)


def _attn_sample_kernel(pt_ref, qlat_ref, qpe_ref, ckvn_ref, kpen_ref, ckv_hbm, kpet_hbm, o_ref,
                        cbuf, kbuf, csem, ksem, *, n_pages):
    s = pl.program_id(0)
    slot = s % 2

    def page_copies(seq, slot_, p):
        pg = pt_ref[seq * n_pages + p]
        tokens = pl.ds(p * PAGE, PAGE)
        return (pltpu.make_async_copy(ckv_hbm.at[pg], cbuf.at[slot_, tokens], csem.at[slot_]),
                pltpu.make_async_copy(kpet_hbm.at[pg], kbuf.at[slot_, :, tokens], ksem.at[slot_]))

    def for_each_page(seq, slot_, fn):
        for p in range(n_pages):
            for copy in page_copies(seq, slot_, p):
                fn(copy)

    @pl.when(s == 0)
    def _():
        for_each_page(0, 0, lambda copy: copy.start())

    @pl.when(s + 1 < pl.num_programs(0))
    def _():
        for_each_page(s + 1, 1 - slot, lambda copy: copy.start())

    for_each_page(s, slot, lambda copy: copy.wait())

    n_rows = HEADS * 8
    ql = qlat_ref[0].reshape(n_rows, KV_LORA).astype(BF16)
    qp = qpe_ref[0].reshape(n_rows, QK_ROPE).astype(BF16)
    cn = ckvn_ref[0].astype(BF16)
    s_new = _dot_nt(ql, cn) + _dot_nt(qp, kpen_ref[0].astype(BF16))
    tok = lax.broadcasted_iota(jnp.int32, (n_rows, 8), 0) % 8
    key = lax.broadcasted_iota(jnp.int32, (n_rows, 8), 1)
    s_new = jnp.where(key <= tok, s_new, -jnp.inf)

    ql_pad = jnp.concatenate([ql, jnp.zeros((HEAD_PAD - n_rows, KV_LORA), BF16)], axis=0)
    chunk_keys = (n_pages // SAMPLE_CHUNKS) * PAGE

    def chunk_scores(i):
        keys = slice(i * chunk_keys, (i + 1) * chunk_keys)
        c = cbuf[slot, keys, :].astype(BF16)
        s = _dot_nt(c, ql_pad).T[0:n_rows] + _dot(qp, kbuf[slot, :, keys].astype(BF16))
        return c, s

    m = jnp.max(s_new, axis=-1, keepdims=True)
    p_new = jnp.exp(s_new - m)
    denom = jnp.sum(p_new, axis=-1, keepdims=True)
    acc = _dot(p_new.astype(BF16), cn)
    pending = chunk_scores(0)
    for i in range(SAMPLE_CHUNKS):
        c, s = pending
        if i + 1 < SAMPLE_CHUNKS:
            pending = chunk_scores(i + 1)
        m_new = jnp.maximum(m, jnp.max(s, axis=-1, keepdims=True))
        alpha = jnp.exp(m - m_new)
        p = jnp.exp(s - m_new)
        denom = alpha * denom + jnp.sum(p, axis=-1, keepdims=True)
        acc = alpha * acc + _dot(p.astype(BF16), c)
        m = m_new
    o_ref[...] = (acc / denom).reshape(HEADS, 8, KV_LORA)


def _post_kernel(*refs, sample):
    if sample:
        (olat_ref, wuvp_ref, x_ref, gcy_ref, gm_ref, lng_ref, lnb_ref, womla_ref, wo_ref,
         l1g_ref, l1b_ref, wup_ref, wdn_ref, l2g_ref, l2b_ref, y_ref) = refs
        pairs = []
        for pair in range(HEADS // 2):
            pairs.append(
                _dot(olat_ref[2 * pair].astype(BF16), wuvp_ref[2 * pair])
                + _dot(olat_ref[2 * pair + 1].astype(BF16), wuvp_ref[2 * pair + 1]))
        y_mla = sum(
            _dot(pairs[pair].astype(BF16), womla_ref[pair * HEAD_PAD:(pair + 1) * HEAD_PAD, :])
            for pair in range(HEADS // 2))
    else:
        (o_ref, x_ref, gcy_ref, gm_ref, lng_ref, lnb_ref, womla_ref, wo_ref,
         l1g_ref, l1b_ref, wup_ref, wdn_ref, l2g_ref, l2b_ref, y_ref) = refs
        y_mla = _dot(o_ref[...], womla_ref[...])
    rows = x_ref.shape[0]
    halves = [slice(0, rows // 2), slice(rows // 2, rows)]
    mixes = []
    for r in halves:
        mix_in = gcy_ref[r, :].astype(F32) + gm_ref[r, :].astype(F32) * y_mla[r, :]
        mixes.append(_dot(mix_in.astype(BF16), wo_ref[...]))
    h1s = []
    for r, mix in zip(halves, mixes):
        h = _layer_norm(x_ref[r, :], lng_ref[...], lnb_ref[...])
        h1s.append(_layer_norm(ALPHA * h + mix, l1g_ref[...], l1b_ref[...]))
    h1bs = [h1.astype(BF16) for h1 in h1s]
    ffs = [None, None]
    chunk = D_MODEL
    for c in range(D_FF // chunk):
        acts = [jnp.maximum(_dot(h1b, wup_ref[:, c * chunk:(c + 1) * chunk]), 0.0) for h1b in h1bs]
        for i, a in enumerate(acts):
            part = _dot((a * a).astype(BF16), wdn_ref[c * chunk:(c + 1) * chunk, :])
            ffs[i] = part if ffs[i] is None else ffs[i] + part
    for r, h1, ff in zip(halves, h1s, ffs):
        y_ref[r, :] = _layer_norm(ALPHA * h1 + ff, l2g_ref[...], l2b_ref[...])


def _full(shape):
    n = len(shape)
    return pl.BlockSpec(shape, lambda *_: (0,) * n)


def _rope_tables(pos, scale):
    inv_freq = ROPE_THETA ** (-2.0 * jnp.arange(HALF, dtype=F32) / QK_ROPE)
    ang = pos.astype(F32)[:, None] * inv_freq[None, :]
    cos, sin = jnp.cos(ang), jnp.sin(ang)
    n = pos.shape[0]
    ones = jnp.ones((n, ROPE_LO), F32)
    z_lo = jnp.zeros((n, ROPE_LO), F32)
    z_hi = jnp.zeros((n, HEAD_PAD - ROPE_LO - QK_ROPE), F32)
    z_half = jnp.zeros((n, HALF), F32)
    c = jnp.concatenate([ones, cos, cos, z_hi + 1.0], axis=1)
    a = jnp.concatenate([z_lo, -sin, z_half, z_hi], axis=1)
    b = jnp.concatenate([z_lo, z_half, sin, z_hi], axis=1)
    return c * scale, a * scale, b * scale


def _pack_weights(w_in, w_uq, w_uk, w_uv):
    offs = [0]
    for s in (Q_LORA, KV_LORA, QK_ROPE, CONV_DIM, CONV_DIM, CONV_DIM, D_MODEL, D_MODEL):
        offs.append(offs[-1] + s)
    pieces = [w_in[:, offs[i]:offs[i + 1]] for i in range(8)]
    kpe_pad = jnp.pad(pieces[2], ((0, 0), (ROPE_LO, HEAD_PAD - ROPE_LO - QK_ROPE)))
    w_pack = jnp.concatenate(pieces[:2] + [kpe_pad] + pieces[3:], axis=1).astype(BF16)
    head_pad = HEAD_PAD - QK_NOPE - QK_ROPE
    w_uq_p = jnp.pad(w_uq.reshape(Q_LORA, HEADS, QK_NOPE + QK_ROPE), ((0, 0), (0, 0), (0, head_pad)))
    w_uq_p = w_uq_p.reshape(Q_LORA, HEADS * HEAD_PAD).astype(BF16)
    w_uk_p = jnp.pad(w_uk, ((0, 0), (0, 0), (0, HEAD_PAD - QK_NOPE)))
    w_uk_p = w_uk_p.reshape(KV_LORA, HEADS * HEAD_PAD).astype(BF16)
    w_uk_t = jnp.pad(jnp.transpose(w_uk, (1, 2, 0)), ((0, 0), (0, HEAD_PAD - QK_NOPE), (0, 0))).astype(BF16)
    w_uv_t = w_uv.reshape(KV_LORA, HEADS * V_HEAD).T.astype(BF16)
    w_uv_even = jnp.pad(w_uv, ((0, 0), (0, 0), (0, V_HEAD)))
    w_uv_odd = jnp.pad(w_uv, ((0, 0), (0, 0), (V_HEAD, 0)))
    is_even = (jnp.arange(HEADS) % 2 == 0)[None, :, None]
    w_uv_p = jnp.transpose(jnp.where(is_even, w_uv_even, w_uv_odd), (1, 0, 2)).astype(BF16)
    return w_pack, w_uq_p, w_uk_p, w_uk_t, w_uv_t, w_uv_p


def _proj_seq(x, tabs, init, shared, tile, ckv_base):
    nb, n, _ = x.shape
    row_offset = ckv_base.shape[1] - n
    nt = n // tile
    row_spec = lambda width: pl.BlockSpec((None, tile, width), lambda t, b: (b, t, 0))
    tab_spec = pl.BlockSpec((tile, HEAD_PAD), lambda t, b: (t, 0))
    ckv_spec = pl.BlockSpec((pl.Element(1), pl.Element(tile), pl.Element(KV_LORA)),
                            lambda t, b: (b, pl.multiple_of(row_offset + t * tile, 8), 0))
    in_specs = ([row_spec(D_MODEL)] + [_full(a.shape) for a in shared] + [tab_spec] * 6 + [_full(init.shape)]
                + [pl.BlockSpec(memory_space=pl.ANY)])
    ins = (x, *shared, *tabs, init, ckv_base)
    out_shape = (
        jax.ShapeDtypeStruct((nb, n, HEADS * HEAD_PAD), BF16),
        jax.ShapeDtypeStruct((nb, n, HEADS * HEAD_PAD), BF16),
        jax.ShapeDtypeStruct((nb, HEADS * V_HEAD, n), BF16),
        jax.ShapeDtypeStruct(ckv_base.shape, F32),
        jax.ShapeDtypeStruct((nb, QK_ROPE, n), F32),
        jax.ShapeDtypeStruct((nb, n, D_MODEL), BF16),
        jax.ShapeDtypeStruct((nb, n, D_MODEL), BF16),
        jax.ShapeDtypeStruct((nb, CONV_K - 1, CONV_DIM), F32),
    )
    col_spec = lambda height: pl.BlockSpec((None, height, tile), lambda t, b: (b, 0, t))
    out_specs = (row_spec(HEADS * HEAD_PAD), row_spec(HEADS * HEAD_PAD), col_spec(HEADS * V_HEAD),
                 ckv_spec, col_spec(QK_ROPE), row_spec(D_MODEL), row_spec(D_MODEL),
                 _full((nb, CONV_K - 1, CONV_DIM)))
    return pl.pallas_call(
        _proj_seq_kernel,
        grid=(nt, nb),
        in_specs=in_specs,
        out_specs=out_specs,
        out_shape=out_shape,
        scratch_shapes=[pltpu.VMEM((tile + 8, CONV_DIM), F32)],
        input_output_aliases={len(ins) - 1: 3},
        compiler_params=pltpu.CompilerParams(
            dimension_semantics=("arbitrary", "arbitrary"), vmem_limit_bytes=VMEM_LIMIT),
        name="proj_seq",
    )(*ins)


def _proj_sample(x, tabs, s0, s1, shared, tile):
    n = x.shape[0]
    n_seq = n // 8
    row = lambda width: pl.BlockSpec((tile, width), lambda i: (i, 0))
    seq4 = lambda width: pl.BlockSpec((tile // 8, HEADS, 8, width), lambda i: (i, 0, 0, 0))
    ins = (x, *shared, *tabs, s0, s1)
    in_specs = ([row(D_MODEL)] + [_full(a.shape) for a in shared] + [row(HEAD_PAD)] * 6
                + [row(CONV_DIM)] * 2)
    out_specs = (seq4(KV_LORA), seq4(QK_ROPE), row(KV_LORA), row(HEAD_PAD), row(D_MODEL), row(D_MODEL),
                 row(CONV_DIM))
    out_shape = (
        jax.ShapeDtypeStruct((n_seq, HEADS, 8, KV_LORA), F32),
        jax.ShapeDtypeStruct((n_seq, HEADS, 8, QK_ROPE), F32),
        jax.ShapeDtypeStruct((n, KV_LORA), F32),
        jax.ShapeDtypeStruct((n, HEAD_PAD), F32),
        jax.ShapeDtypeStruct((n, D_MODEL), BF16),
        jax.ShapeDtypeStruct((n, D_MODEL), BF16),
        jax.ShapeDtypeStruct((n, CONV_DIM), F32),
    )
    return pl.pallas_call(
        _proj_sample_kernel,
        grid=(n // tile,),
        in_specs=in_specs,
        out_specs=out_specs,
        out_shape=out_shape,
        scratch_shapes=[pltpu.VMEM((tile + 8, CONV_DIM), F32)],
        compiler_params=pltpu.CompilerParams(
            dimension_semantics=("arbitrary",), vmem_limit_bytes=VMEM_LIMIT),
        name="proj_sample",
    )(*ins)


def _attn_prompt(q, k, v_t, k_meta, v_meta_t, tq):
    nb, n, _ = q.shape
    return pl.pallas_call(
        _attn_prompt_kernel,
        grid=(nb, n // tq),
        in_specs=[
            pl.BlockSpec((None, tq, HEADS * HEAD_PAD), lambda b, i: (b, i, 0)),
            pl.BlockSpec((None, n, HEADS * HEAD_PAD), lambda b, i: (b, 0, 0)),
            pl.BlockSpec((None, HEADS * V_HEAD, n), lambda b, i: (b, 0, 0)),
            _full(k_meta.shape),
            _full(v_meta_t.shape),
        ],
        out_specs=pl.BlockSpec((None, tq, HEADS * V_HEAD), lambda b, i: (b, i, 0)),
        out_shape=jax.ShapeDtypeStruct((nb, n, HEADS * V_HEAD), BF16),
        scratch_shapes=[pltpu.VMEM((HEADS, 1, tq), F32), pltpu.VMEM((HEADS, V_HEAD + 16, tq), F32)],
        compiler_params=pltpu.CompilerParams(
            dimension_semantics=("arbitrary", "arbitrary"), vmem_limit_bytes=VMEM_LIMIT),
        name="attn_prompt",
    )(q, k, v_t, k_meta, v_meta_t)


def _attn_sample(page_table, q_lat, q_pe, ckv_new, kpe_new, ckv_pool, kpe_pool_t):
    n_seq, n_pages = page_table.shape
    seq4 = lambda width: pl.BlockSpec((1, HEADS, 8, width), lambda s, pt: (s, 0, 0, 0))
    seq3 = lambda width: pl.BlockSpec((1, 8, width), lambda s, pt: (s, 0, 0))
    hbm = pl.BlockSpec(memory_space=pl.ANY)
    grid_spec = pltpu.PrefetchScalarGridSpec(
        num_scalar_prefetch=1,
        grid=(n_seq,),
        in_specs=[seq4(KV_LORA), seq4(QK_ROPE), seq3(KV_LORA), seq3(QK_ROPE), hbm, hbm],
        out_specs=pl.BlockSpec((HEADS, 8, KV_LORA), lambda s, pt: (0, s, 0)),
        scratch_shapes=[pltpu.VMEM((2, n_pages * PAGE, KV_LORA), F32),
                        pltpu.VMEM((2, QK_ROPE, n_pages * PAGE), F32),
                        pltpu.SemaphoreType.DMA((2,)), pltpu.SemaphoreType.DMA((2,))],
    )
    return pl.pallas_call(
        functools.partial(_attn_sample_kernel, n_pages=n_pages),
        grid_spec=grid_spec,
        out_shape=jax.ShapeDtypeStruct((HEADS, n_seq * 8, KV_LORA), F32),
        compiler_params=pltpu.CompilerParams(
            dimension_semantics=("arbitrary",), vmem_limit_bytes=VMEM_LIMIT),
        name="attn_sample",
    )(page_table.reshape(-1), q_lat, q_pe, ckv_new, kpe_new, ckv_pool, kpe_pool_t)


def _post(front, x, gcy, gm, weights, tile, sample):
    n = x.shape[0]
    row = lambda width: pl.BlockSpec((tile, width), lambda i: (i, 0))
    if sample:
        o_lat, w_uv_p = front
        front_specs = [pl.BlockSpec((HEADS, tile, KV_LORA), lambda i: (0, i, 0)), _full(w_uv_p.shape)]
    else:
        front_specs = [row(HEADS * V_HEAD)]
    in_specs = front_specs + [row(D_MODEL), row(D_MODEL), row(D_MODEL)] + [_full(w.shape) for w in weights]
    return pl.pallas_call(
        functools.partial(_post_kernel, sample=sample),
        grid=(n // tile,),
        in_specs=in_specs,
        out_specs=row(D_MODEL),
        out_shape=jax.ShapeDtypeStruct((n, D_MODEL), F32),
        compiler_params=pltpu.CompilerParams(
            dimension_semantics=("arbitrary",), vmem_limit_bytes=VMEM_LIMIT),
        name="post_sample" if sample else "post_prompt",
    )(*front, x, gcy, gm, *weights)


def kernel(x_prompt, x_sample, cache_ckv, cache_kpe, state_conv, page_table, meta_tokens, ln_emb_g, ln_emb_b, w_in, q_norm_g, w_uq, kv_norm_g, w_uk, w_uv, w_o_mla, conv_w, w_conv_out, w_o, ln1_g, ln1_b, w_up, w_down, ln2_g, ln2_b):
    assert w_in.shape[0] == DEPTH
    nb, seq, _ = x_prompt.shape
    n_seq, dec_seq, _ = x_sample.shape
    assert dec_seq == 8
    past_len = page_table.shape[1] * PAGE

    row2 = lambda a: a.reshape(1, -1).astype(F32)
    w_pack, w_uq_p, w_uk_p, w_uk_t, w_uv_t, w_uv_p = _pack_weights(w_in[0], w_uq[0], w_uk[0], w_uv[0])
    w_co = w_conv_out[0].astype(BF16)
    lng, lnb = row2(ln_emb_g), row2(ln_emb_b)
    head = (lng, lnb, w_pack, row2(q_norm_g[0]), w_uq_p, row2(kv_norm_g[0]))
    shared_seq = head + (w_uk_p, w_uv_t, conv_w[0], w_co)
    shared_sample = head + (w_uk_t, conv_w[0], w_co)
    post_w = (lng, lnb, w_o_mla[0].astype(BF16), w_o[0].astype(BF16), row2(ln1_g[0]), row2(ln1_b[0]),
              w_up[0].astype(BF16), w_down[0].astype(BF16), row2(ln2_g[0]), row2(ln2_b[0]))

    def tables(pos, q_scale):
        return _rope_tables(pos, q_scale) + _rope_tables(pos, 1.0)

    zero_state = jnp.zeros((CONV_K - 1, CONV_DIM), F32)
    _, k_meta, vt_meta, ckv_meta, kpet_meta, _, _, conv_meta = _proj_seq(
        meta_tokens[None], tables(jnp.arange(N_META), SM_SCALE * LOG2_E), zero_state, shared_seq, N_META,
        jnp.zeros((1, N_META, KV_LORA), F32))

    q, k, v_t, ckv_p, kpet_p, gcy_p, gm_p, conv_p = _proj_seq(
        x_prompt, tables(N_META + jnp.arange(seq), SM_SCALE * LOG2_E), conv_meta[0], shared_seq, 512,
        jnp.pad(jnp.broadcast_to(ckv_meta, (nb, N_META, KV_LORA)), ((0, 0), (0, seq), (0, 0))))
    o_p = _attn_prompt(q, k, v_t, k_meta[0], vt_meta[0], 256)
    n_p = nb * seq
    y_prompt = _post((o_p.reshape(n_p, -1),), x_prompt.reshape(n_p, D_MODEL), gcy_p.reshape(n_p, D_MODEL),
                     gm_p.reshape(n_p, D_MODEL), post_w, 512, sample=False).reshape(nb, seq, D_MODEL)

    n_s = n_seq * dec_seq
    tabs_s = tuple(jnp.tile(t, (n_seq, 1)) for t in tables(past_len + jnp.arange(dec_seq), SM_SCALE))
    s0 = jnp.repeat(state_conv[0, :, 0], dec_seq, axis=0)
    s1 = jnp.repeat(state_conv[0, :, 1], dec_seq, axis=0)
    xs = x_sample.reshape(n_s, D_MODEL)
    q_lat, q_pe, ckv_s, kpe_s128, gcy_s, gm_s, u_s = _proj_sample(xs, tabs_s, s0, s1, shared_sample, 256)
    kpe_s = kpe_s128[:, ROPE_LO:ROPE_LO + QK_ROPE]
    o_lat = _attn_sample(page_table, q_lat, q_pe, ckv_s.reshape(n_seq, dec_seq, KV_LORA),
                         kpe_s.reshape(n_seq, dec_seq, QK_ROPE), cache_ckv[0],
                         jnp.swapaxes(cache_kpe[0], 1, 2))
    y_sample = _post((o_lat, w_uv_p), xs, gcy_s, gm_s, post_w, 512, sample=True).reshape(n_seq, dec_seq, D_MODEL)

    new_kpe_prompt = jnp.swapaxes(
        jnp.concatenate([jnp.broadcast_to(kpet_meta, (nb, QK_ROPE, N_META)), kpet_p], axis=2), 1, 2)[None]
    new_conv_sample = u_s.reshape(n_seq, dec_seq, CONV_DIM)[:, dec_seq - (CONV_K - 1):][None]
    return (y_prompt, y_sample, ckv_p[None], new_kpe_prompt, conv_p[None],
            ckv_s.reshape(n_seq, dec_seq, KV_LORA)[None], kpe_s.reshape(n_seq, dec_seq, QK_ROPE)[None],
            new_conv_sample)
```

```python
import functools

import jax
import jax.numpy as jnp
from jax import lax
from jax.experimental import pallas as pl
from jax.experimental.pallas import tpu as pltpu

D_MODEL = 1024
N_META = 16
HEADS = 8
Q_LORA = 384
KV_LORA = 256
QK_NOPE = 64
QK_ROPE = 32
V_HEAD = 64
CONV_DIM = 512
CONV_K = 3
D_FF = 4096
PAGE = 128
ROPE_THETA = 10000.0
LN_EPS = 1e-5
RMS_EPS = 1e-6
DEPTH = 1
ALPHA = (2.0 * DEPTH) ** 0.25
SM_SCALE = (QK_NOPE + QK_ROPE) ** -0.5
LOG2_E = 1.4426950408889634

SAMPLE_CHUNKS = 4
HEAD_PAD = 128
ROPE_LO = QK_NOPE
HALF = QK_ROPE // 2

C_CQ = 0
C_KPE = C_CQ + Q_LORA
C_CKV = C_KPE + HEAD_PAD
C_GB = C_CKV + KV_LORA
C_GC = C_GB + CONV_DIM
C_CH = C_GC + CONV_DIM
C_GCV = C_CH + CONV_DIM
C_GML = C_GCV + D_MODEL
C_END = C_GML + D_MODEL

VMEM_LIMIT = 56 * 1024 * 1024

DENSE_TILE = 512
SAMPLE_PROJ_TILE = 512
QUERY_TILE = 256
QUERY_TILES_PER_STEP = 4

F32 = jnp.float32
BF16 = jnp.bfloat16


def _dot(a, b):
    return jnp.dot(a, b, preferred_element_type=F32)


def _dot_nt(a, b):
    return lax.dot_general(a, b, (((1,), (1,)), ((), ())), preferred_element_type=F32)


def _layer_norm(x, g, b):
    mu = jnp.mean(x, axis=-1, keepdims=True)
    xc = x - mu
    var = jnp.mean(xc * xc, axis=-1, keepdims=True)
    return xc * lax.rsqrt(var + LN_EPS) * g + b


def _rms_norm(x, g):
    return x * lax.rsqrt(jnp.mean(x * x, axis=-1, keepdims=True) + RMS_EPS) * g


def _rope_group(x, c, a, b):
    return x * c + pltpu.roll(x, HEAD_PAD - HALF, 1) * a + pltpu.roll(x, HALF, 1) * b


def _proj_seq_kernel(x_ref, lng_ref, lnb_ref, w_ref, qng_ref, wuq_ref, kvg_ref, wuk_ref, wuvt_ref,
                     cw_ref, wco_ref, qcos_ref, qsin_ref, kc_ref, ka_ref, kb_ref, init_ref, ckv_base_ref,
                     qt_ref, k_ref, vt_ref, ckv_ref, kpet_ref, gcy_ref, gm_ref, cst_ref, ubuf, *, q_scale):
    del ckv_base_ref
    t = pl.program_id(0)
    b = pl.program_id(1)
    rows = x_ref.shape[0]

    @pl.when(t == 0)
    def _():
        ubuf[6:8, :] = init_ref[...]

    @pl.when(t > 0)
    def _():
        ubuf[6:8, :] = cst_ref[b]

    hb = _layer_norm(x_ref[...], lng_ref[...], lnb_ref[...]).astype(BF16)
    z_cq_kpe = _dot(hb, w_ref[:, C_CQ:C_CKV])
    z_cq = z_cq_kpe[:, C_CQ:C_KPE]
    z_kpe = z_cq_kpe[:, C_KPE:C_CKV]
    z_ckv = _dot(hb, w_ref[:, C_CKV:C_GB])
    gate_b = _dot(hb, w_ref[:, C_GB:C_GC])
    gate_c = _dot(hb, w_ref[:, C_GC:C_CH])
    conv_h = _dot(hb, w_ref[:, C_CH:C_GCV])
    z_gcv = _dot(hb, w_ref[:, C_GCV:C_GML])
    z_gml = _dot(hb, w_ref[:, C_GML:C_END])
    q_t = _dot_nt(wuq_ref[...], _rms_norm(z_cq, qng_ref[...]).astype(BF16))
    ckv = _rms_norm(z_ckv, kvg_ref[...])
    ckvb = ckv.astype(BF16)
    kn = _dot(ckvb, wuk_ref[...])
    v_t = _dot_nt(wuvt_ref[...], ckvb)
    u = gate_c * conv_h
    ubuf[8:8 + rows, :] = u
    cst_ref[b] = u[rows - 2:rows, :]
    conv_out = (cw_ref[0:1, :] * ubuf[6:6 + rows, :] + cw_ref[1:2, :] * ubuf[7:7 + rows, :]
                + cw_ref[2:3, :] * u)
    y_conv = _dot((gate_b * conv_out).astype(BF16), wco_ref[...])

    kpe = _rope_group(z_kpe, kc_ref[...], ka_ref[...], kb_ref[...])
    q_cos, q_sin = qcos_ref[...], qsin_ref[...]
    for g in range(HEADS):
        group = slice(g * HEAD_PAD, (g + 1) * HEAD_PAD)
        k_ref[:, group] = (kn[:, group] + kpe).astype(BF16)
        lo = g * HEAD_PAD + ROPE_LO
        x1, x2 = q_t[lo:lo + HALF], q_t[lo + HALF:lo + 2 * HALF]
        qt_ref[g * HEAD_PAD:lo, :] = (q_t[g * HEAD_PAD:lo] * q_scale).astype(BF16)
        qt_ref[lo:lo + HALF, :] = (x1 * q_cos - x2 * q_sin).astype(BF16)
        qt_ref[lo + HALF:lo + 2 * HALF, :] = (x2 * q_cos + x1 * q_sin).astype(BF16)
        qt_ref[lo + 2 * HALF:(g + 1) * HEAD_PAD, :] = jnp.zeros((HEAD_PAD - ROPE_LO - QK_ROPE, rows), BF16)
    ckv_ref[...] = ckv.reshape(ckv_ref.shape)
    kpet_ref[...] = kpe.T[ROPE_LO:ROPE_LO + QK_ROPE, :]
    vt_ref[...] = v_t.astype(BF16)
    gcy_ref[...] = (jax.nn.sigmoid(z_gcv) * y_conv).astype(BF16)
    gm_ref[...] = jax.nn.sigmoid(z_gml).astype(BF16)


def _proj_sample_kernel(x_ref, lng_ref, lnb_ref, w_ref, qng_ref, wuq_ref, kvg_ref, wukt_ref,
                        cw_ref, wco_ref, qc_ref, qa_ref, qb_ref, kc_ref, ka_ref, kb_ref,
                        s0_ref, s1_ref,
                        qlat_ref, qpe_ref, ckv_ref, kpe_ref, gcy_ref, gm_ref, u_ref, ubuf):
    rows = x_ref.shape[0]
    n_seq = rows // 8
    hb = _layer_norm(x_ref[...], lng_ref[...], lnb_ref[...]).astype(BF16)
    z_cq_kpe = _dot(hb, w_ref[:, C_CQ:C_CKV])
    z_ckv = _dot(hb, w_ref[:, C_CKV:C_GB])
    gate_b = _dot(hb, w_ref[:, C_GB:C_GC])
    gate_c = _dot(hb, w_ref[:, C_GC:C_CH])
    conv_h = _dot(hb, w_ref[:, C_CH:C_GCV])
    z_gcv = _dot(hb, w_ref[:, C_GCV:C_GML])
    z_gml = _dot(hb, w_ref[:, C_GML:C_END])
    q = _dot(_rms_norm(z_cq_kpe[:, C_CQ:C_KPE], qng_ref[...]).astype(BF16), wuq_ref[...])
    u = gate_c * conv_h
    u_ref[...] = u
    ubuf[8:8 + rows, :] = u
    ubuf[6:8, :] = jnp.zeros((2, CONV_DIM), F32)
    tok = lax.broadcasted_iota(jnp.int32, (rows, 1), 0) % 8
    um1 = jnp.where(tok == 0, s1_ref[...], ubuf[7:7 + rows, :])
    um2 = jnp.where(tok == 0, s0_ref[...], jnp.where(tok == 1, s1_ref[...], ubuf[6:6 + rows, :]))
    conv_out = cw_ref[0:1, :] * um2 + cw_ref[1:2, :] * um1 + cw_ref[2:3, :] * u
    y_conv = _dot((gate_b * conv_out).astype(BF16), wco_ref[...])

    for g in range(HEADS):
        qg = _rope_group(q[:, g * HEAD_PAD:(g + 1) * HEAD_PAD], qc_ref[...], qa_ref[...], qb_ref[...])
        qlat_ref[:, g] = _dot(qg.astype(BF16), wukt_ref[g]).reshape(n_seq, 8, KV_LORA)
        qpe_ref[:, g] = qg[:, ROPE_LO:ROPE_LO + QK_ROPE].reshape(n_seq, 8, QK_ROPE)
    ckv_ref[...] = _rms_norm(z_ckv, kvg_ref[...])
    kpe_ref[...] = _rope_group(z_cq_kpe[:, C_KPE:C_CKV], kc_ref[...], ka_ref[...], kb_ref[...])
    gcy_ref[...] = (jax.nn.sigmoid(z_gcv) * y_conv).astype(BF16)
    gm_ref[...] = jax.nn.sigmoid(z_gml).astype(BF16)


def _attn_prompt_kernel(qt_ref, k_ref, vt_ref, km_ref, vmt_ref, o_ref, m_ref, acc_ref, *, tq):
    tiles = qt_ref.shape[1] // tq
    for sub in range(tiles):
        rows = slice(sub * tq, (sub + 1) * tq)
        _attn_prompt_tile(pl.program_id(1) * tiles + sub, qt_ref.at[:, rows], k_ref, vt_ref, km_ref, vmt_ref,
                          o_ref.at[rows], m_ref, acc_ref)


def _attn_prompt_tile(qi, qt_ref, k_ref, vt_ref, km_ref, vmt_ref, o_ref, m_ref, acc_ref):
    tq = qt_ref.shape[1]
    hsl = lambda h: slice(h * HEAD_PAD, (h + 1) * HEAD_PAD)
    vsl = lambda h: slice(h * V_HEAD, (h + 1) * V_HEAD)

    def values_and_ones(v_t):
        return jnp.concatenate([v_t, jnp.ones((16, v_t.shape[1]), BF16)], axis=0)


    diag = pl.ds(pl.multiple_of(qi * tq, tq), tq)
    key_idx = lax.broadcasted_iota(jnp.int32, (tq, tq), 0)
    query_idx = lax.broadcasted_iota(jnp.int32, (tq, tq), 1)
    causal = key_idx <= query_idx
    scores = [_dot(jnp.concatenate([k_ref[diag, hsl(h)], km_ref[:, hsl(h)]], axis=0), qt_ref[hsl(h), :])
              for h in range(HEADS)]
    probs = []
    for h in range(HEADS):
        s_diag = jnp.where(causal, scores[h][0:tq], -jnp.inf)
        s_meta = scores[h][tq:tq + N_META]
        m = jnp.maximum(jnp.max(s_diag, axis=0, keepdims=True), jnp.max(s_meta, axis=0, keepdims=True))
        m_ref[h] = m
        probs.append((jnp.exp2(s_diag - m).astype(BF16), jnp.exp2(s_meta - m).astype(BF16)))
    for h in range(HEADS):
        acc_ref[h] = (_dot(values_and_ones(vt_ref[vsl(h), diag]), probs[h][0])
                      + _dot(values_and_ones(vmt_ref[vsl(h), :]), probs[h][1]))

    def visible_keys(k0, n_keys):
        keys = pl.ds(pl.multiple_of(k0, tq), n_keys)
        scores = [_dot(k_ref[keys, hsl(h)], qt_ref[hsl(h), :]) for h in range(HEADS)]
        alphas, probs = [], []
        for h in range(HEADS):
            m_prev = m_ref[h]
            m_new = jnp.maximum(m_prev, jnp.max(scores[h], axis=0, keepdims=True))
            m_ref[h] = m_new
            alphas.append(jnp.exp2(m_prev - m_new))
            probs.append(jnp.exp2(scores[h] - m_new).astype(BF16))
        for h in range(HEADS):
            acc_ref[h] = alphas[h] * acc_ref[h] + _dot(values_and_ones(vt_ref[vsl(h), keys]), probs[h])

    def pair_body(i, carry):
        visible_keys(i * (2 * tq), 2 * tq)
        return carry

    lax.fori_loop(0, qi // 2, pair_body, 0)

    @pl.when(qi % 2 == 1)
    def _():
        visible_keys((qi - 1) * tq, tq)

    o_t = jnp.concatenate(
        [acc_ref[h, 0:V_HEAD] / acc_ref[h, V_HEAD:V_HEAD + 1] for h in range(HEADS)], axis=0)
    o_ref[...] = o_t.T.astype(BF16)


def _attn_sample_kernel(pt_ref, qlat_ref, qpe_ref, ckvn_ref, kpen_ref, ckv_hbm, kpet_hbm, o_ref,
                        cbuf, kbuf, csem, ksem, *, n_pages):
    s = pl.program_id(0)
    slot = s % 2

    def page_copies(seq, slot_, p):
        pg = pt_ref[seq * n_pages + p]
        tokens = pl.ds(p * PAGE, PAGE)
        return (pltpu.make_async_copy(ckv_hbm.at[pg], cbuf.at[slot_, tokens], csem.at[slot_]),
                pltpu.make_async_copy(kpet_hbm.at[pg], kbuf.at[slot_, :, tokens], ksem.at[slot_]))

    def for_each_page(seq, slot_, fn):
        for p in range(n_pages):
            for copy in page_copies(seq, slot_, p):
                fn(copy)

    @pl.when(s == 0)
    def _():
        for_each_page(0, 0, lambda copy: copy.start())

    @pl.when(s + 1 < pl.num_programs(0))
    def _():
        for_each_page(s + 1, 1 - slot, lambda copy: copy.start())

    for_each_page(s, slot, lambda copy: copy.wait())

    n_rows = HEADS * 8
    ql = qlat_ref[0].reshape(n_rows, KV_LORA).astype(BF16)
    qp = qpe_ref[0].reshape(n_rows, QK_ROPE).astype(BF16)
    cn = ckvn_ref[0].astype(BF16)
    s_new = _dot_nt(ql, cn) + _dot_nt(qp, kpen_ref[0].astype(BF16))
    tok = lax.broadcasted_iota(jnp.int32, (n_rows, 8), 0) % 8
    key = lax.broadcasted_iota(jnp.int32, (n_rows, 8), 1)
    s_new = jnp.where(key <= tok, s_new, -jnp.inf)

    ql_pad = jnp.concatenate([ql, jnp.zeros((HEAD_PAD - n_rows, KV_LORA), BF16)], axis=0)
    chunk_keys = (n_pages // SAMPLE_CHUNKS) * PAGE

    def chunk_scores(i):
        keys = slice(i * chunk_keys, (i + 1) * chunk_keys)
        c = cbuf[slot, keys, :].astype(BF16)
        s = _dot_nt(c, ql_pad).T[0:n_rows] + _dot(qp, kbuf[slot, :, keys].astype(BF16))
        return c, s

    m = jnp.max(s_new, axis=-1, keepdims=True)
    p_new = jnp.exp(s_new - m)
    denom = jnp.sum(p_new, axis=-1, keepdims=True)
    acc = _dot(p_new.astype(BF16), cn)
    pending = chunk_scores(0)
    for i in range(SAMPLE_CHUNKS):
        c, s = pending
        if i + 1 < SAMPLE_CHUNKS:
            pending = chunk_scores(i + 1)
        m_new = jnp.maximum(m, jnp.max(s, axis=-1, keepdims=True))
        alpha = jnp.exp(m - m_new)
        p = jnp.exp(s - m_new)
        denom = alpha * denom + jnp.sum(p, axis=-1, keepdims=True)
        acc = alpha * acc + _dot(p.astype(BF16), c)
        m = m_new
    o_ref[...] = (acc / denom).reshape(HEADS, 8, KV_LORA)


def _post_kernel(*refs, sample):
    if sample:
        (olat_ref, wuvp_ref, x_ref, gcy_ref, gm_ref, lng_ref, lnb_ref, womla_ref, wo_ref,
         l1g_ref, l1b_ref, wup_ref, wdn_ref, l2g_ref, l2b_ref, y_ref) = refs
        pairs = []
        for pair in range(HEADS // 2):
            pairs.append(
                _dot(olat_ref[2 * pair].astype(BF16), wuvp_ref[2 * pair])
                + _dot(olat_ref[2 * pair + 1].astype(BF16), wuvp_ref[2 * pair + 1]))
        y_mla = sum(
            _dot(pairs[pair].astype(BF16), womla_ref[pair * HEAD_PAD:(pair + 1) * HEAD_PAD, :])
            for pair in range(HEADS // 2))
    else:
        (o_ref, x_ref, gcy_ref, gm_ref, lng_ref, lnb_ref, womla_ref, wo_ref,
         l1g_ref, l1b_ref, wup_ref, wdn_ref, l2g_ref, l2b_ref, y_ref) = refs
        y_mla = _dot(o_ref[...], womla_ref[...])
    rows = x_ref.shape[0]
    halves = [slice(0, rows // 2), slice(rows // 2, rows)]
    mixes = []
    for r in halves:
        mix_in = gcy_ref[r, :].astype(F32) + gm_ref[r, :].astype(F32) * y_mla[r, :]
        mixes.append(_dot(mix_in.astype(BF16), wo_ref[...]))
    h1s = []
    for r, mix in zip(halves, mixes):
        h = _layer_norm(x_ref[r, :], lng_ref[...], lnb_ref[...])
        h1s.append(_layer_norm(ALPHA * h + mix, l1g_ref[...], l1b_ref[...]))
    h1bs = [h1.astype(BF16) for h1 in h1s]
    ffs = [None, None]
    chunk = D_MODEL
    for c in range(D_FF // chunk):
        acts = [jnp.maximum(_dot(h1b, wup_ref[:, c * chunk:(c + 1) * chunk]), 0.0) for h1b in h1bs]
        for i, a in enumerate(acts):
            part = _dot((a * a).astype(BF16), wdn_ref[c * chunk:(c + 1) * chunk, :])
            ffs[i] = part if ffs[i] is None else ffs[i] + part
    for r, h1, ff in zip(halves, h1s, ffs):
        y_ref[r, :] = _layer_norm(ALPHA * h1 + ff, l2g_ref[...], l2b_ref[...])


def _full(shape):
    n = len(shape)
    return pl.BlockSpec(shape, lambda *_: (0,) * n)


def _rope_cos_sin(pos, transposed=False):
    inv_freq = ROPE_THETA ** (-2.0 * jnp.arange(HALF, dtype=F32) / QK_ROPE)
    ang = pos.astype(F32)[:, None] * inv_freq[None, :]
    cos, sin = jnp.cos(ang), jnp.sin(ang)
    return (cos.T, sin.T) if transposed else (cos, sin)


def _rope_tables(pos, scale):
    cos, sin = _rope_cos_sin(pos)
    n = pos.shape[0]
    ones = jnp.ones((n, ROPE_LO), F32)
    z_lo = jnp.zeros((n, ROPE_LO), F32)
    z_hi = jnp.zeros((n, HEAD_PAD - ROPE_LO - QK_ROPE), F32)
    z_half = jnp.zeros((n, HALF), F32)
    c = jnp.concatenate([ones, cos, cos, z_hi + 1.0], axis=1)
    a = jnp.concatenate([z_lo, -sin, z_half, z_hi], axis=1)
    b = jnp.concatenate([z_lo, z_half, sin, z_hi], axis=1)
    return c * scale, a * scale, b * scale


def _pack_weights(w_in, w_uq, w_uk, w_uv):
    offs = [0]
    for s in (Q_LORA, KV_LORA, QK_ROPE, CONV_DIM, CONV_DIM, CONV_DIM, D_MODEL, D_MODEL):
        offs.append(offs[-1] + s)
    pieces = [w_in[:, offs[i]:offs[i + 1]] for i in range(8)]
    kpe_pad = jnp.pad(pieces[2], ((0, 0), (ROPE_LO, HEAD_PAD - ROPE_LO - QK_ROPE)))
    w_pack = jnp.concatenate([pieces[0], kpe_pad, pieces[1]] + pieces[3:], axis=1).astype(BF16)
    head_pad = HEAD_PAD - QK_NOPE - QK_ROPE
    w_uq_p = jnp.pad(w_uq.reshape(Q_LORA, HEADS, QK_NOPE + QK_ROPE), ((0, 0), (0, 0), (0, head_pad)))
    w_uq_p = w_uq_p.reshape(Q_LORA, HEADS * HEAD_PAD).astype(BF16)
    w_uk_p = jnp.pad(w_uk, ((0, 0), (0, 0), (0, HEAD_PAD - QK_NOPE)))
    w_uk_p = w_uk_p.reshape(KV_LORA, HEADS * HEAD_PAD).astype(BF16)
    w_uk_t = jnp.pad(jnp.transpose(w_uk, (1, 2, 0)), ((0, 0), (0, HEAD_PAD - QK_NOPE), (0, 0))).astype(BF16)
    w_uv_t = w_uv.reshape(KV_LORA, HEADS * V_HEAD).T.astype(BF16)
    w_uv_even = jnp.pad(w_uv, ((0, 0), (0, 0), (0, V_HEAD)))
    w_uv_odd = jnp.pad(w_uv, ((0, 0), (0, 0), (V_HEAD, 0)))
    is_even = (jnp.arange(HEADS) % 2 == 0)[None, :, None]
    w_uv_p = jnp.transpose(jnp.where(is_even, w_uv_even, w_uv_odd), (1, 0, 2)).astype(BF16)
    return w_pack, w_uq_p, w_uk_p, w_uk_t, w_uv_t, w_uv_p


def _proj_seq(x, pos, init, shared, tile, ckv_base):
    nb, n, _ = x.shape
    row_offset = ckv_base.shape[1] - n
    nt = n // tile
    q_scale = SM_SCALE * LOG2_E
    q_tabs = tuple(t * q_scale for t in _rope_cos_sin(pos, transposed=True))
    k_tabs = _rope_tables(pos, 1.0)
    row_spec = lambda width: pl.BlockSpec((None, tile, width), lambda t, b: (b, t, 0))
    tab_spec = pl.BlockSpec((tile, HEAD_PAD), lambda t, b: (t, 0))
    tab_t_spec = pl.BlockSpec((HALF, tile), lambda t, b: (0, t))
    ckv_spec = pl.BlockSpec((pl.Element(1), pl.Element(tile), pl.Element(KV_LORA)),
                            lambda t, b: (b, pl.multiple_of(row_offset + t * tile, 8), 0))
    in_specs = ([row_spec(D_MODEL)] + [_full(a.shape) for a in shared] + [tab_t_spec] * 2 + [tab_spec] * 3
                + [_full(init.shape), pl.BlockSpec(memory_space=pl.ANY)])
    ins = (x, *shared, *q_tabs, *k_tabs, init, ckv_base)
    out_shape = (
        jax.ShapeDtypeStruct((nb, HEADS * HEAD_PAD, n), BF16),
        jax.ShapeDtypeStruct((nb, n, HEADS * HEAD_PAD), BF16),
        jax.ShapeDtypeStruct((nb, HEADS * V_HEAD, n), BF16),
        jax.ShapeDtypeStruct(ckv_base.shape, F32),
        jax.ShapeDtypeStruct((nb, QK_ROPE, n), F32),
        jax.ShapeDtypeStruct((nb, n, D_MODEL), BF16),
        jax.ShapeDtypeStruct((nb, n, D_MODEL), BF16),
        jax.ShapeDtypeStruct((nb, CONV_K - 1, CONV_DIM), F32),
    )
    col_spec = lambda height: pl.BlockSpec((None, height, tile), lambda t, b: (b, 0, t))
    out_specs = (col_spec(HEADS * HEAD_PAD), row_spec(HEADS * HEAD_PAD), col_spec(HEADS * V_HEAD),
                 ckv_spec, col_spec(QK_ROPE), row_spec(D_MODEL), row_spec(D_MODEL),
                 _full((nb, CONV_K - 1, CONV_DIM)))
    return pl.pallas_call(
        functools.partial(_proj_seq_kernel, q_scale=q_scale),
        grid=(nt, nb),
        in_specs=in_specs,
        out_specs=out_specs,
        out_shape=out_shape,
        scratch_shapes=[pltpu.VMEM((tile + 8, CONV_DIM), F32)],
        input_output_aliases={len(ins) - 1: 3},
        compiler_params=pltpu.CompilerParams(
            dimension_semantics=("arbitrary", "arbitrary"), vmem_limit_bytes=VMEM_LIMIT),
        name="proj_seq",
    )(*ins)


def _proj_sample(x, tabs, s0, s1, shared, tile):
    n = x.shape[0]
    n_seq = n // 8
    row = lambda width: pl.BlockSpec((tile, width), lambda i: (i, 0))
    seq4 = lambda width: pl.BlockSpec((tile // 8, HEADS, 8, width), lambda i: (i, 0, 0, 0))
    ins = (x, *shared, *tabs, s0, s1)
    in_specs = ([row(D_MODEL)] + [_full(a.shape) for a in shared] + [row(HEAD_PAD)] * 6
                + [row(CONV_DIM)] * 2)
    out_specs = (seq4(KV_LORA), seq4(QK_ROPE), row(KV_LORA), row(HEAD_PAD), row(D_MODEL), row(D_MODEL),
                 row(CONV_DIM))
    out_shape = (
        jax.ShapeDtypeStruct((n_seq, HEADS, 8, KV_LORA), F32),
        jax.ShapeDtypeStruct((n_seq, HEADS, 8, QK_ROPE), F32),
        jax.ShapeDtypeStruct((n, KV_LORA), F32),
        jax.ShapeDtypeStruct((n, HEAD_PAD), F32),
        jax.ShapeDtypeStruct((n, D_MODEL), BF16),
        jax.ShapeDtypeStruct((n, D_MODEL), BF16),
        jax.ShapeDtypeStruct((n, CONV_DIM), F32),
    )
    return pl.pallas_call(
        _proj_sample_kernel,
        grid=(n // tile,),
        in_specs=in_specs,
        out_specs=out_specs,
        out_shape=out_shape,
        scratch_shapes=[pltpu.VMEM((tile + 8, CONV_DIM), F32)],
        compiler_params=pltpu.CompilerParams(
            dimension_semantics=("arbitrary",), vmem_limit_bytes=VMEM_LIMIT),
        name="proj_sample",
    )(*ins)


def _attn_prompt(q_t, k, v_t, k_meta, v_meta_t, tq, tiles_per_step):
    nb, n, _ = k.shape
    step_rows = tq * tiles_per_step
    return pl.pallas_call(
        functools.partial(_attn_prompt_kernel, tq=tq),
        grid=(nb, n // step_rows),
        in_specs=[
            pl.BlockSpec((None, HEADS * HEAD_PAD, step_rows), lambda b, i: (b, 0, i)),
            pl.BlockSpec((None, n, HEADS * HEAD_PAD), lambda b, i: (b, 0, 0)),
            pl.BlockSpec((None, HEADS * V_HEAD, n), lambda b, i: (b, 0, 0)),
            _full(k_meta.shape),
            _full(v_meta_t.shape),
        ],
        out_specs=pl.BlockSpec((None, step_rows, HEADS * V_HEAD), lambda b, i: (b, i, 0)),
        out_shape=jax.ShapeDtypeStruct((nb, n, HEADS * V_HEAD), BF16),
        scratch_shapes=[pltpu.VMEM((HEADS, 1, tq), F32), pltpu.VMEM((HEADS, V_HEAD + 16, tq), F32)],
        compiler_params=pltpu.CompilerParams(
            dimension_semantics=("arbitrary", "arbitrary"), vmem_limit_bytes=VMEM_LIMIT),
        name="attn_prompt",
    )(q_t, k, v_t, k_meta, v_meta_t)


def _attn_sample(page_table, q_lat, q_pe, ckv_new, kpe_new, ckv_pool, kpe_pool_t):
    n_seq, n_pages = page_table.shape
    seq4 = lambda width: pl.BlockSpec((1, HEADS, 8, width), lambda s, pt: (s, 0, 0, 0))
    seq3 = lambda width: pl.BlockSpec((1, 8, width), lambda s, pt: (s, 0, 0))
    hbm = pl.BlockSpec(memory_space=pl.ANY)
    grid_spec = pltpu.PrefetchScalarGridSpec(
        num_scalar_prefetch=1,
        grid=(n_seq,),
        in_specs=[seq4(KV_LORA), seq4(QK_ROPE), seq3(KV_LORA), seq3(QK_ROPE), hbm, hbm],
        out_specs=pl.BlockSpec((HEADS, 8, KV_LORA), lambda s, pt: (0, s, 0)),
        scratch_shapes=[pltpu.VMEM((2, n_pages * PAGE, KV_LORA), F32),
                        pltpu.VMEM((2, QK_ROPE, n_pages * PAGE), F32),
                        pltpu.SemaphoreType.DMA((2,)), pltpu.SemaphoreType.DMA((2,))],
    )
    return pl.pallas_call(
        functools.partial(_attn_sample_kernel, n_pages=n_pages),
        grid_spec=grid_spec,
        out_shape=jax.ShapeDtypeStruct((HEADS, n_seq * 8, KV_LORA), F32),
        compiler_params=pltpu.CompilerParams(
            dimension_semantics=("arbitrary",), vmem_limit_bytes=VMEM_LIMIT),
        name="attn_sample",
    )(page_table.reshape(-1), q_lat, q_pe, ckv_new, kpe_new, ckv_pool, kpe_pool_t)


def _post(front, x, gcy, gm, weights, tile, sample):
    n = x.shape[0]
    row = lambda width: pl.BlockSpec((tile, width), lambda i: (i, 0))
    if sample:
        o_lat, w_uv_p = front
        front_specs = [pl.BlockSpec((HEADS, tile, KV_LORA), lambda i: (0, i, 0)), _full(w_uv_p.shape)]
    else:
        front_specs = [row(HEADS * V_HEAD)]
    in_specs = front_specs + [row(D_MODEL), row(D_MODEL), row(D_MODEL)] + [_full(w.shape) for w in weights]
    return pl.pallas_call(
        functools.partial(_post_kernel, sample=sample),
        grid=(n // tile,),
        in_specs=in_specs,
        out_specs=row(D_MODEL),
        out_shape=jax.ShapeDtypeStruct((n, D_MODEL), F32),
        compiler_params=pltpu.CompilerParams(
            dimension_semantics=("arbitrary",), vmem_limit_bytes=VMEM_LIMIT),
        name="post_sample" if sample else "post_prompt",
    )(*front, x, gcy, gm, *weights)


def kernel(x_prompt, x_sample, cache_ckv, cache_kpe, state_conv, page_table, meta_tokens, ln_emb_g, ln_emb_b, w_in, q_norm_g, w_uq, kv_norm_g, w_uk, w_uv, w_o_mla, conv_w, w_conv_out, w_o, ln1_g, ln1_b, w_up, w_down, ln2_g, ln2_b):
    assert w_in.shape[0] == DEPTH
    nb, seq, _ = x_prompt.shape
    n_seq, dec_seq, _ = x_sample.shape
    assert dec_seq == 8
    past_len = page_table.shape[1] * PAGE

    row2 = lambda a: a.reshape(1, -1).astype(F32)
    w_pack, w_uq_p, w_uk_p, w_uk_t, w_uv_t, w_uv_p = _pack_weights(w_in[0], w_uq[0], w_uk[0], w_uv[0])
    w_co = w_conv_out[0].astype(BF16)
    lng, lnb = row2(ln_emb_g), row2(ln_emb_b)
    qng, kvg = row2(q_norm_g[0]), row2(kv_norm_g[0])
    shared_seq = (lng, lnb, w_pack, qng, w_uq_p.T, kvg, w_uk_p, w_uv_t, conv_w[0], w_co)
    shared_sample = (lng, lnb, w_pack, qng, w_uq_p, kvg, w_uk_t, conv_w[0], w_co)
    post_w = (lng, lnb, w_o_mla[0].astype(BF16), w_o[0].astype(BF16), row2(ln1_g[0]), row2(ln1_b[0]),
              w_up[0].astype(BF16), w_down[0].astype(BF16), row2(ln2_g[0]), row2(ln2_b[0]))

    zero_state = jnp.zeros((CONV_K - 1, CONV_DIM), F32)
    _, k_meta, vt_meta, ckv_meta, kpet_meta, _, _, conv_meta = _proj_seq(
        meta_tokens[None], jnp.arange(N_META), zero_state, shared_seq, N_META,
        jnp.zeros((1, N_META, KV_LORA), F32))

    q_t, k, v_t, ckv_p, kpet_p, gcy_p, gm_p, conv_p = _proj_seq(
        x_prompt, N_META + jnp.arange(seq), conv_meta[0], shared_seq, DENSE_TILE,
        jnp.pad(jnp.broadcast_to(ckv_meta, (nb, N_META, KV_LORA)), ((0, 0), (0, seq), (0, 0))))
    o_p = _attn_prompt(q_t, k, v_t, k_meta[0], vt_meta[0], QUERY_TILE, QUERY_TILES_PER_STEP)
    n_p = nb * seq
    y_prompt = _post((o_p.reshape(n_p, -1),), x_prompt.reshape(n_p, D_MODEL), gcy_p.reshape(n_p, D_MODEL),
                     gm_p.reshape(n_p, D_MODEL), post_w, DENSE_TILE, sample=False).reshape(nb, seq, D_MODEL)

    n_s = n_seq * dec_seq
    pos_s = past_len + jnp.arange(dec_seq)
    tabs_s = tuple(jnp.tile(t, (n_seq, 1)) for t in _rope_tables(pos_s, SM_SCALE) + _rope_tables(pos_s, 1.0))
    s0 = jnp.repeat(state_conv[0, :, 0], dec_seq, axis=0)
    s1 = jnp.repeat(state_conv[0, :, 1], dec_seq, axis=0)
    xs = x_sample.reshape(n_s, D_MODEL)
    q_lat, q_pe, ckv_s, kpe_s128, gcy_s, gm_s, u_s = _proj_sample(xs, tabs_s, s0, s1, shared_sample, SAMPLE_PROJ_TILE)
    kpe_s = kpe_s128[:, ROPE_LO:ROPE_LO + QK_ROPE]
    o_lat = _attn_sample(page_table, q_lat, q_pe, ckv_s.reshape(n_seq, dec_seq, KV_LORA),
                         kpe_s.reshape(n_seq, dec_seq, QK_ROPE), cache_ckv[0],
                         jnp.swapaxes(cache_kpe[0], 1, 2))
    y_sample = _post((o_lat, w_uv_p), xs, gcy_s, gm_s, post_w, DENSE_TILE, sample=True).reshape(
        n_seq, dec_seq, D_MODEL)

    new_kpe_prompt = jnp.swapaxes(
        jnp.concatenate([jnp.broadcast_to(kpet_meta, (nb, QK_ROPE, N_META)), kpet_p], axis=2), 1, 2)[None]
    new_conv_sample = u_s.reshape(n_seq, dec_seq, CONV_DIM)[:, dec_seq - (CONV_K - 1):][None]
    return (y_prompt, y_sample, ckv_p[None], new_kpe_prompt, conv_p[None],
            ckv_s.reshape(n_seq, dec_seq, KV_LORA)[None], kpe_s.reshape(n_seq, dec_seq, QK_ROPE)[None],
            new_conv_sample)
```

```python
import functools

import jax
import jax.numpy as jnp
from jax import lax
from jax.experimental import pallas as pl
from jax.experimental.pallas import tpu as pltpu

D_MODEL = 1024
N_META = 16
HEADS = 8
Q_LORA = 384
KV_LORA = 256
QK_NOPE = 64
QK_ROPE = 32
V_HEAD = 64
CONV_DIM = 512
CONV_K = 3
D_FF = 4096
PAGE = 128
ROPE_THETA = 10000.0
LN_EPS = 1e-5
RMS_EPS = 1e-6
DEPTH = 1
ALPHA = (2.0 * DEPTH) ** 0.25
SM_SCALE = (QK_NOPE + QK_ROPE) ** -0.5
LOG2_E = 1.4426950408889634

SAMPLE_CHUNKS = 4
HEAD_PAD = 128
ROPE_LO = QK_NOPE
HALF = QK_ROPE // 2

C_CQ = 0
C_KPE = C_CQ + Q_LORA
C_CKV = C_KPE + HEAD_PAD
C_GB = C_CKV + KV_LORA
C_GC = C_GB + CONV_DIM
C_CH = C_GC + CONV_DIM
C_GCV = C_CH + CONV_DIM
C_GML = C_GCV + D_MODEL
C_END = C_GML + D_MODEL

VMEM_LIMIT = 56 * 1024 * 1024

DENSE_TILE = 512
SAMPLE_PROJ_TILE = 256
QUERY_TILE = 256
QUERY_TILES_PER_STEP = 8

F32 = jnp.float32
BF16 = jnp.bfloat16


def _dot(a, b):
    return jnp.dot(a, b, preferred_element_type=F32)


def _dot_nt(a, b):
    return lax.dot_general(a, b, (((1,), (1,)), ((), ())), preferred_element_type=F32)


def _layer_norm(x, g, b):
    mu = jnp.mean(x, axis=-1, keepdims=True)
    xc = x - mu
    var = jnp.mean(xc * xc, axis=-1, keepdims=True)
    return xc * lax.rsqrt(var + LN_EPS) * g + b


def _rms_norm(x, g):
    return x * lax.rsqrt(jnp.mean(x * x, axis=-1, keepdims=True) + RMS_EPS) * g


def _rope_group(x, c, a, b):
    return x * c + pltpu.roll(x, HEAD_PAD - HALF, 1) * a + pltpu.roll(x, HALF, 1) * b


def _proj_seq_kernel(x_ref, lng_ref, lnb_ref, w_ref, qng_ref, wuq_ref, kvg_ref, wuk_ref, wuvt_ref,
                     cw_ref, wco_ref, qcos_ref, qsin_ref, kc_ref, ka_ref, kb_ref, init_ref, ckv_base_ref,
                     qt_ref, k_ref, vt_ref, ckv_ref, kpet_ref, gcy_ref, gm_ref, cst_ref, ubuf, *, q_scale):
    del ckv_base_ref
    t = pl.program_id(0)
    b = pl.program_id(1)
    rows = x_ref.shape[0]

    @pl.when(t == 0)
    def _():
        ubuf[6:8, :] = init_ref[...]

    @pl.when(t > 0)
    def _():
        ubuf[6:8, :] = cst_ref[b]

    hb = _layer_norm(x_ref[...], lng_ref[...], lnb_ref[...]).astype(BF16)
    z_cq_kpe = _dot(hb, w_ref[:, C_CQ:C_CKV])
    z_cq = z_cq_kpe[:, C_CQ:C_KPE]
    z_kpe = z_cq_kpe[:, C_KPE:C_CKV]
    z_ckv = _dot(hb, w_ref[:, C_CKV:C_GB])
    gate_b = _dot(hb, w_ref[:, C_GB:C_GC])
    gate_c = _dot(hb, w_ref[:, C_GC:C_CH])
    conv_h = _dot(hb, w_ref[:, C_CH:C_GCV])
    z_gcv = _dot(hb, w_ref[:, C_GCV:C_GML])
    z_gml = _dot(hb, w_ref[:, C_GML:C_END])
    q_t = _dot_nt(wuq_ref[...], _rms_norm(z_cq, qng_ref[...]).astype(BF16))
    ckv = _rms_norm(z_ckv, kvg_ref[...])
    ckvb = ckv.astype(BF16)
    kn = _dot(ckvb, wuk_ref[...])
    v_t = _dot_nt(wuvt_ref[...], ckvb)
    u = gate_c * conv_h
    ubuf[8:8 + rows, :] = u
    cst_ref[b] = u[rows - 2:rows, :]
    conv_out = (cw_ref[0:1, :] * ubuf[6:6 + rows, :] + cw_ref[1:2, :] * ubuf[7:7 + rows, :]
                + cw_ref[2:3, :] * u)
    y_conv = _dot((gate_b * conv_out).astype(BF16), wco_ref[...])

    kpe = _rope_group(z_kpe, kc_ref[...], ka_ref[...], kb_ref[...])
    q_cos, q_sin = qcos_ref[...], qsin_ref[...]
    for g in range(HEADS):
        group = slice(g * HEAD_PAD, (g + 1) * HEAD_PAD)
        k_ref[:, group] = (kn[:, group] + kpe).astype(BF16)
        lo = g * HEAD_PAD + ROPE_LO
        x1, x2 = q_t[lo:lo + HALF], q_t[lo + HALF:lo + 2 * HALF]
        qt_ref[g * HEAD_PAD:lo, :] = (q_t[g * HEAD_PAD:lo] * q_scale).astype(BF16)
        qt_ref[lo:lo + HALF, :] = (x1 * q_cos - x2 * q_sin).astype(BF16)
        qt_ref[lo + HALF:lo + 2 * HALF, :] = (x2 * q_cos + x1 * q_sin).astype(BF16)
        qt_ref[lo + 2 * HALF:(g + 1) * HEAD_PAD, :] = jnp.zeros((HEAD_PAD - ROPE_LO - QK_ROPE, rows), BF16)
    ckv_ref[...] = ckv.reshape(ckv_ref.shape)
    kpet_ref[...] = kpe.T[ROPE_LO:ROPE_LO + QK_ROPE, :]
    vt_ref[...] = v_t.astype(BF16)
    gcy_ref[...] = (jax.nn.sigmoid(z_gcv) * y_conv).astype(BF16)
    gm_ref[...] = jax.nn.sigmoid(z_gml).astype(BF16)


def _proj_sample_kernel(x_ref, lng_ref, lnb_ref, w_ref, qng_ref, wuq_ref, kvg_ref, wukt_ref,
                        cw_ref, wco_ref, qc_ref, qa_ref, qb_ref, kc_ref, ka_ref, kb_ref,
                        s0_ref, s1_ref,
                        qlat_ref, qpe_ref, ckv_ref, kpe_ref, gcy_ref, gm_ref, u_ref, ubuf):
    rows = x_ref.shape[0]
    n_seq = rows // 8
    hb = _layer_norm(x_ref[...], lng_ref[...], lnb_ref[...]).astype(BF16)
    z_cq_kpe = _dot(hb, w_ref[:, C_CQ:C_CKV])
    z_ckv = _dot(hb, w_ref[:, C_CKV:C_GB])
    gate_b = _dot(hb, w_ref[:, C_GB:C_GC])
    gate_c = _dot(hb, w_ref[:, C_GC:C_CH])
    conv_h = _dot(hb, w_ref[:, C_CH:C_GCV])
    z_gcv = _dot(hb, w_ref[:, C_GCV:C_GML])
    z_gml = _dot(hb, w_ref[:, C_GML:C_END])
    q = _dot(_rms_norm(z_cq_kpe[:, C_CQ:C_KPE], qng_ref[...]).astype(BF16), wuq_ref[...])
    u = gate_c * conv_h
    u_ref[...] = u
    ubuf[8:8 + rows, :] = u
    ubuf[6:8, :] = jnp.zeros((2, CONV_DIM), F32)
    tok = lax.broadcasted_iota(jnp.int32, (rows, 1), 0) % 8
    um1 = jnp.where(tok == 0, s1_ref[...], ubuf[7:7 + rows, :])
    um2 = jnp.where(tok == 0, s0_ref[...], jnp.where(tok == 1, s1_ref[...], ubuf[6:6 + rows, :]))
    conv_out = cw_ref[0:1, :] * um2 + cw_ref[1:2, :] * um1 + cw_ref[2:3, :] * u
    y_conv = _dot((gate_b * conv_out).astype(BF16), wco_ref[...])

    for g in range(HEADS):
        qg = _rope_group(q[:, g * HEAD_PAD:(g + 1) * HEAD_PAD], qc_ref[...], qa_ref[...], qb_ref[...])
        qlat_ref[:, g] = _dot(qg.astype(BF16), wukt_ref[g]).reshape(n_seq, 8, KV_LORA)
        qpe_ref[:, g] = qg[:, ROPE_LO:ROPE_LO + QK_ROPE].reshape(n_seq, 8, QK_ROPE)
    ckv_ref[...] = _rms_norm(z_ckv, kvg_ref[...])
    kpe_ref[...] = _rope_group(z_cq_kpe[:, C_KPE:C_CKV], kc_ref[...], ka_ref[...], kb_ref[...])
    gcy_ref[...] = (jax.nn.sigmoid(z_gcv) * y_conv).astype(BF16)
    gm_ref[...] = jax.nn.sigmoid(z_gml).astype(BF16)


def _attn_prompt_kernel(qt_ref, k_ref, vt_ref, km_ref, vmt_ref, o_ref, m_ref, acc_ref, *, tq):
    tiles = qt_ref.shape[1] // tq
    for sub in range(tiles):
        rows = slice(sub * tq, (sub + 1) * tq)
        _attn_prompt_tile(pl.program_id(1) * tiles + sub, qt_ref.at[:, rows], k_ref, vt_ref, km_ref, vmt_ref,
                          o_ref.at[rows], m_ref, acc_ref)


def _attn_prompt_tile(qi, qt_ref, k_ref, vt_ref, km_ref, vmt_ref, o_ref, m_ref, acc_ref):
    tq = qt_ref.shape[1]
    hsl = lambda h: slice(h * HEAD_PAD, (h + 1) * HEAD_PAD)
    vsl = lambda h: slice(h * V_HEAD, (h + 1) * V_HEAD)

    def values_and_ones(v_t):
        return jnp.concatenate([v_t, jnp.ones((16, v_t.shape[1]), BF16)], axis=0)


    diag = pl.ds(pl.multiple_of(qi * tq, tq), tq)
    key_idx = lax.broadcasted_iota(jnp.int32, (tq, tq), 0)
    query_idx = lax.broadcasted_iota(jnp.int32, (tq, tq), 1)
    causal = key_idx <= query_idx
    scores = [_dot(jnp.concatenate([k_ref[diag, hsl(h)], km_ref[:, hsl(h)]], axis=0), qt_ref[hsl(h), :])
              for h in range(HEADS)]
    probs = []
    for h in range(HEADS):
        s_diag = jnp.where(causal, scores[h][0:tq], -jnp.inf)
        s_meta = scores[h][tq:tq + N_META]
        m = jnp.maximum(jnp.max(s_diag, axis=0, keepdims=True), jnp.max(s_meta, axis=0, keepdims=True))
        m_ref[h] = m
        probs.append((jnp.exp2(s_diag - m).astype(BF16), jnp.exp2(s_meta - m).astype(BF16)))
    for h in range(HEADS):
        acc_ref[h] = (_dot(values_and_ones(vt_ref[vsl(h), diag]), probs[h][0])
                      + _dot(values_and_ones(vmt_ref[vsl(h), :]), probs[h][1]))

    def visible_keys(k0, n_keys):
        keys = pl.ds(pl.multiple_of(k0, tq), n_keys)
        scores = [_dot(k_ref[keys, hsl(h)], qt_ref[hsl(h), :]) for h in range(HEADS)]
        alphas, probs = [], []
        for h in range(HEADS):
            m_prev = m_ref[h]
            m_new = jnp.maximum(m_prev, jnp.max(scores[h], axis=0, keepdims=True))
            m_ref[h] = m_new
            alphas.append(jnp.exp2(m_prev - m_new))
            probs.append(jnp.exp2(scores[h] - m_new).astype(BF16))
        for h in range(HEADS):
            acc_ref[h] = alphas[h] * acc_ref[h] + _dot(values_and_ones(vt_ref[vsl(h), keys]), probs[h])

    def pair_body(i, carry):
        visible_keys(i * (2 * tq), 2 * tq)
        return carry

    lax.fori_loop(0, qi // 2, pair_body, 0)

    @pl.when(qi % 2 == 1)
    def _():
        visible_keys((qi - 1) * tq, tq)

    o_t = jnp.concatenate(
        [acc_ref[h, 0:V_HEAD] / acc_ref[h, V_HEAD:V_HEAD + 1] for h in range(HEADS)], axis=0)
    o_ref[...] = o_t.T.astype(BF16)


def _attn_sample_kernel(pt_ref, qlat_ref, qpe_ref, ckvn_ref, kpen_ref, ckv_hbm, kpet_hbm, o_ref,
                        cbuf, kbuf, csem, ksem, *, n_pages):
    s = pl.program_id(0)
    slot = s % 2

    def page_copies(seq, slot_, p):
        pg = pt_ref[seq * n_pages + p]
        tokens = pl.ds(p * PAGE, PAGE)
        return (pltpu.make_async_copy(ckv_hbm.at[pg], cbuf.at[slot_, tokens], csem.at[slot_]),
                pltpu.make_async_copy(kpet_hbm.at[pg], kbuf.at[slot_, :, tokens], ksem.at[slot_]))

    def for_each_page(seq, slot_, fn):
        for p in range(n_pages):
            for copy in page_copies(seq, slot_, p):
                fn(copy)

    @pl.when(s == 0)
    def _():
        for_each_page(0, 0, lambda copy: copy.start())

    @pl.when(s + 1 < pl.num_programs(0))
    def _():
        for_each_page(s + 1, 1 - slot, lambda copy: copy.start())

    for_each_page(s, slot, lambda copy: copy.wait())

    n_rows = HEADS * 8
    ql = qlat_ref[0].reshape(n_rows, KV_LORA).astype(BF16)
    qp = qpe_ref[0].reshape(n_rows, QK_ROPE).astype(BF16)
    cn = ckvn_ref[0].astype(BF16)
    s_new = _dot_nt(ql, cn) + _dot_nt(qp, kpen_ref[0].astype(BF16))
    tok = lax.broadcasted_iota(jnp.int32, (n_rows, 8), 0) % 8
    key = lax.broadcasted_iota(jnp.int32, (n_rows, 8), 1)
    s_new = jnp.where(key <= tok, s_new, -jnp.inf)

    ql_pad = jnp.concatenate([ql, jnp.zeros((HEAD_PAD - n_rows, KV_LORA), BF16)], axis=0)
    chunk_keys = (n_pages // SAMPLE_CHUNKS) * PAGE

    def chunk_scores(i):
        keys = slice(i * chunk_keys, (i + 1) * chunk_keys)
        c = cbuf[slot, keys, :].astype(BF16)
        s = _dot_nt(c, ql_pad).T[0:n_rows] + _dot(qp, kbuf[slot, :, keys].astype(BF16))
        return c, s

    m = jnp.max(s_new, axis=-1, keepdims=True)
    p_new = jnp.exp(s_new - m)
    denom = jnp.sum(p_new, axis=-1, keepdims=True)
    acc = _dot(p_new.astype(BF16), cn)
    pending = chunk_scores(0)
    for i in range(SAMPLE_CHUNKS):
        c, s = pending
        if i + 1 < SAMPLE_CHUNKS:
            pending = chunk_scores(i + 1)
        m_new = jnp.maximum(m, jnp.max(s, axis=-1, keepdims=True))
        alpha = jnp.exp(m - m_new)
        p = jnp.exp(s - m_new)
        denom = alpha * denom + jnp.sum(p, axis=-1, keepdims=True)
        acc = alpha * acc + _dot(p.astype(BF16), c)
        m = m_new
    o_ref[...] = (acc / denom).reshape(HEADS, 8, KV_LORA)


def _post_kernel(*refs, sample):
    if sample:
        (olat_ref, wuvp_ref, x_ref, gcy_ref, gm_ref, lng_ref, lnb_ref, womla_ref, wo_ref,
         l1g_ref, l1b_ref, wup_ref, wdn_ref, l2g_ref, l2b_ref, y_ref) = refs
        pairs = []
        for pair in range(HEADS // 2):
            pairs.append(
                _dot(olat_ref[2 * pair].astype(BF16), wuvp_ref[2 * pair])
                + _dot(olat_ref[2 * pair + 1].astype(BF16), wuvp_ref[2 * pair + 1]))
        y_mla = sum(
            _dot(pairs[pair].astype(BF16), womla_ref[pair * HEAD_PAD:(pair + 1) * HEAD_PAD, :])
            for pair in range(HEADS // 2))
    else:
        (o_ref, x_ref, gcy_ref, gm_ref, lng_ref, lnb_ref, womla_ref, wo_ref,
         l1g_ref, l1b_ref, wup_ref, wdn_ref, l2g_ref, l2b_ref, y_ref) = refs
        y_mla = _dot(o_ref[...], womla_ref[...])
    rows = x_ref.shape[0]
    halves = [slice(0, rows // 2), slice(rows // 2, rows)]
    mixes = []
    for r in halves:
        mix_in = gcy_ref[r, :].astype(F32) + gm_ref[r, :].astype(F32) * y_mla[r, :]
        mixes.append(_dot(mix_in.astype(BF16), wo_ref[...]))
    h1s = []
    for r, mix in zip(halves, mixes):
        h = _layer_norm(x_ref[r, :], lng_ref[...], lnb_ref[...])
        h1s.append(_layer_norm(ALPHA * h + mix, l1g_ref[...], l1b_ref[...]))
    h1bs = [h1.astype(BF16) for h1 in h1s]
    ffs = [None, None]
    chunk = D_MODEL
    for c in range(D_FF // chunk):
        acts = [jnp.maximum(_dot(h1b, wup_ref[:, c * chunk:(c + 1) * chunk]), 0.0) for h1b in h1bs]
        for i, a in enumerate(acts):
            part = _dot((a * a).astype(BF16), wdn_ref[c * chunk:(c + 1) * chunk, :])
            ffs[i] = part if ffs[i] is None else ffs[i] + part
    for r, h1, ff in zip(halves, h1s, ffs):
        y_ref[r, :] = _layer_norm(ALPHA * h1 + ff, l2g_ref[...], l2b_ref[...])


def _full(shape):
    n = len(shape)
    return pl.BlockSpec(shape, lambda *_: (0,) * n)


def _rope_cos_sin(pos, transposed=False):
    inv_freq = ROPE_THETA ** (-2.0 * jnp.arange(HALF, dtype=F32) / QK_ROPE)
    ang = pos.astype(F32)[:, None] * inv_freq[None, :]
    cos, sin = jnp.cos(ang), jnp.sin(ang)
    return (cos.T, sin.T) if transposed else (cos, sin)


def _rope_tables(pos, scale):
    cos, sin = _rope_cos_sin(pos)
    n = pos.shape[0]
    ones = jnp.ones((n, ROPE_LO), F32)
    z_lo = jnp.zeros((n, ROPE_LO), F32)
    z_hi = jnp.zeros((n, HEAD_PAD - ROPE_LO - QK_ROPE), F32)
    z_half = jnp.zeros((n, HALF), F32)
    c = jnp.concatenate([ones, cos, cos, z_hi + 1.0], axis=1)
    a = jnp.concatenate([z_lo, -sin, z_half, z_hi], axis=1)
    b = jnp.concatenate([z_lo, z_half, sin, z_hi], axis=1)
    return c * scale, a * scale, b * scale


def _pack_weights(w_in, w_uq, w_uk, w_uv):
    offs = [0]
    for s in (Q_LORA, KV_LORA, QK_ROPE, CONV_DIM, CONV_DIM, CONV_DIM, D_MODEL, D_MODEL):
        offs.append(offs[-1] + s)
    pieces = [w_in[:, offs[i]:offs[i + 1]] for i in range(8)]
    kpe_pad = jnp.pad(pieces[2], ((0, 0), (ROPE_LO, HEAD_PAD - ROPE_LO - QK_ROPE)))
    w_pack = jnp.concatenate([pieces[0], kpe_pad, pieces[1]] + pieces[3:], axis=1).astype(BF16)
    head_pad = HEAD_PAD - QK_NOPE - QK_ROPE
    w_uq_p = jnp.pad(w_uq.reshape(Q_LORA, HEADS, QK_NOPE + QK_ROPE), ((0, 0), (0, 0), (0, head_pad)))
    w_uq_p = w_uq_p.reshape(Q_LORA, HEADS * HEAD_PAD).astype(BF16)
    w_uk_p = jnp.pad(w_uk, ((0, 0), (0, 0), (0, HEAD_PAD - QK_NOPE)))
    w_uk_p = w_uk_p.reshape(KV_LORA, HEADS * HEAD_PAD).astype(BF16)
    w_uk_t = jnp.pad(jnp.transpose(w_uk, (1, 2, 0)), ((0, 0), (0, HEAD_PAD - QK_NOPE), (0, 0))).astype(BF16)
    w_uv_t = w_uv.reshape(KV_LORA, HEADS * V_HEAD).T.astype(BF16)
    w_uv_even = jnp.pad(w_uv, ((0, 0), (0, 0), (0, V_HEAD)))
    w_uv_odd = jnp.pad(w_uv, ((0, 0), (0, 0), (V_HEAD, 0)))
    is_even = (jnp.arange(HEADS) % 2 == 0)[None, :, None]
    w_uv_p = jnp.transpose(jnp.where(is_even, w_uv_even, w_uv_odd), (1, 0, 2)).astype(BF16)
    return w_pack, w_uq_p, w_uk_p, w_uk_t, w_uv_t, w_uv_p


def _proj_seq(x, pos, init, shared, tile, ckv_base):
    nb, n, _ = x.shape
    row_offset = ckv_base.shape[1] - n
    nt = n // tile
    q_scale = SM_SCALE * LOG2_E
    q_tabs = tuple(t * q_scale for t in _rope_cos_sin(pos, transposed=True))
    k_tabs = _rope_tables(pos, 1.0)
    row_spec = lambda width: pl.BlockSpec((None, tile, width), lambda t, b: (b, t, 0))
    tab_spec = pl.BlockSpec((tile, HEAD_PAD), lambda t, b: (t, 0))
    tab_t_spec = pl.BlockSpec((HALF, tile), lambda t, b: (0, t))
    ckv_spec = pl.BlockSpec((pl.Element(1), pl.Element(tile), pl.Element(KV_LORA)),
                            lambda t, b: (b, pl.multiple_of(row_offset + t * tile, 8), 0))
    in_specs = ([row_spec(D_MODEL)] + [_full(a.shape) for a in shared] + [tab_t_spec] * 2 + [tab_spec] * 3
                + [_full(init.shape), pl.BlockSpec(memory_space=pl.ANY)])
    ins = (x, *shared, *q_tabs, *k_tabs, init, ckv_base)
    out_shape = (
        jax.ShapeDtypeStruct((nb, HEADS * HEAD_PAD, n), BF16),
        jax.ShapeDtypeStruct((nb, n, HEADS * HEAD_PAD), BF16),
        jax.ShapeDtypeStruct((nb, HEADS * V_HEAD, n), BF16),
        jax.ShapeDtypeStruct(ckv_base.shape, F32),
        jax.ShapeDtypeStruct((nb, QK_ROPE, n), F32),
        jax.ShapeDtypeStruct((nb, n, D_MODEL), BF16),
        jax.ShapeDtypeStruct((nb, n, D_MODEL), BF16),
        jax.ShapeDtypeStruct((nb, CONV_K - 1, CONV_DIM), F32),
    )
    col_spec = lambda height: pl.BlockSpec((None, height, tile), lambda t, b: (b, 0, t))
    out_specs = (col_spec(HEADS * HEAD_PAD), row_spec(HEADS * HEAD_PAD), col_spec(HEADS * V_HEAD),
                 ckv_spec, col_spec(QK_ROPE), row_spec(D_MODEL), row_spec(D_MODEL),
                 _full((nb, CONV_K - 1, CONV_DIM)))
    return pl.pallas_call(
        functools.partial(_proj_seq_kernel, q_scale=q_scale),
        grid=(nt, nb),
        in_specs=in_specs,
        out_specs=out_specs,
        out_shape=out_shape,
        scratch_shapes=[pltpu.VMEM((tile + 8, CONV_DIM), F32)],
        input_output_aliases={len(ins) - 1: 3},
        compiler_params=pltpu.CompilerParams(
            dimension_semantics=("arbitrary", "arbitrary"), vmem_limit_bytes=VMEM_LIMIT),
        name="proj_seq",
    )(*ins)


def _proj_sample(x, tabs, s0, s1, shared, tile):
    n = x.shape[0]
    n_seq = n // 8
    row = lambda width: pl.BlockSpec((tile, width), lambda i: (i, 0))
    seq4 = lambda width: pl.BlockSpec((tile // 8, HEADS, 8, width), lambda i: (i, 0, 0, 0))
    ins = (x, *shared, *tabs, s0, s1)
    in_specs = ([row(D_MODEL)] + [_full(a.shape) for a in shared] + [row(HEAD_PAD)] * 6
                + [row(CONV_DIM)] * 2)
    out_specs = (seq4(KV_LORA), seq4(QK_ROPE), row(KV_LORA), row(HEAD_PAD), row(D_MODEL), row(D_MODEL),
                 row(CONV_DIM))
    out_shape = (
        jax.ShapeDtypeStruct((n_seq, HEADS, 8, KV_LORA), F32),
        jax.ShapeDtypeStruct((n_seq, HEADS, 8, QK_ROPE), F32),
        jax.ShapeDtypeStruct((n, KV_LORA), F32),
        jax.ShapeDtypeStruct((n, HEAD_PAD), F32),
        jax.ShapeDtypeStruct((n, D_MODEL), BF16),
        jax.ShapeDtypeStruct((n, D_MODEL), BF16),
        jax.ShapeDtypeStruct((n, CONV_DIM), F32),
    )
    return pl.pallas_call(
        _proj_sample_kernel,
        grid=(n // tile,),
        in_specs=in_specs,
        out_specs=out_specs,
        out_shape=out_shape,
        scratch_shapes=[pltpu.VMEM((tile + 8, CONV_DIM), F32)],
        compiler_params=pltpu.CompilerParams(
            dimension_semantics=("arbitrary",), vmem_limit_bytes=VMEM_LIMIT),
        name="proj_sample",
    )(*ins)


def _attn_prompt(q_t, k, v_t, k_meta, v_meta_t, tq, tiles_per_step):
    nb, n, _ = k.shape
    step_rows = tq * tiles_per_step
    return pl.pallas_call(
        functools.partial(_attn_prompt_kernel, tq=tq),
        grid=(nb, n // step_rows),
        in_specs=[
            pl.BlockSpec((None, HEADS * HEAD_PAD, step_rows), lambda b, i: (b, 0, i)),
            pl.BlockSpec((None, n, HEADS * HEAD_PAD), lambda b, i: (b, 0, 0)),
            pl.BlockSpec((None, HEADS * V_HEAD, n), lambda b, i: (b, 0, 0)),
            _full(k_meta.shape),
            _full(v_meta_t.shape),
        ],
        out_specs=pl.BlockSpec((None, step_rows, HEADS * V_HEAD), lambda b, i: (b, i, 0)),
        out_shape=jax.ShapeDtypeStruct((nb, n, HEADS * V_HEAD), BF16),
        scratch_shapes=[pltpu.VMEM((HEADS, 1, tq), F32), pltpu.VMEM((HEADS, V_HEAD + 16, tq), F32)],
        compiler_params=pltpu.CompilerParams(
            dimension_semantics=("arbitrary", "arbitrary"), vmem_limit_bytes=VMEM_LIMIT),
        name="attn_prompt",
    )(q_t, k, v_t, k_meta, v_meta_t)


def _attn_sample(page_table, q_lat, q_pe, ckv_new, kpe_new, ckv_pool, kpe_pool_t):
    n_seq, n_pages = page_table.shape
    seq4 = lambda width: pl.BlockSpec((1, HEADS, 8, width), lambda s, pt: (s, 0, 0, 0))
    seq3 = lambda width: pl.BlockSpec((1, 8, width), lambda s, pt: (s, 0, 0))
    hbm = pl.BlockSpec(memory_space=pl.ANY)
    grid_spec = pltpu.PrefetchScalarGridSpec(
        num_scalar_prefetch=1,
        grid=(n_seq,),
        in_specs=[seq4(KV_LORA), seq4(QK_ROPE), seq3(KV_LORA), seq3(QK_ROPE), hbm, hbm],
        out_specs=pl.BlockSpec((HEADS, 8, KV_LORA), lambda s, pt: (0, s, 0)),
        scratch_shapes=[pltpu.VMEM((2, n_pages * PAGE, KV_LORA), F32),
                        pltpu.VMEM((2, QK_ROPE, n_pages * PAGE), F32),
                        pltpu.SemaphoreType.DMA((2,)), pltpu.SemaphoreType.DMA((2,))],
    )
    return pl.pallas_call(
        functools.partial(_attn_sample_kernel, n_pages=n_pages),
        grid_spec=grid_spec,
        out_shape=jax.ShapeDtypeStruct((HEADS, n_seq * 8, KV_LORA), F32),
        compiler_params=pltpu.CompilerParams(
            dimension_semantics=("arbitrary",), vmem_limit_bytes=VMEM_LIMIT),
        name="attn_sample",
    )(page_table.reshape(-1), q_lat, q_pe, ckv_new, kpe_new, ckv_pool, kpe_pool_t)


def _post(front, x, gcy, gm, weights, tile, sample):
    n = x.shape[0]
    row = lambda width: pl.BlockSpec((tile, width), lambda i: (i, 0))
    if sample:
        o_lat, w_uv_p = front
        front_specs = [pl.BlockSpec((HEADS, tile, KV_LORA), lambda i: (0, i, 0)), _full(w_uv_p.shape)]
    else:
        front_specs = [row(HEADS * V_HEAD)]
    in_specs = front_specs + [row(D_MODEL), row(D_MODEL), row(D_MODEL)] + [_full(w.shape) for w in weights]
    return pl.pallas_call(
        functools.partial(_post_kernel, sample=sample),
        grid=(n // tile,),
        in_specs=in_specs,
        out_specs=row(D_MODEL),
        out_shape=jax.ShapeDtypeStruct((n, D_MODEL), F32),
        compiler_params=pltpu.CompilerParams(
            dimension_semantics=("arbitrary",), vmem_limit_bytes=VMEM_LIMIT),
        name="post_sample" if sample else "post_prompt",
    )(*front, x, gcy, gm, *weights)


def kernel(x_prompt, x_sample, cache_ckv, cache_kpe, state_conv, page_table, meta_tokens, ln_emb_g, ln_emb_b, w_in, q_norm_g, w_uq, kv_norm_g, w_uk, w_uv, w_o_mla, conv_w, w_conv_out, w_o, ln1_g, ln1_b, w_up, w_down, ln2_g, ln2_b):
    assert w_in.shape[0] == DEPTH
    nb, seq, _ = x_prompt.shape
    n_seq, dec_seq, _ = x_sample.shape
    assert dec_seq == 8
    past_len = page_table.shape[1] * PAGE

    row2 = lambda a: a.reshape(1, -1).astype(F32)
    w_pack, w_uq_p, w_uk_p, w_uk_t, w_uv_t, w_uv_p = _pack_weights(w_in[0], w_uq[0], w_uk[0], w_uv[0])
    w_co = w_conv_out[0].astype(BF16)
    lng, lnb = row2(ln_emb_g), row2(ln_emb_b)
    qng, kvg = row2(q_norm_g[0]), row2(kv_norm_g[0])
    shared_seq = (lng, lnb, w_pack, qng, w_uq_p.T, kvg, w_uk_p, w_uv_t, conv_w[0], w_co)
    shared_sample = (lng, lnb, w_pack, qng, w_uq_p, kvg, w_uk_t, conv_w[0], w_co)
    post_w = (lng, lnb, w_o_mla[0].astype(BF16), w_o[0].astype(BF16), row2(ln1_g[0]), row2(ln1_b[0]),
              w_up[0].astype(BF16), w_down[0].astype(BF16), row2(ln2_g[0]), row2(ln2_b[0]))

    zero_state = jnp.zeros((CONV_K - 1, CONV_DIM), F32)
    _, k_meta, vt_meta, ckv_meta, kpet_meta, _, _, conv_meta = _proj_seq(
        meta_tokens[None], jnp.arange(N_META), zero_state, shared_seq, N_META,
        jnp.zeros((1, N_META, KV_LORA), F32))

    q_t, k, v_t, ckv_p, kpet_p, gcy_p, gm_p, conv_p = _proj_seq(
        x_prompt, N_META + jnp.arange(seq), conv_meta[0], shared_seq, DENSE_TILE,
        jnp.pad(jnp.broadcast_to(ckv_meta, (nb, N_META, KV_LORA)), ((0, 0), (0, seq), (0, 0))))
    o_p = _attn_prompt(q_t, k, v_t, k_meta[0], vt_meta[0], QUERY_TILE, QUERY_TILES_PER_STEP)
    n_p = nb * seq
    y_prompt = _post((o_p.reshape(n_p, -1),), x_prompt.reshape(n_p, D_MODEL), gcy_p.reshape(n_p, D_MODEL),
                     gm_p.reshape(n_p, D_MODEL), post_w, DENSE_TILE, sample=False).reshape(nb, seq, D_MODEL)

    n_s = n_seq * dec_seq
    pos_s = past_len + jnp.arange(dec_seq)
    tabs_s = tuple(jnp.tile(t, (n_seq, 1)) for t in _rope_tables(pos_s, SM_SCALE) + _rope_tables(pos_s, 1.0))
    s0 = jnp.repeat(state_conv[0, :, 0], dec_seq, axis=0)
    s1 = jnp.repeat(state_conv[0, :, 1], dec_seq, axis=0)
    xs = x_sample.reshape(n_s, D_MODEL)
    q_lat, q_pe, ckv_s, kpe_s128, gcy_s, gm_s, u_s = _proj_sample(xs, tabs_s, s0, s1, shared_sample, SAMPLE_PROJ_TILE)
    kpe_s = kpe_s128[:, ROPE_LO:ROPE_LO + QK_ROPE]
    o_lat = _attn_sample(page_table, q_lat, q_pe, ckv_s.reshape(n_seq, dec_seq, KV_LORA),
                         kpe_s.reshape(n_seq, dec_seq, QK_ROPE), cache_ckv[0],
                         jnp.swapaxes(cache_kpe[0], 1, 2))
    y_sample = _post((o_lat, w_uv_p), xs, gcy_s, gm_s, post_w, DENSE_TILE, sample=True).reshape(
        n_seq, dec_seq, D_MODEL)

    new_kpe_prompt = jnp.swapaxes(
        jnp.concatenate([jnp.broadcast_to(kpet_meta, (nb, QK_ROPE, N_META)), kpet_p], axis=2), 1, 2)[None]
    new_conv_sample = u_s.reshape(n_seq, dec_seq, CONV_DIM)[:, dec_seq - (CONV_K - 1):][None]
    return (y_prompt, y_sample, ckv_p[None], new_kpe_prompt, conv_p[None],
            ckv_s.reshape(n_seq, dec_seq, KV_LORA)[None], kpe_s.reshape(n_seq, dec_seq, QK_ROPE)[None],
            new_conv_sample)
```

```python
import functools

import jax
import jax.numpy as jnp
from jax import lax
from jax.experimental import pallas as pl
from jax.experimental.pallas import tpu as pltpu

D_MODEL = 1024
N_META = 16
HEADS = 8
Q_LORA = 384
KV_LORA = 256
QK_NOPE = 64
QK_ROPE = 32
V_HEAD = 64
CONV_DIM = 512
CONV_K = 3
D_FF = 4096
PAGE = 128
ROPE_THETA = 10000.0
LN_EPS = 1e-5
RMS_EPS = 1e-6
DEPTH = 1
ALPHA = (2.0 * DEPTH) ** 0.25
SM_SCALE = (QK_NOPE + QK_ROPE) ** -0.5
LOG2_E = 1.4426950408889634

SAMPLE_CHUNKS = 4
HEAD_PAD = 128
ROPE_LO = QK_NOPE
HALF = QK_ROPE // 2

C_CQ = 0
C_KPE = C_CQ + Q_LORA
C_CKV = C_KPE + HEAD_PAD
C_GB = C_CKV + KV_LORA
C_GC = C_GB + CONV_DIM
C_CH = C_GC + CONV_DIM
C_GCV = C_CH + CONV_DIM
C_GML = C_GCV + D_MODEL
C_END = C_GML + D_MODEL

VMEM_LIMIT = 56 * 1024 * 1024

DENSE_TILE = 512
SAMPLE_PROJ_TILE = 256
QUERY_TILE = 256
QUERY_TILES_PER_STEP = 8

F32 = jnp.float32
BF16 = jnp.bfloat16


def _dot(a, b):
    return jnp.dot(a, b, preferred_element_type=F32)


def _dot_nt(a, b):
    return lax.dot_general(a, b, (((1,), (1,)), ((), ())), preferred_element_type=F32)


def _layer_norm(x, g, b):
    mu = jnp.mean(x, axis=-1, keepdims=True)
    xc = x - mu
    var = jnp.mean(xc * xc, axis=-1, keepdims=True)
    return xc * lax.rsqrt(var + LN_EPS) * g + b


def _rms_norm(x, g):
    return x * lax.rsqrt(jnp.mean(x * x, axis=-1, keepdims=True) + RMS_EPS) * g


def _rope_group(x, c, a, b):
    return x * c + pltpu.roll(x, HEAD_PAD - HALF, 1) * a + pltpu.roll(x, HALF, 1) * b


def _proj_seq_kernel(x_ref, lng_ref, lnb_ref, w_ref, qng_ref, wuq_ref, kvg_ref, wuk_ref, wuvt_ref,
                     cw_ref, wco_ref, qcos_ref, qsin_ref, kc_ref, ka_ref, kb_ref, init_ref, ckv_base_ref,
                     qt_ref, k_ref, vt_ref, ckv_ref, kpet_ref, gcy_ref, gm_ref, cst_ref, ubuf, *, q_scale):
    del ckv_base_ref
    t = pl.program_id(0)
    b = pl.program_id(1)
    rows = x_ref.shape[0]

    @pl.when(t == 0)
    def _():
        ubuf[6:8, :] = init_ref[...]

    @pl.when(t > 0)
    def _():
        ubuf[6:8, :] = cst_ref[b]

    hb = _layer_norm(x_ref[...], lng_ref[...], lnb_ref[...]).astype(BF16)
    z_cq_kpe = _dot(hb, w_ref[:, C_CQ:C_CKV])
    z_cq = z_cq_kpe[:, C_CQ:C_KPE]
    z_kpe = z_cq_kpe[:, C_KPE:C_CKV]
    z_ckv = _dot(hb, w_ref[:, C_CKV:C_GB])
    gate_b = _dot(hb, w_ref[:, C_GB:C_GC])
    gate_c = _dot(hb, w_ref[:, C_GC:C_CH])
    conv_h = _dot(hb, w_ref[:, C_CH:C_GCV])
    z_gcv = _dot(hb, w_ref[:, C_GCV:C_GML])
    z_gml = _dot(hb, w_ref[:, C_GML:C_END])
    q_t = _dot_nt(wuq_ref[...], _rms_norm(z_cq, qng_ref[...]).astype(BF16))
    ckv = _rms_norm(z_ckv, kvg_ref[...])
    ckvb = ckv.astype(BF16)
    kn = _dot(ckvb, wuk_ref[...])
    v_t = _dot_nt(wuvt_ref[...], ckvb)
    u = gate_c * conv_h
    ubuf[8:8 + rows, :] = u
    cst_ref[b] = u[rows - 2:rows, :]
    conv_out = (cw_ref[0:1, :] * ubuf[6:6 + rows, :] + cw_ref[1:2, :] * ubuf[7:7 + rows, :]
                + cw_ref[2:3, :] * u)
    y_conv = _dot((gate_b * conv_out).astype(BF16), wco_ref[...])

    kpe = _rope_group(z_kpe, kc_ref[...], ka_ref[...], kb_ref[...])
    q_cos, q_sin = qcos_ref[...], qsin_ref[...]
    for g in range(HEADS):
        group = slice(g * HEAD_PAD, (g + 1) * HEAD_PAD)
        k_ref[:, group] = (kn[:, group] + kpe).astype(BF16)
        lo = g * HEAD_PAD + ROPE_LO
        x1, x2 = q_t[lo:lo + HALF], q_t[lo + HALF:lo + 2 * HALF]
        qt_ref[g * HEAD_PAD:lo, :] = (q_t[g * HEAD_PAD:lo] * q_scale).astype(BF16)
        qt_ref[lo:lo + HALF, :] = (x1 * q_cos - x2 * q_sin).astype(BF16)
        qt_ref[lo + HALF:lo + 2 * HALF, :] = (x2 * q_cos + x1 * q_sin).astype(BF16)
        qt_ref[lo + 2 * HALF:(g + 1) * HEAD_PAD, :] = jnp.zeros((HEAD_PAD - ROPE_LO - QK_ROPE, rows), BF16)
    ckv_ref[...] = ckv.reshape(ckv_ref.shape)
    kpet_ref[...] = kpe.T[ROPE_LO:ROPE_LO + QK_ROPE, :]
    vt_ref[...] = v_t.astype(BF16)
    gcy_ref[...] = (jax.nn.sigmoid(z_gcv) * y_conv).astype(BF16)
    gm_ref[...] = jax.nn.sigmoid(z_gml).astype(BF16)


def _proj_sample_kernel(x_ref, lng_ref, lnb_ref, w_ref, qng_ref, wuq_ref, kvg_ref, wukt_ref,
                        cw_ref, wco_ref, qc_ref, qa_ref, qb_ref, kc_ref, ka_ref, kb_ref,
                        s0_ref, s1_ref,
                        qlat_ref, qpe_ref, ckv_ref, kpe_ref, gcy_ref, gm_ref, u_ref, ubuf):
    rows = x_ref.shape[0]
    n_seq = rows // 8
    hb = _layer_norm(x_ref[...], lng_ref[...], lnb_ref[...]).astype(BF16)
    z_cq_kpe = _dot(hb, w_ref[:, C_CQ:C_CKV])
    z_ckv = _dot(hb, w_ref[:, C_CKV:C_GB])
    gate_b = _dot(hb, w_ref[:, C_GB:C_GC])
    gate_c = _dot(hb, w_ref[:, C_GC:C_CH])
    conv_h = _dot(hb, w_ref[:, C_CH:C_GCV])
    z_gcv = _dot(hb, w_ref[:, C_GCV:C_GML])
    z_gml = _dot(hb, w_ref[:, C_GML:C_END])
    q = _dot(_rms_norm(z_cq_kpe[:, C_CQ:C_KPE], qng_ref[...]).astype(BF16), wuq_ref[...])
    u = gate_c * conv_h
    u_ref[...] = u
    ubuf[8:8 + rows, :] = u
    ubuf[6:8, :] = jnp.zeros((2, CONV_DIM), F32)
    tok = lax.broadcasted_iota(jnp.int32, (rows, 1), 0) % 8
    um1 = jnp.where(tok == 0, s1_ref[...], ubuf[7:7 + rows, :])
    um2 = jnp.where(tok == 0, s0_ref[...], jnp.where(tok == 1, s1_ref[...], ubuf[6:6 + rows, :]))
    conv_out = cw_ref[0:1, :] * um2 + cw_ref[1:2, :] * um1 + cw_ref[2:3, :] * u
    y_conv = _dot((gate_b * conv_out).astype(BF16), wco_ref[...])

    for g in range(HEADS):
        qg = _rope_group(q[:, g * HEAD_PAD:(g + 1) * HEAD_PAD], qc_ref[...], qa_ref[...], qb_ref[...])
        qlat_ref[:, g] = _dot(qg.astype(BF16), wukt_ref[g]).reshape(n_seq, 8, KV_LORA)
        qpe_ref[:, g] = qg[:, ROPE_LO:ROPE_LO + QK_ROPE].reshape(n_seq, 8, QK_ROPE)
    ckv_ref[...] = _rms_norm(z_ckv, kvg_ref[...])
    kpe_ref[...] = _rope_group(z_cq_kpe[:, C_KPE:C_CKV], kc_ref[...], ka_ref[...], kb_ref[...])
    gcy_ref[...] = (jax.nn.sigmoid(z_gcv) * y_conv).astype(BF16)
    gm_ref[...] = jax.nn.sigmoid(z_gml).astype(BF16)


def _attn_prompt_kernel(qt_ref, k_ref, vt_ref, km_ref, vmt_ref, o_ref, m_ref, acc_ref, *, tq):
    tiles = qt_ref.shape[1] // tq
    assert tiles * tq == k_ref.shape[0]
    for qi in range(tiles):
        rows = slice(qi * tq, (qi + 1) * tq)
        _attn_prompt_tile(qi, qt_ref.at[:, rows], k_ref, vt_ref, km_ref, vmt_ref, o_ref.at[rows], m_ref, acc_ref)


def _attn_prompt_tile(qi, qt_ref, k_ref, vt_ref, km_ref, vmt_ref, o_ref, m_ref, acc_ref):
    tq = qt_ref.shape[1]
    hsl = lambda h: slice(h * HEAD_PAD, (h + 1) * HEAD_PAD)
    vsl = lambda h: slice(h * V_HEAD, (h + 1) * V_HEAD)

    def values_and_ones(v_t):
        return jnp.concatenate([v_t, jnp.ones((16, v_t.shape[1]), BF16)], axis=0)


    diag = pl.ds(qi * tq, tq)
    key_idx = lax.broadcasted_iota(jnp.int32, (tq, tq), 0)
    query_idx = lax.broadcasted_iota(jnp.int32, (tq, tq), 1)
    causal = key_idx <= query_idx
    scores = [_dot(jnp.concatenate([k_ref[diag, hsl(h)], km_ref[:, hsl(h)]], axis=0), qt_ref[hsl(h), :])
              for h in range(HEADS)]
    probs = []
    for h in range(HEADS):
        s_diag = jnp.where(causal, scores[h][0:tq], -jnp.inf)
        s_meta = scores[h][tq:tq + N_META]
        m = jnp.maximum(jnp.max(s_diag, axis=0, keepdims=True), jnp.max(s_meta, axis=0, keepdims=True))
        m_ref[h] = m
        probs.append((jnp.exp2(s_diag - m).astype(BF16), jnp.exp2(s_meta - m).astype(BF16)))
    for h in range(HEADS):
        acc_ref[h] = (_dot(values_and_ones(vt_ref[vsl(h), diag]), probs[h][0])
                      + _dot(values_and_ones(vmt_ref[vsl(h), :]), probs[h][1]))

    def visible_keys(k0, n_keys):
        keys = pl.ds(k0, n_keys)
        scores = [_dot(k_ref[keys, hsl(h)], qt_ref[hsl(h), :]) for h in range(HEADS)]
        alphas, probs = [], []
        for h in range(HEADS):
            m_prev = m_ref[h]
            m_new = jnp.maximum(m_prev, jnp.max(scores[h], axis=0, keepdims=True))
            m_ref[h] = m_new
            alphas.append(jnp.exp2(m_prev - m_new))
            probs.append(jnp.exp2(scores[h] - m_new).astype(BF16))
        for h in range(HEADS):
            acc_ref[h] = alphas[h] * acc_ref[h] + _dot(values_and_ones(vt_ref[vsl(h), keys]), probs[h])

    for i in range(qi // 2):
        visible_keys(i * (2 * tq), 2 * tq)
    if qi % 2 == 1:
        visible_keys((qi - 1) * tq, tq)

    o_t = jnp.concatenate(
        [acc_ref[h, 0:V_HEAD] / acc_ref[h, V_HEAD:V_HEAD + 1] for h in range(HEADS)], axis=0)
    o_ref[...] = o_t.T.astype(BF16)


def _attn_sample_kernel(pt_ref, qlat_ref, qpe_ref, ckvn_ref, kpen_ref, ckv_hbm, kpet_hbm, o_ref,
                        cbuf, kbuf, csem, ksem, *, n_pages):
    s = pl.program_id(0)
    slot = s % 2

    def page_copies(seq, slot_, p):
        pg = pt_ref[seq * n_pages + p]
        tokens = pl.ds(p * PAGE, PAGE)
        return (pltpu.make_async_copy(ckv_hbm.at[pg], cbuf.at[slot_, tokens], csem.at[slot_]),
                pltpu.make_async_copy(kpet_hbm.at[pg], kbuf.at[slot_, :, tokens], ksem.at[slot_]))

    def for_each_page(seq, slot_, fn):
        for p in range(n_pages):
            for copy in page_copies(seq, slot_, p):
                fn(copy)

    @pl.when(s == 0)
    def _():
        for_each_page(0, 0, lambda copy: copy.start())

    @pl.when(s + 1 < pl.num_programs(0))
    def _():
        for_each_page(s + 1, 1 - slot, lambda copy: copy.start())

    for_each_page(s, slot, lambda copy: copy.wait())

    n_rows = HEADS * 8
    ql = qlat_ref[0].reshape(n_rows, KV_LORA).astype(BF16)
    qp = qpe_ref[0].reshape(n_rows, QK_ROPE).astype(BF16)
    cn = ckvn_ref[0].astype(BF16)
    s_new = _dot_nt(ql, cn) + _dot_nt(qp, kpen_ref[0].astype(BF16))
    tok = lax.broadcasted_iota(jnp.int32, (n_rows, 8), 0) % 8
    key = lax.broadcasted_iota(jnp.int32, (n_rows, 8), 1)
    s_new = jnp.where(key <= tok, s_new, -jnp.inf)

    ql_pad = jnp.concatenate([ql, jnp.zeros((HEAD_PAD - n_rows, KV_LORA), BF16)], axis=0)
    chunk_keys = (n_pages // SAMPLE_CHUNKS) * PAGE

    def chunk_scores(i):
        keys = slice(i * chunk_keys, (i + 1) * chunk_keys)
        c = cbuf[slot, keys, :].astype(BF16)
        s = _dot_nt(c, ql_pad).T[0:n_rows] + _dot(qp, kbuf[slot, :, keys].astype(BF16))
        return c, s

    m = jnp.max(s_new, axis=-1, keepdims=True)
    p_new = jnp.exp(s_new - m)
    denom = jnp.sum(p_new, axis=-1, keepdims=True)
    acc = _dot(p_new.astype(BF16), cn)
    pending = chunk_scores(0)
    for i in range(SAMPLE_CHUNKS):
        c, s = pending
        if i + 1 < SAMPLE_CHUNKS:
            pending = chunk_scores(i + 1)
        m_new = jnp.maximum(m, jnp.max(s, axis=-1, keepdims=True))
        alpha = jnp.exp(m - m_new)
        p = jnp.exp(s - m_new)
        denom = alpha * denom + jnp.sum(p, axis=-1, keepdims=True)
        acc = alpha * acc + _dot(p.astype(BF16), c)
        m = m_new
    o_ref[...] = (acc / denom).reshape(HEADS, 8, KV_LORA)


def _post_kernel(*refs, sample):
    if sample:
        (olat_ref, wuvp_ref, x_ref, gcy_ref, gm_ref, lng_ref, lnb_ref, womla_ref, wo_ref,
         l1g_ref, l1b_ref, wup_ref, wdn_ref, l2g_ref, l2b_ref, y_ref) = refs
        pairs = []
        for pair in range(HEADS // 2):
            pairs.append(
                _dot(olat_ref[2 * pair].astype(BF16), wuvp_ref[2 * pair])
                + _dot(olat_ref[2 * pair + 1].astype(BF16), wuvp_ref[2 * pair + 1]))
        y_mla = sum(
            _dot(pairs[pair].astype(BF16), womla_ref[pair * HEAD_PAD:(pair + 1) * HEAD_PAD, :])
            for pair in range(HEADS // 2))
    else:
        (o_ref, x_ref, gcy_ref, gm_ref, lng_ref, lnb_ref, womla_ref, wo_ref,
         l1g_ref, l1b_ref, wup_ref, wdn_ref, l2g_ref, l2b_ref, y_ref) = refs
        y_mla = _dot(o_ref[...], womla_ref[...])
    rows = x_ref.shape[0]
    halves = [slice(0, rows // 2), slice(rows // 2, rows)]
    mixes = []
    for r in halves:
        mix_in = gcy_ref[r, :].astype(F32) + gm_ref[r, :].astype(F32) * y_mla[r, :]
        mixes.append(_dot(mix_in.astype(BF16), wo_ref[...]))
    h1s = []
    for r, mix in zip(halves, mixes):
        h = _layer_norm(x_ref[r, :], lng_ref[...], lnb_ref[...])
        h1s.append(_layer_norm(ALPHA * h + mix, l1g_ref[...], l1b_ref[...]))
    h1bs = [h1.astype(BF16) for h1 in h1s]
    ffs = [None, None]
    chunk = D_MODEL
    for c in range(D_FF // chunk):
        acts = [jnp.maximum(_dot(h1b, wup_ref[:, c * chunk:(c + 1) * chunk]), 0.0) for h1b in h1bs]
        for i, a in enumerate(acts):
            part = _dot((a * a).astype(BF16), wdn_ref[c * chunk:(c + 1) * chunk, :])
            ffs[i] = part if ffs[i] is None else ffs[i] + part
    for r, h1, ff in zip(halves, h1s, ffs):
        y_ref[r, :] = _layer_norm(ALPHA * h1 + ff, l2g_ref[...], l2b_ref[...])


def _full(shape):
    n = len(shape)
    return pl.BlockSpec(shape, lambda *_: (0,) * n)


def _rope_cos_sin(pos, transposed=False):
    inv_freq = ROPE_THETA ** (-2.0 * jnp.arange(HALF, dtype=F32) / QK_ROPE)
    ang = pos.astype(F32)[:, None] * inv_freq[None, :]
    cos, sin = jnp.cos(ang), jnp.sin(ang)
    return (cos.T, sin.T) if transposed else (cos, sin)


def _rope_tables(pos, scale):
    cos, sin = _rope_cos_sin(pos)
    n = pos.shape[0]
    ones = jnp.ones((n, ROPE_LO), F32)
    z_lo = jnp.zeros((n, ROPE_LO), F32)
    z_hi = jnp.zeros((n, HEAD_PAD - ROPE_LO - QK_ROPE), F32)
    z_half = jnp.zeros((n, HALF), F32)
    c = jnp.concatenate([ones, cos, cos, z_hi + 1.0], axis=1)
    a = jnp.concatenate([z_lo, -sin, z_half, z_hi], axis=1)
    b = jnp.concatenate([z_lo, z_half, sin, z_hi], axis=1)
    return c * scale, a * scale, b * scale


def _pack_weights(w_in, w_uq, w_uk, w_uv):
    offs = [0]
    for s in (Q_LORA, KV_LORA, QK_ROPE, CONV_DIM, CONV_DIM, CONV_DIM, D_MODEL, D_MODEL):
        offs.append(offs[-1] + s)
    pieces = [w_in[:, offs[i]:offs[i + 1]] for i in range(8)]
    kpe_pad = jnp.pad(pieces[2], ((0, 0), (ROPE_LO, HEAD_PAD - ROPE_LO - QK_ROPE)))
    w_pack = jnp.concatenate([pieces[0], kpe_pad, pieces[1]] + pieces[3:], axis=1).astype(BF16)
    head_pad = HEAD_PAD - QK_NOPE - QK_ROPE
    w_uq_p = jnp.pad(w_uq.reshape(Q_LORA, HEADS, QK_NOPE + QK_ROPE), ((0, 0), (0, 0), (0, head_pad)))
    w_uq_p = w_uq_p.reshape(Q_LORA, HEADS * HEAD_PAD).astype(BF16)
    w_uk_p = jnp.pad(w_uk, ((0, 0), (0, 0), (0, HEAD_PAD - QK_NOPE)))
    w_uk_p = w_uk_p.reshape(KV_LORA, HEADS * HEAD_PAD).astype(BF16)
    w_uk_t = jnp.pad(jnp.transpose(w_uk, (1, 2, 0)), ((0, 0), (0, HEAD_PAD - QK_NOPE), (0, 0))).astype(BF16)
    w_uv_t = w_uv.reshape(KV_LORA, HEADS * V_HEAD).T.astype(BF16)
    w_uv_even = jnp.pad(w_uv, ((0, 0), (0, 0), (0, V_HEAD)))
    w_uv_odd = jnp.pad(w_uv, ((0, 0), (0, 0), (V_HEAD, 0)))
    is_even = (jnp.arange(HEADS) % 2 == 0)[None, :, None]
    w_uv_p = jnp.transpose(jnp.where(is_even, w_uv_even, w_uv_odd), (1, 0, 2)).astype(BF16)
    return w_pack, w_uq_p, w_uk_p, w_uk_t, w_uv_t, w_uv_p


def _proj_seq(x, pos, init, shared, tile, ckv_base):
    nb, n, _ = x.shape
    row_offset = ckv_base.shape[1] - n
    nt = n // tile
    q_scale = SM_SCALE * LOG2_E
    q_tabs = tuple(t * q_scale for t in _rope_cos_sin(pos, transposed=True))
    k_tabs = _rope_tables(pos, 1.0)
    row_spec = lambda width: pl.BlockSpec((None, tile, width), lambda t, b: (b, t, 0))
    tab_spec = pl.BlockSpec((tile, HEAD_PAD), lambda t, b: (t, 0))
    tab_t_spec = pl.BlockSpec((HALF, tile), lambda t, b: (0, t))
    ckv_spec = pl.BlockSpec((pl.Element(1), pl.Element(tile), pl.Element(KV_LORA)),
                            lambda t, b: (b, pl.multiple_of(row_offset + t * tile, 8), 0))
    in_specs = ([row_spec(D_MODEL)] + [_full(a.shape) for a in shared] + [tab_t_spec] * 2 + [tab_spec] * 3
                + [_full(init.shape), pl.BlockSpec(memory_space=pl.ANY)])
    ins = (x, *shared, *q_tabs, *k_tabs, init, ckv_base)
    out_shape = (
        jax.ShapeDtypeStruct((nb, HEADS * HEAD_PAD, n), BF16),
        jax.ShapeDtypeStruct((nb, n, HEADS * HEAD_PAD), BF16),
        jax.ShapeDtypeStruct((nb, HEADS * V_HEAD, n), BF16),
        jax.ShapeDtypeStruct(ckv_base.shape, F32),
        jax.ShapeDtypeStruct((nb, QK_ROPE, n), F32),
        jax.ShapeDtypeStruct((nb, n, D_MODEL), BF16),
        jax.ShapeDtypeStruct((nb, n, D_MODEL), BF16),
        jax.ShapeDtypeStruct((nb, CONV_K - 1, CONV_DIM), F32),
    )
    col_spec = lambda height: pl.BlockSpec((None, height, tile), lambda t, b: (b, 0, t))
    out_specs = (col_spec(HEADS * HEAD_PAD), row_spec(HEADS * HEAD_PAD), col_spec(HEADS * V_HEAD),
                 ckv_spec, col_spec(QK_ROPE), row_spec(D_MODEL), row_spec(D_MODEL),
                 _full((nb, CONV_K - 1, CONV_DIM)))
    return pl.pallas_call(
        functools.partial(_proj_seq_kernel, q_scale=q_scale),
        grid=(nt, nb),
        in_specs=in_specs,
        out_specs=out_specs,
        out_shape=out_shape,
        scratch_shapes=[pltpu.VMEM((tile + 8, CONV_DIM), F32)],
        input_output_aliases={len(ins) - 1: 3},
        compiler_params=pltpu.CompilerParams(
            dimension_semantics=("arbitrary", "arbitrary"), vmem_limit_bytes=VMEM_LIMIT),
        name="proj_seq",
    )(*ins)


def _proj_sample(x, tabs, s0, s1, shared, tile):
    n = x.shape[0]
    n_seq = n // 8
    row = lambda width: pl.BlockSpec((tile, width), lambda i: (i, 0))
    seq4 = lambda width: pl.BlockSpec((tile // 8, HEADS, 8, width), lambda i: (i, 0, 0, 0))
    ins = (x, *shared, *tabs, s0, s1)
    in_specs = ([row(D_MODEL)] + [_full(a.shape) for a in shared] + [row(HEAD_PAD)] * 6
                + [row(CONV_DIM)] * 2)
    out_specs = (seq4(KV_LORA), seq4(QK_ROPE), row(KV_LORA), row(HEAD_PAD), row(D_MODEL), row(D_MODEL),
                 row(CONV_DIM))
    out_shape = (
        jax.ShapeDtypeStruct((n_seq, HEADS, 8, KV_LORA), F32),
        jax.ShapeDtypeStruct((n_seq, HEADS, 8, QK_ROPE), F32),
        jax.ShapeDtypeStruct((n, KV_LORA), F32),
        jax.ShapeDtypeStruct((n, HEAD_PAD), F32),
        jax.ShapeDtypeStruct((n, D_MODEL), BF16),
        jax.ShapeDtypeStruct((n, D_MODEL), BF16),
        jax.ShapeDtypeStruct((n, CONV_DIM), F32),
    )
    return pl.pallas_call(
        _proj_sample_kernel,
        grid=(n // tile,),
        in_specs=in_specs,
        out_specs=out_specs,
        out_shape=out_shape,
        scratch_shapes=[pltpu.VMEM((tile + 8, CONV_DIM), F32)],
        compiler_params=pltpu.CompilerParams(
            dimension_semantics=("arbitrary",), vmem_limit_bytes=VMEM_LIMIT),
        name="proj_sample",
    )(*ins)


def _attn_prompt(q_t, k, v_t, k_meta, v_meta_t, tq, tiles_per_step):
    nb, n, _ = k.shape
    step_rows = tq * tiles_per_step
    return pl.pallas_call(
        functools.partial(_attn_prompt_kernel, tq=tq),
        grid=(nb, n // step_rows),
        in_specs=[
            pl.BlockSpec((None, HEADS * HEAD_PAD, step_rows), lambda b, i: (b, 0, i)),
            pl.BlockSpec((None, n, HEADS * HEAD_PAD), lambda b, i: (b, 0, 0)),
            pl.BlockSpec((None, HEADS * V_HEAD, n), lambda b, i: (b, 0, 0)),
            _full(k_meta.shape),
            _full(v_meta_t.shape),
        ],
        out_specs=pl.BlockSpec((None, step_rows, HEADS * V_HEAD), lambda b, i: (b, i, 0)),
        out_shape=jax.ShapeDtypeStruct((nb, n, HEADS * V_HEAD), BF16),
        scratch_shapes=[pltpu.VMEM((HEADS, 1, tq), F32), pltpu.VMEM((HEADS, V_HEAD + 16, tq), F32)],
        compiler_params=pltpu.CompilerParams(
            dimension_semantics=("arbitrary", "arbitrary"), vmem_limit_bytes=VMEM_LIMIT),
        name="attn_prompt",
    )(q_t, k, v_t, k_meta, v_meta_t)


def _attn_sample(page_table, q_lat, q_pe, ckv_new, kpe_new, ckv_pool, kpe_pool_t):
    n_seq, n_pages = page_table.shape
    seq4 = lambda width: pl.BlockSpec((1, HEADS, 8, width), lambda s, pt: (s, 0, 0, 0))
    seq3 = lambda width: pl.BlockSpec((1, 8, width), lambda s, pt: (s, 0, 0))
    hbm = pl.BlockSpec(memory_space=pl.ANY)
    grid_spec = pltpu.PrefetchScalarGridSpec(
        num_scalar_prefetch=1,
        grid=(n_seq,),
        in_specs=[seq4(KV_LORA), seq4(QK_ROPE), seq3(KV_LORA), seq3(QK_ROPE), hbm, hbm],
        out_specs=pl.BlockSpec((HEADS, 8, KV_LORA), lambda s, pt: (0, s, 0)),
        scratch_shapes=[pltpu.VMEM((2, n_pages * PAGE, KV_LORA), F32),
                        pltpu.VMEM((2, QK_ROPE, n_pages * PAGE), F32),
                        pltpu.SemaphoreType.DMA((2,)), pltpu.SemaphoreType.DMA((2,))],
    )
    return pl.pallas_call(
        functools.partial(_attn_sample_kernel, n_pages=n_pages),
        grid_spec=grid_spec,
        out_shape=jax.ShapeDtypeStruct((HEADS, n_seq * 8, KV_LORA), F32),
        compiler_params=pltpu.CompilerParams(
            dimension_semantics=("arbitrary",), vmem_limit_bytes=VMEM_LIMIT),
        name="attn_sample",
    )(page_table.reshape(-1), q_lat, q_pe, ckv_new, kpe_new, ckv_pool, kpe_pool_t)


def _post(front, x, gcy, gm, weights, tile, sample):
    n = x.shape[0]
    row = lambda width: pl.BlockSpec((tile, width), lambda i: (i, 0))
    if sample:
        o_lat, w_uv_p = front
        front_specs = [pl.BlockSpec((HEADS, tile, KV_LORA), lambda i: (0, i, 0)), _full(w_uv_p.shape)]
    else:
        front_specs = [row(HEADS * V_HEAD)]
    in_specs = front_specs + [row(D_MODEL), row(D_MODEL), row(D_MODEL)] + [_full(w.shape) for w in weights]
    return pl.pallas_call(
        functools.partial(_post_kernel, sample=sample),
        grid=(n // tile,),
        in_specs=in_specs,
        out_specs=row(D_MODEL),
        out_shape=jax.ShapeDtypeStruct((n, D_MODEL), F32),
        compiler_params=pltpu.CompilerParams(
            dimension_semantics=("arbitrary",), vmem_limit_bytes=VMEM_LIMIT),
        name="post_sample" if sample else "post_prompt",
    )(*front, x, gcy, gm, *weights)


def kernel(x_prompt, x_sample, cache_ckv, cache_kpe, state_conv, page_table, meta_tokens, ln_emb_g, ln_emb_b, w_in, q_norm_g, w_uq, kv_norm_g, w_uk, w_uv, w_o_mla, conv_w, w_conv_out, w_o, ln1_g, ln1_b, w_up, w_down, ln2_g, ln2_b):
    assert w_in.shape[0] == DEPTH
    nb, seq, _ = x_prompt.shape
    n_seq, dec_seq, _ = x_sample.shape
    assert dec_seq == 8
    past_len = page_table.shape[1] * PAGE

    row2 = lambda a: a.reshape(1, -1).astype(F32)
    w_pack, w_uq_p, w_uk_p, w_uk_t, w_uv_t, w_uv_p = _pack_weights(w_in[0], w_uq[0], w_uk[0], w_uv[0])
    w_co = w_conv_out[0].astype(BF16)
    lng, lnb = row2(ln_emb_g), row2(ln_emb_b)
    qng, kvg = row2(q_norm_g[0]), row2(kv_norm_g[0])
    shared_seq = (lng, lnb, w_pack, qng, w_uq_p.T, kvg, w_uk_p, w_uv_t, conv_w[0], w_co)
    shared_sample = (lng, lnb, w_pack, qng, w_uq_p, kvg, w_uk_t, conv_w[0], w_co)
    post_w = (lng, lnb, w_o_mla[0].astype(BF16), w_o[0].astype(BF16), row2(ln1_g[0]), row2(ln1_b[0]),
              w_up[0].astype(BF16), w_down[0].astype(BF16), row2(ln2_g[0]), row2(ln2_b[0]))

    zero_state = jnp.zeros((CONV_K - 1, CONV_DIM), F32)
    _, k_meta, vt_meta, ckv_meta, kpet_meta, _, _, conv_meta = _proj_seq(
        meta_tokens[None], jnp.arange(N_META), zero_state, shared_seq, N_META,
        jnp.zeros((1, N_META, KV_LORA), F32))

    q_t, k, v_t, ckv_p, kpet_p, gcy_p, gm_p, conv_p = _proj_seq(
        x_prompt, N_META + jnp.arange(seq), conv_meta[0], shared_seq, DENSE_TILE,
        jnp.pad(jnp.broadcast_to(ckv_meta, (nb, N_META, KV_LORA)), ((0, 0), (0, seq), (0, 0))))
    o_p = _attn_prompt(q_t, k, v_t, k_meta[0], vt_meta[0], QUERY_TILE, QUERY_TILES_PER_STEP)
    n_p = nb * seq
    y_prompt = _post((o_p.reshape(n_p, -1),), x_prompt.reshape(n_p, D_MODEL), gcy_p.reshape(n_p, D_MODEL),
                     gm_p.reshape(n_p, D_MODEL), post_w, DENSE_TILE, sample=False).reshape(nb, seq, D_MODEL)

    n_s = n_seq * dec_seq
    pos_s = past_len + jnp.arange(dec_seq)
    tabs_s = tuple(jnp.tile(t, (n_seq, 1)) for t in _rope_tables(pos_s, SM_SCALE) + _rope_tables(pos_s, 1.0))
    s0 = jnp.repeat(state_conv[0, :, 0], dec_seq, axis=0)
    s1 = jnp.repeat(state_conv[0, :, 1], dec_seq, axis=0)
    xs = x_sample.reshape(n_s, D_MODEL)
    q_lat, q_pe, ckv_s, kpe_s128, gcy_s, gm_s, u_s = _proj_sample(xs, tabs_s, s0, s1, shared_sample, SAMPLE_PROJ_TILE)
    kpe_s = kpe_s128[:, ROPE_LO:ROPE_LO + QK_ROPE]
    o_lat = _attn_sample(page_table, q_lat, q_pe, ckv_s.reshape(n_seq, dec_seq, KV_LORA),
                         kpe_s.reshape(n_seq, dec_seq, QK_ROPE), cache_ckv[0],
                         jnp.swapaxes(cache_kpe[0], 1, 2))
    y_sample = _post((o_lat, w_uv_p), xs, gcy_s, gm_s, post_w, DENSE_TILE, sample=True).reshape(
        n_seq, dec_seq, D_MODEL)

    new_kpe_prompt = jnp.swapaxes(
        jnp.concatenate([jnp.broadcast_to(kpet_meta, (nb, QK_ROPE, N_META)), kpet_p], axis=2), 1, 2)[None]
    new_conv_sample = u_s.reshape(n_seq, dec_seq, CONV_DIM)[:, dec_seq - (CONV_K - 1):][None]
    return (y_prompt, y_sample, ckv_p[None], new_kpe_prompt, conv_p[None],
            ckv_s.reshape(n_seq, dec_seq, KV_LORA)[None], kpe_s.reshape(n_seq, dec_seq, QK_ROPE)[None],
            new_conv_sample)
```

```python
import functools

import jax
import jax.numpy as jnp
from jax import lax
from jax.experimental import pallas as pl
from jax.experimental.pallas import tpu as pltpu

D_MODEL = 1024
N_META = 16
HEADS = 8
Q_LORA = 384
KV_LORA = 256
QK_NOPE = 64
QK_ROPE = 32
V_HEAD = 64
CONV_DIM = 512
CONV_K = 3
D_FF = 4096
PAGE = 128
ROPE_THETA = 10000.0
LN_EPS = 1e-5
RMS_EPS = 1e-6
DEPTH = 1
ALPHA = (2.0 * DEPTH) ** 0.25
SM_SCALE = (QK_NOPE + QK_ROPE) ** -0.5
LOG2_E = 1.4426950408889634

SAMPLE_CHUNKS = 4
SEQS_PER_STEP = 2
HEAD_PAD = 128
ROPE_LO = QK_NOPE
HALF = QK_ROPE // 2

C_CQ = 0
C_KPE = C_CQ + Q_LORA
C_CKV = C_KPE + HEAD_PAD
C_GB = C_CKV + KV_LORA
C_GC = C_GB + CONV_DIM
C_CH = C_GC + CONV_DIM
C_GCV = C_CH + CONV_DIM
C_GML = C_GCV + D_MODEL
C_END = C_GML + D_MODEL

VMEM_LIMIT = 56 * 1024 * 1024

DENSE_TILE = 512
SAMPLE_PROJ_TILE = 256
QUERY_TILE = 256
QUERY_TILES_PER_STEP = 8

F32 = jnp.float32
BF16 = jnp.bfloat16


def _dot(a, b):
    return jnp.dot(a, b, preferred_element_type=F32)


def _dot_nt(a, b):
    return lax.dot_general(a, b, (((1,), (1,)), ((), ())), preferred_element_type=F32)


def _layer_norm(x, g, b):
    mu = jnp.mean(x, axis=-1, keepdims=True)
    xc = x - mu
    var = jnp.mean(xc * xc, axis=-1, keepdims=True)
    return xc * lax.rsqrt(var + LN_EPS) * g + b


def _rms_norm(x, g):
    return x * lax.rsqrt(jnp.mean(x * x, axis=-1, keepdims=True) + RMS_EPS) * g


def _rope_group(x, c, a, b):
    return x * c + pltpu.roll(x, HEAD_PAD - HALF, 1) * a + pltpu.roll(x, HALF, 1) * b


def _proj_seq_kernel(x_ref, lng_ref, lnb_ref, w_ref, qng_ref, wuq_ref, kvg_ref, wuk_ref, wuvt_ref,
                     cw_ref, wco_ref, qcos_ref, qsin_ref, kc_ref, ka_ref, kb_ref, init_ref, ckv_base_ref,
                     qt_ref, k_ref, vt_ref, ckv_ref, kpet_ref, gcy_ref, gm_ref, cst_ref, ubuf, *, q_scale):
    del ckv_base_ref
    t = pl.program_id(0)
    b = pl.program_id(1)
    rows = x_ref.shape[0]

    @pl.when(t == 0)
    def _():
        ubuf[6:8, :] = init_ref[...]

    @pl.when(t > 0)
    def _():
        ubuf[6:8, :] = cst_ref[b]

    hb = _layer_norm(x_ref[...], lng_ref[...], lnb_ref[...]).astype(BF16)
    z_cq_kpe = _dot(hb, w_ref[:, C_CQ:C_CKV])
    z_cq = z_cq_kpe[:, C_CQ:C_KPE]
    z_kpe = z_cq_kpe[:, C_KPE:C_CKV]
    z_ckv = _dot(hb, w_ref[:, C_CKV:C_GB])
    gate_b = _dot(hb, w_ref[:, C_GB:C_GC])
    gate_c = _dot(hb, w_ref[:, C_GC:C_CH])
    conv_h = _dot(hb, w_ref[:, C_CH:C_GCV])
    z_gcv = _dot(hb, w_ref[:, C_GCV:C_GML])
    z_gml = _dot(hb, w_ref[:, C_GML:C_END])
    q_t = _dot_nt(wuq_ref[...], _rms_norm(z_cq, qng_ref[...]).astype(BF16))
    ckv = _rms_norm(z_ckv, kvg_ref[...])
    ckvb = ckv.astype(BF16)
    kn = _dot(ckvb, wuk_ref[...])
    v_t = _dot_nt(wuvt_ref[...], ckvb)
    u = gate_c * conv_h
    ubuf[8:8 + rows, :] = u
    cst_ref[b] = u[rows - 2:rows, :]
    conv_out = (cw_ref[0:1, :] * ubuf[6:6 + rows, :] + cw_ref[1:2, :] * ubuf[7:7 + rows, :]
                + cw_ref[2:3, :] * u)
    y_conv = _dot((gate_b * conv_out).astype(BF16), wco_ref[...])

    kpe = _rope_group(z_kpe, kc_ref[...], ka_ref[...], kb_ref[...])
    q_cos, q_sin = qcos_ref[...], qsin_ref[...]
    for g in range(HEADS):
        group = slice(g * HEAD_PAD, (g + 1) * HEAD_PAD)
        k_ref[:, group] = (kn[:, group] + kpe).astype(BF16)
        lo = g * HEAD_PAD + ROPE_LO
        x1, x2 = q_t[lo:lo + HALF], q_t[lo + HALF:lo + 2 * HALF]
        qt_ref[g * HEAD_PAD:lo, :] = (q_t[g * HEAD_PAD:lo] * q_scale).astype(BF16)
        qt_ref[lo:lo + HALF, :] = (x1 * q_cos - x2 * q_sin).astype(BF16)
        qt_ref[lo + HALF:lo + 2 * HALF, :] = (x2 * q_cos + x1 * q_sin).astype(BF16)
        qt_ref[lo + 2 * HALF:(g + 1) * HEAD_PAD, :] = jnp.zeros((HEAD_PAD - ROPE_LO - QK_ROPE, rows), BF16)
    ckv_ref[...] = ckv.reshape(ckv_ref.shape)
    kpet_ref[...] = kpe.T[ROPE_LO:ROPE_LO + QK_ROPE, :]
    vt_ref[...] = v_t.astype(BF16)
    gcy_ref[...] = (jax.nn.sigmoid(z_gcv) * y_conv).astype(BF16)
    gm_ref[...] = jax.nn.sigmoid(z_gml).astype(BF16)


def _proj_sample_kernel(x_ref, lng_ref, lnb_ref, w_ref, qng_ref, wuq_ref, kvg_ref, wukt_ref,
                        cw_ref, wco_ref, qc_ref, qa_ref, qb_ref, kc_ref, ka_ref, kb_ref,
                        s0_ref, s1_ref,
                        qlat_ref, qpe_ref, ckv_ref, kpe_ref, gcy_ref, gm_ref, u_ref, ubuf):
    rows = x_ref.shape[0]
    n_seq = rows // 8
    hb = _layer_norm(x_ref[...], lng_ref[...], lnb_ref[...]).astype(BF16)
    z_cq_kpe = _dot(hb, w_ref[:, C_CQ:C_CKV])
    z_ckv = _dot(hb, w_ref[:, C_CKV:C_GB])
    gate_b = _dot(hb, w_ref[:, C_GB:C_GC])
    gate_c = _dot(hb, w_ref[:, C_GC:C_CH])
    conv_h = _dot(hb, w_ref[:, C_CH:C_GCV])
    z_gcv = _dot(hb, w_ref[:, C_GCV:C_GML])
    z_gml = _dot(hb, w_ref[:, C_GML:C_END])
    q = _dot(_rms_norm(z_cq_kpe[:, C_CQ:C_KPE], qng_ref[...]).astype(BF16), wuq_ref[...])
    u = gate_c * conv_h
    u_ref[...] = u
    ubuf[8:8 + rows, :] = u
    ubuf[6:8, :] = jnp.zeros((2, CONV_DIM), F32)
    tok = lax.broadcasted_iota(jnp.int32, (rows, 1), 0) % 8
    um1 = jnp.where(tok == 0, s1_ref[...], ubuf[7:7 + rows, :])
    um2 = jnp.where(tok == 0, s0_ref[...], jnp.where(tok == 1, s1_ref[...], ubuf[6:6 + rows, :]))
    conv_out = cw_ref[0:1, :] * um2 + cw_ref[1:2, :] * um1 + cw_ref[2:3, :] * u
    y_conv = _dot((gate_b * conv_out).astype(BF16), wco_ref[...])

    for g in range(HEADS):
        qg = _rope_group(q[:, g * HEAD_PAD:(g + 1) * HEAD_PAD], qc_ref[...], qa_ref[...], qb_ref[...])
        qlat_ref[:, g] = _dot(qg.astype(BF16), wukt_ref[g]).reshape(n_seq, 8, KV_LORA)
        qpe_ref[:, g] = qg[:, ROPE_LO:ROPE_LO + QK_ROPE].reshape(n_seq, 8, QK_ROPE)
    ckv_ref[...] = _rms_norm(z_ckv, kvg_ref[...])
    kpe_ref[...] = _rope_group(z_cq_kpe[:, C_KPE:C_CKV], kc_ref[...], ka_ref[...], kb_ref[...])
    gcy_ref[...] = (jax.nn.sigmoid(z_gcv) * y_conv).astype(BF16)
    gm_ref[...] = jax.nn.sigmoid(z_gml).astype(BF16)


def _attn_prompt_kernel(qt_ref, k_ref, vt_ref, km_ref, vmt_ref, o_ref, m_ref, acc_ref, *, tq):
    tiles = qt_ref.shape[1] // tq
    assert tiles * tq == k_ref.shape[0]
    for qi in range(tiles):
        rows = slice(qi * tq, (qi + 1) * tq)
        _attn_prompt_tile(qi, qt_ref.at[:, rows], k_ref, vt_ref, km_ref, vmt_ref, o_ref.at[rows], m_ref, acc_ref)


def _attn_prompt_tile(qi, qt_ref, k_ref, vt_ref, km_ref, vmt_ref, o_ref, m_ref, acc_ref):
    tq = qt_ref.shape[1]
    hsl = lambda h: slice(h * HEAD_PAD, (h + 1) * HEAD_PAD)
    vsl = lambda h: slice(h * V_HEAD, (h + 1) * V_HEAD)

    def values_and_ones(v_t):
        return jnp.concatenate([v_t, jnp.ones((16, v_t.shape[1]), BF16)], axis=0)


    diag = pl.ds(qi * tq, tq)
    key_idx = lax.broadcasted_iota(jnp.int32, (tq, tq), 0)
    query_idx = lax.broadcasted_iota(jnp.int32, (tq, tq), 1)
    causal = key_idx <= query_idx
    scores = [_dot(jnp.concatenate([k_ref[diag, hsl(h)], km_ref[:, hsl(h)]], axis=0), qt_ref[hsl(h), :])
              for h in range(HEADS)]
    probs = []
    for h in range(HEADS):
        s_diag = jnp.where(causal, scores[h][0:tq], -jnp.inf)
        s_meta = scores[h][tq:tq + N_META]
        m = jnp.maximum(jnp.max(s_diag, axis=0, keepdims=True), jnp.max(s_meta, axis=0, keepdims=True))
        m_ref[h] = m
        probs.append((jnp.exp2(s_diag - m).astype(BF16), jnp.exp2(s_meta - m).astype(BF16)))
    for h in range(HEADS):
        acc_ref[h] = (_dot(values_and_ones(vt_ref[vsl(h), diag]), probs[h][0])
                      + _dot(values_and_ones(vmt_ref[vsl(h), :]), probs[h][1]))

    def visible_keys(k0, n_keys):
        keys = pl.ds(k0, n_keys)
        scores = [_dot(k_ref[keys, hsl(h)], qt_ref[hsl(h), :]) for h in range(HEADS)]
        alphas, probs = [], []
        for h in range(HEADS):
            m_prev = m_ref[h]
            m_new = jnp.maximum(m_prev, jnp.max(scores[h], axis=0, keepdims=True))
            m_ref[h] = m_new
            alphas.append(jnp.exp2(m_prev - m_new))
            probs.append(jnp.exp2(scores[h] - m_new).astype(BF16))
        for h in range(HEADS):
            acc_ref[h] = alphas[h] * acc_ref[h] + _dot(values_and_ones(vt_ref[vsl(h), keys]), probs[h])

    for i in range(qi // 2):
        visible_keys(i * (2 * tq), 2 * tq)
    if qi % 2 == 1:
        visible_keys((qi - 1) * tq, tq)

    o_t = jnp.concatenate(
        [acc_ref[h, 0:V_HEAD] / acc_ref[h, V_HEAD:V_HEAD + 1] for h in range(HEADS)], axis=0)
    o_ref[...] = o_t.T.astype(BF16)


def _attn_sample_kernel(pt_ref, qlat_ref, qpe_ref, ckvn_ref, kpen_ref, ckv_hbm, kpet_hbm, o_ref,
                        cbuf, kbuf, csem, ksem, *, n_pages):
    step = pl.program_id(0)
    slot = step % 2

    def page_copies(step_, slot_, j, p):
        pg = pt_ref[(step_ * SEQS_PER_STEP + j) * n_pages + p]
        tokens = pl.ds(p * PAGE, PAGE)
        return (pltpu.make_async_copy(ckv_hbm.at[pg], cbuf.at[slot_, j, tokens], csem.at[slot_]),
                pltpu.make_async_copy(kpet_hbm.at[pg], kbuf.at[slot_, j, :, tokens], ksem.at[slot_]))

    def for_each_page(step_, slot_, fn):
        for j in range(SEQS_PER_STEP):
            for p in range(n_pages):
                for copy in page_copies(step_, slot_, j, p):
                    fn(copy)

    @pl.when(step == 0)
    def _():
        for_each_page(0, 0, lambda copy: copy.start())

    @pl.when(step + 1 < pl.num_programs(0))
    def _():
        for_each_page(step + 1, 1 - slot, lambda copy: copy.start())

    for_each_page(step, slot, lambda copy: copy.wait())

    n_rows = HEADS * 8
    chunk_keys = (n_pages // SAMPLE_CHUNKS) * PAGE
    tok = lax.broadcasted_iota(jnp.int32, (n_rows, 8), 0) % 8
    key = lax.broadcasted_iota(jnp.int32, (n_rows, 8), 1)

    qls, qps, states = [], [], []
    for j in range(SEQS_PER_STEP):
        ql = qlat_ref[j].reshape(n_rows, KV_LORA).astype(BF16)
        qp = qpe_ref[j].reshape(n_rows, QK_ROPE).astype(BF16)
        cn = ckvn_ref[j].astype(BF16)
        s_new = _dot_nt(ql, cn) + _dot_nt(qp, kpen_ref[j].astype(BF16))
        s_new = jnp.where(key <= tok, s_new, -jnp.inf)
        m = jnp.max(s_new, axis=-1, keepdims=True)
        p_new = jnp.exp(s_new - m)
        states.append((m, jnp.sum(p_new, axis=-1, keepdims=True), _dot(p_new.astype(BF16), cn)))
        qls.append(jnp.concatenate([ql, jnp.zeros((HEAD_PAD - n_rows, KV_LORA), BF16)], axis=0))
        qps.append(qp)

    def chunk_scores(j, i):
        keys = slice(i * chunk_keys, (i + 1) * chunk_keys)
        c = cbuf[slot, j, keys, :].astype(BF16)
        s = _dot_nt(c, qls[j]).T[0:n_rows] + _dot(qps[j], kbuf[slot, j, :, keys].astype(BF16))
        return c, s

    pending = [chunk_scores(j, 0) for j in range(SEQS_PER_STEP)]
    for i in range(SAMPLE_CHUNKS):
        current = pending
        if i + 1 < SAMPLE_CHUNKS:
            pending = [chunk_scores(j, i + 1) for j in range(SEQS_PER_STEP)]
        for j in range(SEQS_PER_STEP):
            c, s = current[j]
            m, denom, acc = states[j]
            m_new = jnp.maximum(m, jnp.max(s, axis=-1, keepdims=True))
            alpha = jnp.exp(m - m_new)
            p = jnp.exp(s - m_new)
            states[j] = (m_new, alpha * denom + jnp.sum(p, axis=-1, keepdims=True),
                         alpha * acc + _dot(p.astype(BF16), c))
    for j in range(SEQS_PER_STEP):
        _, denom, acc = states[j]
        o_ref[:, j * 8:(j + 1) * 8, :] = (acc / denom).reshape(HEADS, 8, KV_LORA)


def _post_kernel(*refs, sample):
    if sample:
        (olat_ref, wuvp_ref, x_ref, gcy_ref, gm_ref, lng_ref, lnb_ref, womla_ref, wo_ref,
         l1g_ref, l1b_ref, wup_ref, wdn_ref, l2g_ref, l2b_ref, y_ref) = refs
        pairs = []
        for pair in range(HEADS // 2):
            pairs.append(
                _dot(olat_ref[2 * pair].astype(BF16), wuvp_ref[2 * pair])
                + _dot(olat_ref[2 * pair + 1].astype(BF16), wuvp_ref[2 * pair + 1]))
        y_mla = sum(
            _dot(pairs[pair].astype(BF16), womla_ref[pair * HEAD_PAD:(pair + 1) * HEAD_PAD, :])
            for pair in range(HEADS // 2))
    else:
        (o_ref, x_ref, gcy_ref, gm_ref, lng_ref, lnb_ref, womla_ref, wo_ref,
         l1g_ref, l1b_ref, wup_ref, wdn_ref, l2g_ref, l2b_ref, y_ref) = refs
        y_mla = _dot(o_ref[...], womla_ref[...])
    rows = x_ref.shape[0]
    halves = [slice(0, rows // 2), slice(rows // 2, rows)]
    mixes = []
    for r in halves:
        mix_in = gcy_ref[r, :].astype(F32) + gm_ref[r, :].astype(F32) * y_mla[r, :]
        mixes.append(_dot(mix_in.astype(BF16), wo_ref[...]))
    h1s = []
    for r, mix in zip(halves, mixes):
        h = _layer_norm(x_ref[r, :], lng_ref[...], lnb_ref[...])
        h1s.append(_layer_norm(ALPHA * h + mix, l1g_ref[...], l1b_ref[...]))
    h1bs = [h1.astype(BF16) for h1 in h1s]
    ffs = [None, None]
    chunk = D_MODEL
    for c in range(D_FF // chunk):
        acts = [jnp.maximum(_dot(h1b, wup_ref[:, c * chunk:(c + 1) * chunk]), 0.0) for h1b in h1bs]
        for i, a in enumerate(acts):
            part = _dot((a * a).astype(BF16), wdn_ref[c * chunk:(c + 1) * chunk, :])
            ffs[i] = part if ffs[i] is None else ffs[i] + part
    for r, h1, ff in zip(halves, h1s, ffs):
        y_ref[r, :] = _layer_norm(ALPHA * h1 + ff, l2g_ref[...], l2b_ref[...])


def _full(shape):
    n = len(shape)
    return pl.BlockSpec(shape, lambda *_: (0,) * n)


def _rope_cos_sin(pos, transposed=False):
    inv_freq = ROPE_THETA ** (-2.0 * jnp.arange(HALF, dtype=F32) / QK_ROPE)
    ang = pos.astype(F32)[:, None] * inv_freq[None, :]
    cos, sin = jnp.cos(ang), jnp.sin(ang)
    return (cos.T, sin.T) if transposed else (cos, sin)


def _rope_tables(pos, scale):
    cos, sin = _rope_cos_sin(pos)
    n = pos.shape[0]
    ones = jnp.ones((n, ROPE_LO), F32)
    z_lo = jnp.zeros((n, ROPE_LO), F32)
    z_hi = jnp.zeros((n, HEAD_PAD - ROPE_LO - QK_ROPE), F32)
    z_half = jnp.zeros((n, HALF), F32)
    c = jnp.concatenate([ones, cos, cos, z_hi + 1.0], axis=1)
    a = jnp.concatenate([z_lo, -sin, z_half, z_hi], axis=1)
    b = jnp.concatenate([z_lo, z_half, sin, z_hi], axis=1)
    return c * scale, a * scale, b * scale


def _pack_weights(w_in, w_uq, w_uk, w_uv):
    offs = [0]
    for s in (Q_LORA, KV_LORA, QK_ROPE, CONV_DIM, CONV_DIM, CONV_DIM, D_MODEL, D_MODEL):
        offs.append(offs[-1] + s)
    pieces = [w_in[:, offs[i]:offs[i + 1]] for i in range(8)]
    kpe_pad = jnp.pad(pieces[2], ((0, 0), (ROPE_LO, HEAD_PAD - ROPE_LO - QK_ROPE)))
    w_pack = jnp.concatenate([pieces[0], kpe_pad, pieces[1]] + pieces[3:], axis=1).astype(BF16)
    head_pad = HEAD_PAD - QK_NOPE - QK_ROPE
    w_uq_p = jnp.pad(w_uq.reshape(Q_LORA, HEADS, QK_NOPE + QK_ROPE), ((0, 0), (0, 0), (0, head_pad)))
    w_uq_p = w_uq_p.reshape(Q_LORA, HEADS * HEAD_PAD).astype(BF16)
    w_uk_p = jnp.pad(w_uk, ((0, 0), (0, 0), (0, HEAD_PAD - QK_NOPE)))
    w_uk_p = w_uk_p.reshape(KV_LORA, HEADS * HEAD_PAD).astype(BF16)
    w_uk_t = jnp.pad(jnp.transpose(w_uk, (1, 2, 0)), ((0, 0), (0, HEAD_PAD - QK_NOPE), (0, 0))).astype(BF16)
    w_uv_t = w_uv.reshape(KV_LORA, HEADS * V_HEAD).T.astype(BF16)
    w_uv_even = jnp.pad(w_uv, ((0, 0), (0, 0), (0, V_HEAD)))
    w_uv_odd = jnp.pad(w_uv, ((0, 0), (0, 0), (V_HEAD, 0)))
    is_even = (jnp.arange(HEADS) % 2 == 0)[None, :, None]
    w_uv_p = jnp.transpose(jnp.where(is_even, w_uv_even, w_uv_odd), (1, 0, 2)).astype(BF16)
    return w_pack, w_uq_p, w_uk_p, w_uk_t, w_uv_t, w_uv_p


def _proj_seq(x, pos, init, shared, tile, ckv_base):
    nb, n, _ = x.shape
    row_offset = ckv_base.shape[1] - n
    nt = n // tile
    q_scale = SM_SCALE * LOG2_E
    q_tabs = tuple(t * q_scale for t in _rope_cos_sin(pos, transposed=True))
    k_tabs = _rope_tables(pos, 1.0)
    row_spec = lambda width: pl.BlockSpec((None, tile, width), lambda t, b: (b, t, 0))
    tab_spec = pl.BlockSpec((tile, HEAD_PAD), lambda t, b: (t, 0))
    tab_t_spec = pl.BlockSpec((HALF, tile), lambda t, b: (0, t))
    ckv_spec = pl.BlockSpec((pl.Element(1), pl.Element(tile), pl.Element(KV_LORA)),
                            lambda t, b: (b, pl.multiple_of(row_offset + t * tile, 8), 0))
    in_specs = ([row_spec(D_MODEL)] + [_full(a.shape) for a in shared] + [tab_t_spec] * 2 + [tab_spec] * 3
                + [_full(init.shape), pl.BlockSpec(memory_space=pl.ANY)])
    ins = (x, *shared, *q_tabs, *k_tabs, init, ckv_base)
    out_shape = (
        jax.ShapeDtypeStruct((nb, HEADS * HEAD_PAD, n), BF16),
        jax.ShapeDtypeStruct((nb, n, HEADS * HEAD_PAD), BF16),
        jax.ShapeDtypeStruct((nb, HEADS * V_HEAD, n), BF16),
        jax.ShapeDtypeStruct(ckv_base.shape, F32),
        jax.ShapeDtypeStruct((nb, QK_ROPE, n), F32),
        jax.ShapeDtypeStruct((nb, n, D_MODEL), BF16),
        jax.ShapeDtypeStruct((nb, n, D_MODEL), BF16),
        jax.ShapeDtypeStruct((nb, CONV_K - 1, CONV_DIM), F32),
    )
    col_spec = lambda height: pl.BlockSpec((None, height, tile), lambda t, b: (b, 0, t))
    out_specs = (col_spec(HEADS * HEAD_PAD), row_spec(HEADS * HEAD_PAD), col_spec(HEADS * V_HEAD),
                 ckv_spec, col_spec(QK_ROPE), row_spec(D_MODEL), row_spec(D_MODEL),
                 _full((nb, CONV_K - 1, CONV_DIM)))
    return pl.pallas_call(
        functools.partial(_proj_seq_kernel, q_scale=q_scale),
        grid=(nt, nb),
        in_specs=in_specs,
        out_specs=out_specs,
        out_shape=out_shape,
        scratch_shapes=[pltpu.VMEM((tile + 8, CONV_DIM), F32)],
        input_output_aliases={len(ins) - 1: 3},
        compiler_params=pltpu.CompilerParams(
            dimension_semantics=("arbitrary", "arbitrary"), vmem_limit_bytes=VMEM_LIMIT),
        name="proj_seq",
    )(*ins)


def _proj_sample(x, tabs, s0, s1, shared, tile):
    n = x.shape[0]
    n_seq = n // 8
    row = lambda width: pl.BlockSpec((tile, width), lambda i: (i, 0))
    seq4 = lambda width: pl.BlockSpec((tile // 8, HEADS, 8, width), lambda i: (i, 0, 0, 0))
    ins = (x, *shared, *tabs, s0, s1)
    in_specs = ([row(D_MODEL)] + [_full(a.shape) for a in shared] + [row(HEAD_PAD)] * 6
                + [row(CONV_DIM)] * 2)
    out_specs = (seq4(KV_LORA), seq4(QK_ROPE), row(KV_LORA), row(HEAD_PAD), row(D_MODEL), row(D_MODEL),
                 row(CONV_DIM))
    out_shape = (
        jax.ShapeDtypeStruct((n_seq, HEADS, 8, KV_LORA), F32),
        jax.ShapeDtypeStruct((n_seq, HEADS, 8, QK_ROPE), F32),
        jax.ShapeDtypeStruct((n, KV_LORA), F32),
        jax.ShapeDtypeStruct((n, HEAD_PAD), F32),
        jax.ShapeDtypeStruct((n, D_MODEL), BF16),
        jax.ShapeDtypeStruct((n, D_MODEL), BF16),
        jax.ShapeDtypeStruct((n, CONV_DIM), F32),
    )
    return pl.pallas_call(
        _proj_sample_kernel,
        grid=(n // tile,),
        in_specs=in_specs,
        out_specs=out_specs,
        out_shape=out_shape,
        scratch_shapes=[pltpu.VMEM((tile + 8, CONV_DIM), F32)],
        compiler_params=pltpu.CompilerParams(
            dimension_semantics=("arbitrary",), vmem_limit_bytes=VMEM_LIMIT),
        name="proj_sample",
    )(*ins)


def _attn_prompt(q_t, k, v_t, k_meta, v_meta_t, tq, tiles_per_step):
    nb, n, _ = k.shape
    step_rows = tq * tiles_per_step
    return pl.pallas_call(
        functools.partial(_attn_prompt_kernel, tq=tq),
        grid=(nb, n // step_rows),
        in_specs=[
            pl.BlockSpec((None, HEADS * HEAD_PAD, step_rows), lambda b, i: (b, 0, i)),
            pl.BlockSpec((None, n, HEADS * HEAD_PAD), lambda b, i: (b, 0, 0)),
            pl.BlockSpec((None, HEADS * V_HEAD, n), lambda b, i: (b, 0, 0)),
            _full(k_meta.shape),
            _full(v_meta_t.shape),
        ],
        out_specs=pl.BlockSpec((None, step_rows, HEADS * V_HEAD), lambda b, i: (b, i, 0)),
        out_shape=jax.ShapeDtypeStruct((nb, n, HEADS * V_HEAD), BF16),
        scratch_shapes=[pltpu.VMEM((HEADS, 1, tq), F32), pltpu.VMEM((HEADS, V_HEAD + 16, tq), F32)],
        compiler_params=pltpu.CompilerParams(
            dimension_semantics=("arbitrary", "arbitrary"), vmem_limit_bytes=VMEM_LIMIT),
        name="attn_prompt",
    )(q_t, k, v_t, k_meta, v_meta_t)


def _attn_sample(page_table, q_lat, q_pe, ckv_new, kpe_new, ckv_pool, kpe_pool_t):
    n_seq, n_pages = page_table.shape
    seq4 = lambda width: pl.BlockSpec((SEQS_PER_STEP, HEADS, 8, width), lambda s, pt: (s, 0, 0, 0))
    seq3 = lambda width: pl.BlockSpec((SEQS_PER_STEP, 8, width), lambda s, pt: (s, 0, 0))
    hbm = pl.BlockSpec(memory_space=pl.ANY)
    grid_spec = pltpu.PrefetchScalarGridSpec(
        num_scalar_prefetch=1,
        grid=(n_seq // SEQS_PER_STEP,),
        in_specs=[seq4(KV_LORA), seq4(QK_ROPE), seq3(KV_LORA), seq3(QK_ROPE), hbm, hbm],
        out_specs=pl.BlockSpec((HEADS, SEQS_PER_STEP * 8, KV_LORA), lambda s, pt: (0, s, 0)),
        scratch_shapes=[pltpu.VMEM((2, SEQS_PER_STEP, n_pages * PAGE, KV_LORA), F32),
                        pltpu.VMEM((2, SEQS_PER_STEP, QK_ROPE, n_pages * PAGE), F32),
                        pltpu.SemaphoreType.DMA((2,)), pltpu.SemaphoreType.DMA((2,))],
    )
    return pl.pallas_call(
        functools.partial(_attn_sample_kernel, n_pages=n_pages),
        grid_spec=grid_spec,
        out_shape=jax.ShapeDtypeStruct((HEADS, n_seq * 8, KV_LORA), F32),
        compiler_params=pltpu.CompilerParams(
            dimension_semantics=("arbitrary",), vmem_limit_bytes=VMEM_LIMIT),
        name="attn_sample",
    )(page_table.reshape(-1), q_lat, q_pe, ckv_new, kpe_new, ckv_pool, kpe_pool_t)


def _post(front, x, gcy, gm, weights, tile, sample):
    n = x.shape[0]
    row = lambda width: pl.BlockSpec((tile, width), lambda i: (i, 0))
    if sample:
        o_lat, w_uv_p = front
        front_specs = [pl.BlockSpec((HEADS, tile, KV_LORA), lambda i: (0, i, 0)), _full(w_uv_p.shape)]
    else:
        front_specs = [row(HEADS * V_HEAD)]
    in_specs = front_specs + [row(D_MODEL), row(D_MODEL), row(D_MODEL)] + [_full(w.shape) for w in weights]
    return pl.pallas_call(
        functools.partial(_post_kernel, sample=sample),
        grid=(n // tile,),
        in_specs=in_specs,
        out_specs=row(D_MODEL),
        out_shape=jax.ShapeDtypeStruct((n, D_MODEL), F32),
        compiler_params=pltpu.CompilerParams(
            dimension_semantics=("arbitrary",), vmem_limit_bytes=VMEM_LIMIT),
        name="post_sample" if sample else "post_prompt",
    )(*front, x, gcy, gm, *weights)


def kernel(x_prompt, x_sample, cache_ckv, cache_kpe, state_conv, page_table, meta_tokens, ln_emb_g, ln_emb_b, w_in, q_norm_g, w_uq, kv_norm_g, w_uk, w_uv, w_o_mla, conv_w, w_conv_out, w_o, ln1_g, ln1_b, w_up, w_down, ln2_g, ln2_b):
    assert w_in.shape[0] == DEPTH
    nb, seq, _ = x_prompt.shape
    n_seq, dec_seq, _ = x_sample.shape
    assert dec_seq == 8
    past_len = page_table.shape[1] * PAGE

    row2 = lambda a: a.reshape(1, -1).astype(F32)
    w_pack, w_uq_p, w_uk_p, w_uk_t, w_uv_t, w_uv_p = _pack_weights(w_in[0], w_uq[0], w_uk[0], w_uv[0])
    w_co = w_conv_out[0].astype(BF16)
    lng, lnb = row2(ln_emb_g), row2(ln_emb_b)
    qng, kvg = row2(q_norm_g[0]), row2(kv_norm_g[0])
    shared_seq = (lng, lnb, w_pack, qng, w_uq_p.T, kvg, w_uk_p, w_uv_t, conv_w[0], w_co)
    shared_sample = (lng, lnb, w_pack, qng, w_uq_p, kvg, w_uk_t, conv_w[0], w_co)
    post_w = (lng, lnb, w_o_mla[0].astype(BF16), w_o[0].astype(BF16), row2(ln1_g[0]), row2(ln1_b[0]),
              w_up[0].astype(BF16), w_down[0].astype(BF16), row2(ln2_g[0]), row2(ln2_b[0]))

    zero_state = jnp.zeros((CONV_K - 1, CONV_DIM), F32)
    _, k_meta, vt_meta, ckv_meta, kpet_meta, _, _, conv_meta = _proj_seq(
        meta_tokens[None], jnp.arange(N_META), zero_state, shared_seq, N_META,
        jnp.zeros((1, N_META, KV_LORA), F32))

    q_t, k, v_t, ckv_p, kpet_p, gcy_p, gm_p, conv_p = _proj_seq(
        x_prompt, N_META + jnp.arange(seq), conv_meta[0], shared_seq, DENSE_TILE,
        jnp.pad(jnp.broadcast_to(ckv_meta, (nb, N_META, KV_LORA)), ((0, 0), (0, seq), (0, 0))))
    o_p = _attn_prompt(q_t, k, v_t, k_meta[0], vt_meta[0], QUERY_TILE, QUERY_TILES_PER_STEP)
    n_p = nb * seq
    y_prompt = _post((o_p.reshape(n_p, -1),), x_prompt.reshape(n_p, D_MODEL), gcy_p.reshape(n_p, D_MODEL),
                     gm_p.reshape(n_p, D_MODEL), post_w, DENSE_TILE, sample=False).reshape(nb, seq, D_MODEL)

    n_s = n_seq * dec_seq
    pos_s = past_len + jnp.arange(dec_seq)
    tabs_s = tuple(jnp.tile(t, (n_seq, 1)) for t in _rope_tables(pos_s, SM_SCALE) + _rope_tables(pos_s, 1.0))
    s0 = jnp.repeat(state_conv[0, :, 0], dec_seq, axis=0)
    s1 = jnp.repeat(state_conv[0, :, 1], dec_seq, axis=0)
    xs = x_sample.reshape(n_s, D_MODEL)
    q_lat, q_pe, ckv_s, kpe_s128, gcy_s, gm_s, u_s = _proj_sample(xs, tabs_s, s0, s1, shared_sample, SAMPLE_PROJ_TILE)
    kpe_s = kpe_s128[:, ROPE_LO:ROPE_LO + QK_ROPE]
    o_lat = _attn_sample(page_table, q_lat, q_pe, ckv_s.reshape(n_seq, dec_seq, KV_LORA),
                         kpe_s.reshape(n_seq, dec_seq, QK_ROPE), cache_ckv[0],
                         jnp.swapaxes(cache_kpe[0], 1, 2))
    y_sample = _post((o_lat, w_uv_p), xs, gcy_s, gm_s, post_w, DENSE_TILE, sample=True).reshape(
        n_seq, dec_seq, D_MODEL)

    new_kpe_prompt = jnp.swapaxes(
        jnp.concatenate([jnp.broadcast_to(kpet_meta, (nb, QK_ROPE, N_META)), kpet_p], axis=2), 1, 2)[None]
    new_conv_sample = u_s.reshape(n_seq, dec_seq, CONV_DIM)[:, dec_seq - (CONV_K - 1):][None]
    return (y_prompt, y_sample, ckv_p[None], new_kpe_prompt, conv_p[None],
            ckv_s.reshape(n_seq, dec_seq, KV_LORA)[None], kpe_s.reshape(n_seq, dec_seq, QK_ROPE)[None],
            new_conv_sample)
```

```python
import functools

import jax
import jax.numpy as jnp
from jax import lax
from jax.experimental import pallas as pl
from jax.experimental.pallas import tpu as pltpu

D_MODEL = 1024
N_META = 16
HEADS = 8
Q_LORA = 384
KV_LORA = 256
QK_NOPE = 64
QK_ROPE = 32
V_HEAD = 64
CONV_DIM = 512
CONV_K = 3
D_FF = 4096
PAGE = 128
ROPE_THETA = 10000.0
LN_EPS = 1e-5
RMS_EPS = 1e-6
DEPTH = 1
ALPHA = (2.0 * DEPTH) ** 0.25
SM_SCALE = (QK_NOPE + QK_ROPE) ** -0.5
LOG2_E = 1.4426950408889634

SAMPLE_CHUNKS = 4
SEQS_PER_STEP = 2
HEAD_PAD = 128
ROPE_LO = QK_NOPE
HALF = QK_ROPE // 2

C_CQ = 0
C_KPE = C_CQ + Q_LORA
C_CKV = C_KPE + HEAD_PAD
C_GB = C_CKV + KV_LORA
C_GC = C_GB + CONV_DIM
C_CH = C_GC + CONV_DIM
C_GCV = C_CH + CONV_DIM
C_GML = C_GCV + D_MODEL
C_END = C_GML + D_MODEL

VMEM_LIMIT = 56 * 1024 * 1024

DENSE_TILE = 512
SAMPLE_PROJ_TILE = 256
QUERY_TILE = 256
QUERY_TILES_PER_STEP = 8

F32 = jnp.float32
BF16 = jnp.bfloat16


def _dot(a, b):
    return jnp.dot(a, b, preferred_element_type=F32)


def _dot_nt(a, b):
    return lax.dot_general(a, b, (((1,), (1,)), ((), ())), preferred_element_type=F32)


def _layer_norm(x, g, b):
    mu = jnp.mean(x, axis=-1, keepdims=True)
    xc = x - mu
    var = jnp.mean(xc * xc, axis=-1, keepdims=True)
    return xc * lax.rsqrt(var + LN_EPS) * g + b


def _rms_norm(x, g):
    return x * lax.rsqrt(jnp.mean(x * x, axis=-1, keepdims=True) + RMS_EPS) * g


def _rope_group(x, c, a, b):
    return x * c + pltpu.roll(x, HEAD_PAD - HALF, 1) * a + pltpu.roll(x, HALF, 1) * b


def _proj_seq_kernel(x_ref, lng_ref, lnb_ref, w_ref, qng_ref, wuq_ref, kvg_ref, wuk_ref, wuvt_ref,
                     cw_ref, wco_ref, qcos_ref, qsin_ref, kc_ref, ka_ref, kb_ref, init_ref, ckv_base_ref,
                     qt_ref, k_ref, vt_ref, ckv_ref, kpet_ref, gcy_ref, gm_ref, cst_ref, ubuf, *, q_scale):
    del ckv_base_ref
    t = pl.program_id(0)
    b = pl.program_id(1)
    rows = x_ref.shape[0]

    @pl.when(t == 0)
    def _():
        ubuf[6:8, :] = init_ref[...]

    @pl.when(t > 0)
    def _():
        ubuf[6:8, :] = cst_ref[b]

    hb = _layer_norm(x_ref[...], lng_ref[...], lnb_ref[...]).astype(BF16)
    z_cq_kpe = _dot(hb, w_ref[:, C_CQ:C_CKV])
    z_cq = z_cq_kpe[:, C_CQ:C_KPE]
    z_kpe = z_cq_kpe[:, C_KPE:C_CKV]
    z_ckv = _dot(hb, w_ref[:, C_CKV:C_GB])
    gate_b = _dot(hb, w_ref[:, C_GB:C_GC])
    gate_c = _dot(hb, w_ref[:, C_GC:C_CH])
    conv_h = _dot(hb, w_ref[:, C_CH:C_GCV])
    z_gcv = _dot(hb, w_ref[:, C_GCV:C_GML])
    z_gml = _dot(hb, w_ref[:, C_GML:C_END])
    q_t = _dot_nt(wuq_ref[...], _rms_norm(z_cq, qng_ref[...]).astype(BF16))
    ckv = _rms_norm(z_ckv, kvg_ref[...])
    ckvb = ckv.astype(BF16)
    kn = _dot(ckvb, wuk_ref[...])
    v_t = _dot_nt(wuvt_ref[...], ckvb)
    u = gate_c * conv_h
    ubuf[8:8 + rows, :] = u
    cst_ref[b] = u[rows - 2:rows, :]
    conv_out = (cw_ref[0:1, :] * ubuf[6:6 + rows, :] + cw_ref[1:2, :] * ubuf[7:7 + rows, :]
                + cw_ref[2:3, :] * u)
    y_conv = _dot((gate_b * conv_out).astype(BF16), wco_ref[...])

    kpe = _rope_group(z_kpe, kc_ref[...], ka_ref[...], kb_ref[...])
    q_cos, q_sin = qcos_ref[...], qsin_ref[...]
    for g in range(HEADS):
        group = slice(g * HEAD_PAD, (g + 1) * HEAD_PAD)
        k_ref[:, group] = (kn[:, group] + kpe).astype(BF16)
        lo = g * HEAD_PAD + ROPE_LO
        x1, x2 = q_t[lo:lo + HALF], q_t[lo + HALF:lo + 2 * HALF]
        qt_ref[g * HEAD_PAD:lo, :] = (q_t[g * HEAD_PAD:lo] * q_scale).astype(BF16)
        qt_ref[lo:lo + HALF, :] = (x1 * q_cos - x2 * q_sin).astype(BF16)
        qt_ref[lo + HALF:lo + 2 * HALF, :] = (x2 * q_cos + x1 * q_sin).astype(BF16)
        qt_ref[lo + 2 * HALF:(g + 1) * HEAD_PAD, :] = jnp.zeros((HEAD_PAD - ROPE_LO - QK_ROPE, rows), BF16)
    ckv_ref[...] = ckv.reshape(ckv_ref.shape)
    kpet_ref[...] = kpe.T[ROPE_LO:ROPE_LO + QK_ROPE, :]
    vt_ref[...] = v_t.astype(BF16)
    gcy_ref[...] = (jax.nn.sigmoid(z_gcv) * y_conv).astype(BF16)
    gm_ref[...] = jax.nn.sigmoid(z_gml).astype(BF16)


def _proj_sample_kernel(x_ref, lng_ref, lnb_ref, w_ref, qng_ref, wuq_ref, kvg_ref, wukt_ref,
                        cw_ref, wco_ref, qc_ref, qa_ref, qb_ref, kc_ref, ka_ref, kb_ref,
                        s0_ref, s1_ref,
                        qlat_ref, qpe_ref, ckv_ref, kpe_ref, gcy_ref, gm_ref, u_ref, ubuf):
    rows = x_ref.shape[0]
    n_seq = rows // 8
    hb = _layer_norm(x_ref[...], lng_ref[...], lnb_ref[...]).astype(BF16)
    z_cq_kpe = _dot(hb, w_ref[:, C_CQ:C_CKV])
    z_ckv = _dot(hb, w_ref[:, C_CKV:C_GB])
    gate_b = _dot(hb, w_ref[:, C_GB:C_GC])
    gate_c = _dot(hb, w_ref[:, C_GC:C_CH])
    conv_h = _dot(hb, w_ref[:, C_CH:C_GCV])
    z_gcv = _dot(hb, w_ref[:, C_GCV:C_GML])
    z_gml = _dot(hb, w_ref[:, C_GML:C_END])
    q = _dot(_rms_norm(z_cq_kpe[:, C_CQ:C_KPE], qng_ref[...]).astype(BF16), wuq_ref[...])
    u = gate_c * conv_h
    u_ref[...] = u
    ubuf[8:8 + rows, :] = u
    ubuf[6:8, :] = jnp.zeros((2, CONV_DIM), F32)
    tok = lax.broadcasted_iota(jnp.int32, (rows, 1), 0) % 8
    um1 = jnp.where(tok == 0, s1_ref[...], ubuf[7:7 + rows, :])
    um2 = jnp.where(tok == 0, s0_ref[...], jnp.where(tok == 1, s1_ref[...], ubuf[6:6 + rows, :]))
    conv_out = cw_ref[0:1, :] * um2 + cw_ref[1:2, :] * um1 + cw_ref[2:3, :] * u
    y_conv = _dot((gate_b * conv_out).astype(BF16), wco_ref[...])

    for g in range(HEADS):
        qg = _rope_group(q[:, g * HEAD_PAD:(g + 1) * HEAD_PAD], qc_ref[...], qa_ref[...], qb_ref[...])
        qlat_ref[:, g] = _dot(qg.astype(BF16), wukt_ref[g]).reshape(n_seq, 8, KV_LORA)
        qpe_ref[:, g] = qg[:, ROPE_LO:ROPE_LO + QK_ROPE].reshape(n_seq, 8, QK_ROPE)
    ckv_ref[...] = _rms_norm(z_ckv, kvg_ref[...])
    kpe_ref[...] = _rope_group(z_cq_kpe[:, C_KPE:C_CKV], kc_ref[...], ka_ref[...], kb_ref[...])
    gcy_ref[...] = (jax.nn.sigmoid(z_gcv) * y_conv).astype(BF16)
    gm_ref[...] = jax.nn.sigmoid(z_gml).astype(BF16)


def _attn_prompt_kernel(qt_ref, k_ref, vt_ref, km_ref, vmt_ref, o_ref, m_ref, acc_ref, *, tq):
    tiles = qt_ref.shape[1] // tq
    assert tiles * tq == k_ref.shape[0]
    for qi in range(tiles):
        rows = slice(qi * tq, (qi + 1) * tq)
        _attn_prompt_tile(qi, qt_ref.at[:, rows], k_ref, vt_ref, km_ref, vmt_ref, o_ref.at[rows], m_ref, acc_ref)


def _attn_prompt_tile(qi, qt_ref, k_ref, vt_ref, km_ref, vmt_ref, o_ref, m_ref, acc_ref):
    tq = qt_ref.shape[1]
    hsl = lambda h: slice(h * HEAD_PAD, (h + 1) * HEAD_PAD)
    vsl = lambda h: slice(h * V_HEAD, (h + 1) * V_HEAD)

    def values_and_ones(v_t):
        return jnp.concatenate([v_t, jnp.ones((16, v_t.shape[1]), BF16)], axis=0)


    diag = pl.ds(qi * tq, tq)
    key_idx = lax.broadcasted_iota(jnp.int32, (tq, tq), 0)
    query_idx = lax.broadcasted_iota(jnp.int32, (tq, tq), 1)
    causal = key_idx <= query_idx
    scores = [_dot(jnp.concatenate([k_ref[diag, hsl(h)], km_ref[:, hsl(h)]], axis=0), qt_ref[hsl(h), :])
              for h in range(HEADS)]
    probs = []
    for h in range(HEADS):
        s_diag = jnp.where(causal, scores[h][0:tq], -jnp.inf)
        s_meta = scores[h][tq:tq + N_META]
        m = jnp.maximum(jnp.max(s_diag, axis=0, keepdims=True), jnp.max(s_meta, axis=0, keepdims=True))
        m_ref[h] = m
        probs.append((jnp.exp2(s_diag - m).astype(BF16), jnp.exp2(s_meta - m).astype(BF16)))
    for h in range(HEADS):
        acc_ref[h] = (_dot(values_and_ones(vt_ref[vsl(h), diag]), probs[h][0])
                      + _dot(values_and_ones(vmt_ref[vsl(h), :]), probs[h][1]))

    def visible_keys(k0, n_keys):
        keys = pl.ds(k0, n_keys)
        scores = [_dot(k_ref[keys, hsl(h)], qt_ref[hsl(h), :]) for h in range(HEADS)]
        alphas, probs = [], []
        for h in range(HEADS):
            m_prev = m_ref[h]
            m_new = jnp.maximum(m_prev, jnp.max(scores[h], axis=0, keepdims=True))
            m_ref[h] = m_new
            alphas.append(jnp.exp2(m_prev - m_new))
            probs.append(jnp.exp2(scores[h] - m_new).astype(BF16))
        for h in range(HEADS):
            acc_ref[h] = alphas[h] * acc_ref[h] + _dot(values_and_ones(vt_ref[vsl(h), keys]), probs[h])

    for i in range(qi // 2):
        visible_keys(i * (2 * tq), 2 * tq)
    if qi % 2 == 1:
        visible_keys((qi - 1) * tq, tq)

    o_t = jnp.concatenate(
        [acc_ref[h, 0:V_HEAD] / acc_ref[h, V_HEAD:V_HEAD + 1] for h in range(HEADS)], axis=0)
    o_ref[...] = o_t.T.astype(BF16)


def _attn_sample_kernel(pt_ref, qlat_ref, qpe_ref, ckvn_ref, kpen_ref, ckv_hbm, kpet_hbm, o_ref,
                        cbuf, kbuf, csem, ksem, *, n_pages):
    step = pl.program_id(0)
    slot = step % 2

    def page_copies(step_, slot_, j, p):
        pg = pt_ref[(step_ * SEQS_PER_STEP + j) * n_pages + p]
        tokens = pl.ds(p * PAGE, PAGE)
        return (pltpu.make_async_copy(ckv_hbm.at[pg], cbuf.at[slot_, j, tokens], csem.at[slot_]),
                pltpu.make_async_copy(kpet_hbm.at[pg], kbuf.at[slot_, j, :, tokens], ksem.at[slot_]))

    def for_each_page(step_, slot_, fn):
        for j in range(SEQS_PER_STEP):
            for p in range(n_pages):
                for copy in page_copies(step_, slot_, j, p):
                    fn(copy, p)

    start = lambda copy, p: copy.start(priority=p % 2)

    @pl.when(step == 0)
    def _():
        for_each_page(0, 0, start)

    @pl.when(step + 1 < pl.num_programs(0))
    def _():
        for_each_page(step + 1, 1 - slot, start)

    for_each_page(step, slot, lambda copy, p: copy.wait())

    n_rows = HEADS * 8
    chunk_keys = (n_pages // SAMPLE_CHUNKS) * PAGE
    tok = lax.broadcasted_iota(jnp.int32, (n_rows, 8), 0) % 8
    key = lax.broadcasted_iota(jnp.int32, (n_rows, 8), 1)

    qls, qps, states = [], [], []
    for j in range(SEQS_PER_STEP):
        ql = qlat_ref[j].reshape(n_rows, KV_LORA).astype(BF16)
        qp = qpe_ref[j].reshape(n_rows, QK_ROPE).astype(BF16)
        cn = ckvn_ref[j].astype(BF16)
        s_new = _dot_nt(ql, cn) + _dot_nt(qp, kpen_ref[j].astype(BF16))
        s_new = jnp.where(key <= tok, s_new, -jnp.inf)
        m = jnp.max(s_new, axis=-1, keepdims=True)
        p_new = jnp.exp(s_new - m)
        states.append((m, jnp.sum(p_new, axis=-1, keepdims=True), _dot(p_new.astype(BF16), cn)))
        qls.append(jnp.concatenate([ql, jnp.zeros((HEAD_PAD - n_rows, KV_LORA), BF16)], axis=0))
        qps.append(qp)

    def chunk_scores(j, i):
        keys = slice(i * chunk_keys, (i + 1) * chunk_keys)
        c = cbuf[slot, j, keys, :].astype(BF16)
        s = _dot_nt(c, qls[j]).T[0:n_rows] + _dot(qps[j], kbuf[slot, j, :, keys].astype(BF16))
        return c, s

    pending = [chunk_scores(j, 0) for j in range(SEQS_PER_STEP)]
    for i in range(SAMPLE_CHUNKS):
        current = pending
        if i + 1 < SAMPLE_CHUNKS:
            pending = [chunk_scores(j, i + 1) for j in range(SEQS_PER_STEP)]
        for j in range(SEQS_PER_STEP):
            c, s = current[j]
            m, denom, acc = states[j]
            m_new = jnp.maximum(m, jnp.max(s, axis=-1, keepdims=True))
            alpha = jnp.exp(m - m_new)
            p = jnp.exp(s - m_new)
            states[j] = (m_new, alpha * denom + jnp.sum(p, axis=-1, keepdims=True),
                         alpha * acc + _dot(p.astype(BF16), c))
    for j in range(SEQS_PER_STEP):
        _, denom, acc = states[j]
        o_ref[:, j * 8:(j + 1) * 8, :] = (acc / denom).reshape(HEADS, 8, KV_LORA)


def _post_kernel(*refs, sample):
    if sample:
        (olat_ref, wuvp_ref, x_ref, gcy_ref, gm_ref, lng_ref, lnb_ref, womla_ref, wo_ref,
         l1g_ref, l1b_ref, wup_ref, wdn_ref, l2g_ref, l2b_ref, y_ref) = refs
        pairs = []
        for pair in range(HEADS // 2):
            pairs.append(
                _dot(olat_ref[2 * pair].astype(BF16), wuvp_ref[2 * pair])
                + _dot(olat_ref[2 * pair + 1].astype(BF16), wuvp_ref[2 * pair + 1]))
        y_mla = sum(
            _dot(pairs[pair].astype(BF16), womla_ref[pair * HEAD_PAD:(pair + 1) * HEAD_PAD, :])
            for pair in range(HEADS // 2))
    else:
        (o_ref, x_ref, gcy_ref, gm_ref, lng_ref, lnb_ref, womla_ref, wo_ref,
         l1g_ref, l1b_ref, wup_ref, wdn_ref, l2g_ref, l2b_ref, y_ref) = refs
        y_mla = _dot(o_ref[...], womla_ref[...])
    rows = x_ref.shape[0]
    halves = [slice(0, rows // 2), slice(rows // 2, rows)]
    mixes = []
    for r in halves:
        mix_in = gcy_ref[r, :].astype(F32) + gm_ref[r, :].astype(F32) * y_mla[r, :]
        mixes.append(_dot(mix_in.astype(BF16), wo_ref[...]))
    h1s = []
    for r, mix in zip(halves, mixes):
        h = _layer_norm(x_ref[r, :], lng_ref[...], lnb_ref[...])
        h1s.append(_layer_norm(ALPHA * h + mix, l1g_ref[...], l1b_ref[...]))
    h1bs = [h1.astype(BF16) for h1 in h1s]
    ffs = [None, None]
    chunk = D_MODEL
    for c in range(D_FF // chunk):
        acts = [jnp.maximum(_dot(h1b, wup_ref[:, c * chunk:(c + 1) * chunk]), 0.0) for h1b in h1bs]
        for i, a in enumerate(acts):
            part = _dot((a * a).astype(BF16), wdn_ref[c * chunk:(c + 1) * chunk, :])
            ffs[i] = part if ffs[i] is None else ffs[i] + part
    for r, h1, ff in zip(halves, h1s, ffs):
        y_ref[r, :] = _layer_norm(ALPHA * h1 + ff, l2g_ref[...], l2b_ref[...])


def _full(shape):
    n = len(shape)
    return pl.BlockSpec(shape, lambda *_: (0,) * n)


def _rope_cos_sin(pos, transposed=False):
    inv_freq = ROPE_THETA ** (-2.0 * jnp.arange(HALF, dtype=F32) / QK_ROPE)
    ang = pos.astype(F32)[:, None] * inv_freq[None, :]
    cos, sin = jnp.cos(ang), jnp.sin(ang)
    return (cos.T, sin.T) if transposed else (cos, sin)


def _rope_tables(pos, scale):
    cos, sin = _rope_cos_sin(pos)
    n = pos.shape[0]
    ones = jnp.ones((n, ROPE_LO), F32)
    z_lo = jnp.zeros((n, ROPE_LO), F32)
    z_hi = jnp.zeros((n, HEAD_PAD - ROPE_LO - QK_ROPE), F32)
    z_half = jnp.zeros((n, HALF), F32)
    c = jnp.concatenate([ones, cos, cos, z_hi + 1.0], axis=1)
    a = jnp.concatenate([z_lo, -sin, z_half, z_hi], axis=1)
    b = jnp.concatenate([z_lo, z_half, sin, z_hi], axis=1)
    return c * scale, a * scale, b * scale


def _pack_weights(w_in, w_uq, w_uk, w_uv):
    offs = [0]
    for s in (Q_LORA, KV_LORA, QK_ROPE, CONV_DIM, CONV_DIM, CONV_DIM, D_MODEL, D_MODEL):
        offs.append(offs[-1] + s)
    pieces = [w_in[:, offs[i]:offs[i + 1]] for i in range(8)]
    kpe_pad = jnp.pad(pieces[2], ((0, 0), (ROPE_LO, HEAD_PAD - ROPE_LO - QK_ROPE)))
    w_pack = jnp.concatenate([pieces[0], kpe_pad, pieces[1]] + pieces[3:], axis=1).astype(BF16)
    head_pad = HEAD_PAD - QK_NOPE - QK_ROPE
    w_uq_p = jnp.pad(w_uq.reshape(Q_LORA, HEADS, QK_NOPE + QK_ROPE), ((0, 0), (0, 0), (0, head_pad)))
    w_uq_p = w_uq_p.reshape(Q_LORA, HEADS * HEAD_PAD).astype(BF16)
    w_uk_p = jnp.pad(w_uk, ((0, 0), (0, 0), (0, HEAD_PAD - QK_NOPE)))
    w_uk_p = w_uk_p.reshape(KV_LORA, HEADS * HEAD_PAD).astype(BF16)
    w_uk_t = jnp.pad(jnp.transpose(w_uk, (1, 2, 0)), ((0, 0), (0, HEAD_PAD - QK_NOPE), (0, 0))).astype(BF16)
    w_uv_t = w_uv.reshape(KV_LORA, HEADS * V_HEAD).T.astype(BF16)
    w_uv_even = jnp.pad(w_uv, ((0, 0), (0, 0), (0, V_HEAD)))
    w_uv_odd = jnp.pad(w_uv, ((0, 0), (0, 0), (V_HEAD, 0)))
    is_even = (jnp.arange(HEADS) % 2 == 0)[None, :, None]
    w_uv_p = jnp.transpose(jnp.where(is_even, w_uv_even, w_uv_odd), (1, 0, 2)).astype(BF16)
    return w_pack, w_uq_p, w_uk_p, w_uk_t, w_uv_t, w_uv_p


def _proj_seq(x, pos, init, shared, tile, ckv_base):
    nb, n, _ = x.shape
    row_offset = ckv_base.shape[1] - n
    nt = n // tile
    q_scale = SM_SCALE * LOG2_E
    q_tabs = tuple(t * q_scale for t in _rope_cos_sin(pos, transposed=True))
    k_tabs = _rope_tables(pos, 1.0)
    row_spec = lambda width: pl.BlockSpec((None, tile, width), lambda t, b: (b, t, 0))
    tab_spec = pl.BlockSpec((tile, HEAD_PAD), lambda t, b: (t, 0))
    tab_t_spec = pl.BlockSpec((HALF, tile), lambda t, b: (0, t))
    ckv_spec = pl.BlockSpec((pl.Element(1), pl.Element(tile), pl.Element(KV_LORA)),
                            lambda t, b: (b, pl.multiple_of(row_offset + t * tile, 8), 0))
    in_specs = ([row_spec(D_MODEL)] + [_full(a.shape) for a in shared] + [tab_t_spec] * 2 + [tab_spec] * 3
                + [_full(init.shape), pl.BlockSpec(memory_space=pl.ANY)])
    ins = (x, *shared, *q_tabs, *k_tabs, init, ckv_base)
    out_shape = (
        jax.ShapeDtypeStruct((nb, HEADS * HEAD_PAD, n), BF16),
        jax.ShapeDtypeStruct((nb, n, HEADS * HEAD_PAD), BF16),
        jax.ShapeDtypeStruct((nb, HEADS * V_HEAD, n), BF16),
        jax.ShapeDtypeStruct(ckv_base.shape, F32),
        jax.ShapeDtypeStruct((nb, QK_ROPE, n), F32),
        jax.ShapeDtypeStruct((nb, n, D_MODEL), BF16),
        jax.ShapeDtypeStruct((nb, n, D_MODEL), BF16),
        jax.ShapeDtypeStruct((nb, CONV_K - 1, CONV_DIM), F32),
    )
    col_spec = lambda height: pl.BlockSpec((None, height, tile), lambda t, b: (b, 0, t))
    out_specs = (col_spec(HEADS * HEAD_PAD), row_spec(HEADS * HEAD_PAD), col_spec(HEADS * V_HEAD),
                 ckv_spec, col_spec(QK_ROPE), row_spec(D_MODEL), row_spec(D_MODEL),
                 _full((nb, CONV_K - 1, CONV_DIM)))
    return pl.pallas_call(
        functools.partial(_proj_seq_kernel, q_scale=q_scale),
        grid=(nt, nb),
        in_specs=in_specs,
        out_specs=out_specs,
        out_shape=out_shape,
        scratch_shapes=[pltpu.VMEM((tile + 8, CONV_DIM), F32)],
        input_output_aliases={len(ins) - 1: 3},
        compiler_params=pltpu.CompilerParams(
            dimension_semantics=("arbitrary", "arbitrary"), vmem_limit_bytes=VMEM_LIMIT),
        name="proj_seq",
    )(*ins)


def _proj_sample(x, tabs, s0, s1, shared, tile):
    n = x.shape[0]
    n_seq = n // 8
    row = lambda width: pl.BlockSpec((tile, width), lambda i: (i, 0))
    seq4 = lambda width: pl.BlockSpec((tile // 8, HEADS, 8, width), lambda i: (i, 0, 0, 0))
    ins = (x, *shared, *tabs, s0, s1)
    in_specs = ([row(D_MODEL)] + [_full(a.shape) for a in shared] + [row(HEAD_PAD)] * 6
                + [row(CONV_DIM)] * 2)
    out_specs = (seq4(KV_LORA), seq4(QK_ROPE), row(KV_LORA), row(HEAD_PAD), row(D_MODEL), row(D_MODEL),
                 row(CONV_DIM))
    out_shape = (
        jax.ShapeDtypeStruct((n_seq, HEADS, 8, KV_LORA), F32),
        jax.ShapeDtypeStruct((n_seq, HEADS, 8, QK_ROPE), F32),
        jax.ShapeDtypeStruct((n, KV_LORA), F32),
        jax.ShapeDtypeStruct((n, HEAD_PAD), F32),
        jax.ShapeDtypeStruct((n, D_MODEL), BF16),
        jax.ShapeDtypeStruct((n, D_MODEL), BF16),
        jax.ShapeDtypeStruct((n, CONV_DIM), F32),
    )
    return pl.pallas_call(
        _proj_sample_kernel,
        grid=(n // tile,),
        in_specs=in_specs,
        out_specs=out_specs,
        out_shape=out_shape,
        scratch_shapes=[pltpu.VMEM((tile + 8, CONV_DIM), F32)],
        compiler_params=pltpu.CompilerParams(
            dimension_semantics=("arbitrary",), vmem_limit_bytes=VMEM_LIMIT),
        name="proj_sample",
    )(*ins)


def _attn_prompt(q_t, k, v_t, k_meta, v_meta_t, tq, tiles_per_step):
    nb, n, _ = k.shape
    step_rows = tq * tiles_per_step
    return pl.pallas_call(
        functools.partial(_attn_prompt_kernel, tq=tq),
        grid=(nb, n // step_rows),
        in_specs=[
            pl.BlockSpec((None, HEADS * HEAD_PAD, step_rows), lambda b, i: (b, 0, i)),
            pl.BlockSpec((None, n, HEADS * HEAD_PAD), lambda b, i: (b, 0, 0)),
            pl.BlockSpec((None, HEADS * V_HEAD, n), lambda b, i: (b, 0, 0)),
            _full(k_meta.shape),
            _full(v_meta_t.shape),
        ],
        out_specs=pl.BlockSpec((None, step_rows, HEADS * V_HEAD), lambda b, i: (b, i, 0)),
        out_shape=jax.ShapeDtypeStruct((nb, n, HEADS * V_HEAD), BF16),
        scratch_shapes=[pltpu.VMEM((HEADS, 1, tq), F32), pltpu.VMEM((HEADS, V_HEAD + 16, tq), F32)],
        compiler_params=pltpu.CompilerParams(
            dimension_semantics=("arbitrary", "arbitrary"), vmem_limit_bytes=VMEM_LIMIT),
        name="attn_prompt",
    )(q_t, k, v_t, k_meta, v_meta_t)


def _attn_sample(page_table, q_lat, q_pe, ckv_new, kpe_new, ckv_pool, kpe_pool_t):
    n_seq, n_pages = page_table.shape
    seq4 = lambda width: pl.BlockSpec((SEQS_PER_STEP, HEADS, 8, width), lambda s, pt: (s, 0, 0, 0))
    seq3 = lambda width: pl.BlockSpec((SEQS_PER_STEP, 8, width), lambda s, pt: (s, 0, 0))
    hbm = pl.BlockSpec(memory_space=pl.ANY)
    grid_spec = pltpu.PrefetchScalarGridSpec(
        num_scalar_prefetch=1,
        grid=(n_seq // SEQS_PER_STEP,),
        in_specs=[seq4(KV_LORA), seq4(QK_ROPE), seq3(KV_LORA), seq3(QK_ROPE), hbm, hbm],
        out_specs=pl.BlockSpec((HEADS, SEQS_PER_STEP * 8, KV_LORA), lambda s, pt: (0, s, 0)),
        scratch_shapes=[pltpu.VMEM((2, SEQS_PER_STEP, n_pages * PAGE, KV_LORA), F32),
                        pltpu.VMEM((2, SEQS_PER_STEP, QK_ROPE, n_pages * PAGE), F32),
                        pltpu.SemaphoreType.DMA((2,)), pltpu.SemaphoreType.DMA((2,))],
    )
    return pl.pallas_call(
        functools.partial(_attn_sample_kernel, n_pages=n_pages),
        grid_spec=grid_spec,
        out_shape=jax.ShapeDtypeStruct((HEADS, n_seq * 8, KV_LORA), F32),
        compiler_params=pltpu.CompilerParams(
            dimension_semantics=("arbitrary",), vmem_limit_bytes=VMEM_LIMIT),
        name="attn_sample",
    )(page_table.reshape(-1), q_lat, q_pe, ckv_new, kpe_new, ckv_pool, kpe_pool_t)


def _post(front, x, gcy, gm, weights, tile, sample):
    n = x.shape[0]
    row = lambda width: pl.BlockSpec((tile, width), lambda i: (i, 0))
    if sample:
        o_lat, w_uv_p = front
        front_specs = [pl.BlockSpec((HEADS, tile, KV_LORA), lambda i: (0, i, 0)), _full(w_uv_p.shape)]
    else:
        front_specs = [row(HEADS * V_HEAD)]
    in_specs = front_specs + [row(D_MODEL), row(D_MODEL), row(D_MODEL)] + [_full(w.shape) for w in weights]
    return pl.pallas_call(
        functools.partial(_post_kernel, sample=sample),
        grid=(n // tile,),
        in_specs=in_specs,
        out_specs=row(D_MODEL),
        out_shape=jax.ShapeDtypeStruct((n, D_MODEL), F32),
        compiler_params=pltpu.CompilerParams(
            dimension_semantics=("arbitrary",), vmem_limit_bytes=VMEM_LIMIT),
        name="post_sample" if sample else "post_prompt",
    )(*front, x, gcy, gm, *weights)


def kernel(x_prompt, x_sample, cache_ckv, cache_kpe, state_conv, page_table, meta_tokens, ln_emb_g, ln_emb_b, w_in, q_norm_g, w_uq, kv_norm_g, w_uk, w_uv, w_o_mla, conv_w, w_conv_out, w_o, ln1_g, ln1_b, w_up, w_down, ln2_g, ln2_b):
    assert w_in.shape[0] == DEPTH
    nb, seq, _ = x_prompt.shape
    n_seq, dec_seq, _ = x_sample.shape
    assert dec_seq == 8
    past_len = page_table.shape[1] * PAGE

    row2 = lambda a: a.reshape(1, -1).astype(F32)
    w_pack, w_uq_p, w_uk_p, w_uk_t, w_uv_t, w_uv_p = _pack_weights(w_in[0], w_uq[0], w_uk[0], w_uv[0])
    w_co = w_conv_out[0].astype(BF16)
    lng, lnb = row2(ln_emb_g), row2(ln_emb_b)
    qng, kvg = row2(q_norm_g[0]), row2(kv_norm_g[0])
    shared_seq = (lng, lnb, w_pack, qng, w_uq_p.T, kvg, w_uk_p, w_uv_t, conv_w[0], w_co)
    shared_sample = (lng, lnb, w_pack, qng, w_uq_p, kvg, w_uk_t, conv_w[0], w_co)
    post_w = (lng, lnb, w_o_mla[0].astype(BF16), w_o[0].astype(BF16), row2(ln1_g[0]), row2(ln1_b[0]),
              w_up[0].astype(BF16), w_down[0].astype(BF16), row2(ln2_g[0]), row2(ln2_b[0]))

    zero_state = jnp.zeros((CONV_K - 1, CONV_DIM), F32)
    _, k_meta, vt_meta, ckv_meta, kpet_meta, _, _, conv_meta = _proj_seq(
        meta_tokens[None], jnp.arange(N_META), zero_state, shared_seq, N_META,
        jnp.zeros((1, N_META, KV_LORA), F32))

    q_t, k, v_t, ckv_p, kpet_p, gcy_p, gm_p, conv_p = _proj_seq(
        x_prompt, N_META + jnp.arange(seq), conv_meta[0], shared_seq, DENSE_TILE,
        jnp.pad(jnp.broadcast_to(ckv_meta, (nb, N_META, KV_LORA)), ((0, 0), (0, seq), (0, 0))))
    o_p = _attn_prompt(q_t, k, v_t, k_meta[0], vt_meta[0], QUERY_TILE, QUERY_TILES_PER_STEP)
    n_p = nb * seq
    y_prompt = _post((o_p.reshape(n_p, -1),), x_prompt.reshape(n_p, D_MODEL), gcy_p.reshape(n_p, D_MODEL),
                     gm_p.reshape(n_p, D_MODEL), post_w, DENSE_TILE, sample=False).reshape(nb, seq, D_MODEL)

    n_s = n_seq * dec_seq
    pos_s = past_len + jnp.arange(dec_seq)
    tabs_s = tuple(jnp.tile(t, (n_seq, 1)) for t in _rope_tables(pos_s, SM_SCALE) + _rope_tables(pos_s, 1.0))
    s0 = jnp.repeat(state_conv[0, :, 0], dec_seq, axis=0)
    s1 = jnp.repeat(state_conv[0, :, 1], dec_seq, axis=0)
    xs = x_sample.reshape(n_s, D_MODEL)
    q_lat, q_pe, ckv_s, kpe_s128, gcy_s, gm_s, u_s = _proj_sample(xs, tabs_s, s0, s1, shared_sample, SAMPLE_PROJ_TILE)
    kpe_s = kpe_s128[:, ROPE_LO:ROPE_LO + QK_ROPE]
    o_lat = _attn_sample(page_table, q_lat, q_pe, ckv_s.reshape(n_seq, dec_seq, KV_LORA),
                         kpe_s.reshape(n_seq, dec_seq, QK_ROPE), cache_ckv[0],
                         jnp.swapaxes(cache_kpe[0], 1, 2))
    y_sample = _post((o_lat, w_uv_p), xs, gcy_s, gm_s, post_w, DENSE_TILE, sample=True).reshape(
        n_seq, dec_seq, D_MODEL)

    new_kpe_prompt = jnp.swapaxes(
        jnp.concatenate([jnp.broadcast_to(kpet_meta, (nb, QK_ROPE, N_META)), kpet_p], axis=2), 1, 2)[None]
    new_conv_sample = u_s.reshape(n_seq, dec_seq, CONV_DIM)[:, dec_seq - (CONV_K - 1):][None]
    return (y_prompt, y_sample, ckv_p[None], new_kpe_prompt, conv_p[None],
            ckv_s.reshape(n_seq, dec_seq, KV_LORA)[None], kpe_s.reshape(n_seq, dec_seq, QK_ROPE)[None],
            new_conv_sample)
```

```python
import functools

import jax
import jax.numpy as jnp
from jax import lax
from jax.experimental import pallas as pl
from jax.experimental.pallas import tpu as pltpu

D_MODEL = 1024
N_META = 16
HEADS = 8
Q_LORA = 384
KV_LORA = 256
QK_NOPE = 64
QK_ROPE = 32
V_HEAD = 64
CONV_DIM = 512
CONV_K = 3
D_FF = 4096
PAGE = 128
ROPE_THETA = 10000.0
LN_EPS = 1e-5
RMS_EPS = 1e-6
DEPTH = 1
ALPHA = (2.0 * DEPTH) ** 0.25
SM_SCALE = (QK_NOPE + QK_ROPE) ** -0.5
LOG2_E = 1.4426950408889634

SAMPLE_CHUNKS = 4
SEQS_PER_STEP = 2
HEAD_PAD = 128
ROPE_LO = QK_NOPE
HALF = QK_ROPE // 2

C_CQ = 0
C_KPE = C_CQ + Q_LORA
C_CKV = C_KPE + HEAD_PAD
C_GB = C_CKV + KV_LORA
C_GC = C_GB + CONV_DIM
C_CH = C_GC + CONV_DIM
C_GCV = C_CH + CONV_DIM
C_GML = C_GCV + D_MODEL
C_END = C_GML + D_MODEL

VMEM_LIMIT = 56 * 1024 * 1024

DENSE_TILE = 512
SAMPLE_PROJ_TILE = 256
QUERY_TILE = 256
QUERY_TILES_PER_STEP = 8

F32 = jnp.float32
BF16 = jnp.bfloat16


def _dot(a, b):
    return jnp.dot(a, b, preferred_element_type=F32)


def _dot_nt(a, b):
    return lax.dot_general(a, b, (((1,), (1,)), ((), ())), preferred_element_type=F32)


def _layer_norm(x, g, b):
    mu = jnp.mean(x, axis=-1, keepdims=True)
    xc = x - mu
    var = jnp.mean(xc * xc, axis=-1, keepdims=True)
    return xc * lax.rsqrt(var + LN_EPS) * g + b


def _rms_norm(x, g):
    return x * lax.rsqrt(jnp.mean(x * x, axis=-1, keepdims=True) + RMS_EPS) * g


def _rope_group(x, c, a, b):
    return x * c + pltpu.roll(x, HEAD_PAD - HALF, 1) * a + pltpu.roll(x, HALF, 1) * b


def _proj_seq_kernel(x_ref, lng_ref, lnb_ref, w_ref, qng_ref, wuq_ref, kvg_ref, wuk_ref, wuvt_ref,
                     cw_ref, wco_ref, qcos_ref, qsin_ref, kc_ref, ka_ref, kb_ref, init_ref, ckv_base_ref,
                     qt_ref, k_ref, vt_ref, ckv_ref, kpet_ref, gcy_ref, gm_ref, cst_ref, ubuf, *, q_scale):
    del ckv_base_ref
    t = pl.program_id(0)
    b = pl.program_id(1)
    rows = x_ref.shape[0]

    @pl.when(t == 0)
    def _():
        ubuf[6:8, :] = init_ref[...]

    @pl.when(t > 0)
    def _():
        ubuf[6:8, :] = cst_ref[b]

    hb = _layer_norm(x_ref[...], lng_ref[...], lnb_ref[...]).astype(BF16)
    z_cq_kpe = _dot(hb, w_ref[:, C_CQ:C_CKV])
    z_cq = z_cq_kpe[:, C_CQ:C_KPE]
    z_kpe = z_cq_kpe[:, C_KPE:C_CKV]
    z_ckv = _dot(hb, w_ref[:, C_CKV:C_GB])
    gate_b = _dot(hb, w_ref[:, C_GB:C_GC])
    gate_c = _dot(hb, w_ref[:, C_GC:C_CH])
    conv_h = _dot(hb, w_ref[:, C_CH:C_GCV])
    z_gcv = _dot(hb, w_ref[:, C_GCV:C_GML])
    z_gml = _dot(hb, w_ref[:, C_GML:C_END])
    q_t = _dot_nt(wuq_ref[...], _rms_norm(z_cq, qng_ref[...]).astype(BF16))
    ckv = _rms_norm(z_ckv, kvg_ref[...])
    ckvb = ckv.astype(BF16)
    kn = _dot(ckvb, wuk_ref[...])
    v_t = _dot_nt(wuvt_ref[...], ckvb)
    u = gate_c * conv_h
    ubuf[8:8 + rows, :] = u
    cst_ref[b] = u[rows - 2:rows, :]
    conv_out = (cw_ref[0:1, :] * ubuf[6:6 + rows, :] + cw_ref[1:2, :] * ubuf[7:7 + rows, :]
                + cw_ref[2:3, :] * u)
    y_conv = _dot((gate_b * conv_out).astype(BF16), wco_ref[...])

    kpe = _rope_group(z_kpe, kc_ref[...], ka_ref[...], kb_ref[...])
    q_cos, q_sin = qcos_ref[...], qsin_ref[...]
    for g in range(HEADS):
        group = slice(g * HEAD_PAD, (g + 1) * HEAD_PAD)
        k_ref[:, group] = (kn[:, group] + kpe).astype(BF16)
        lo = g * HEAD_PAD + ROPE_LO
        x1, x2 = q_t[lo:lo + HALF], q_t[lo + HALF:lo + 2 * HALF]
        qt_ref[g * HEAD_PAD:lo, :] = (q_t[g * HEAD_PAD:lo] * q_scale).astype(BF16)
        qt_ref[lo:lo + HALF, :] = (x1 * q_cos - x2 * q_sin).astype(BF16)
        qt_ref[lo + HALF:lo + 2 * HALF, :] = (x2 * q_cos + x1 * q_sin).astype(BF16)
        qt_ref[lo + 2 * HALF:(g + 1) * HEAD_PAD, :] = jnp.zeros((HEAD_PAD - ROPE_LO - QK_ROPE, rows), BF16)
    ckv_ref[...] = ckv.reshape(ckv_ref.shape)
    kpet_ref[...] = kpe.T[ROPE_LO:ROPE_LO + QK_ROPE, :]
    vt_ref[...] = v_t.astype(BF16)
    gcy_ref[...] = (jax.nn.sigmoid(z_gcv) * y_conv).astype(BF16)
    gm_ref[...] = jax.nn.sigmoid(z_gml).astype(BF16)


def _proj_sample_kernel(x_ref, lng_ref, lnb_ref, w_ref, qng_ref, wuq_ref, kvg_ref, wukt_ref,
                        cw_ref, wco_ref, qc_ref, qa_ref, qb_ref, kc_ref, ka_ref, kb_ref,
                        s0_ref, s1_ref,
                        qlat_ref, qpe_ref, ckv_ref, kpe_ref, gcy_ref, gm_ref, u_ref, ubuf):
    rows = x_ref.shape[0]
    n_seq = rows // 8
    hb = _layer_norm(x_ref[...], lng_ref[...], lnb_ref[...]).astype(BF16)
    z_cq_kpe = _dot(hb, w_ref[:, C_CQ:C_CKV])
    z_ckv = _dot(hb, w_ref[:, C_CKV:C_GB])
    gate_b = _dot(hb, w_ref[:, C_GB:C_GC])
    gate_c = _dot(hb, w_ref[:, C_GC:C_CH])
    conv_h = _dot(hb, w_ref[:, C_CH:C_GCV])
    z_gcv = _dot(hb, w_ref[:, C_GCV:C_GML])
    z_gml = _dot(hb, w_ref[:, C_GML:C_END])
    q = _dot(_rms_norm(z_cq_kpe[:, C_CQ:C_KPE], qng_ref[...]).astype(BF16), wuq_ref[...])
    u = gate_c * conv_h
    u_ref[...] = u
    ubuf[8:8 + rows, :] = u
    ubuf[6:8, :] = jnp.zeros((2, CONV_DIM), F32)
    tok = lax.broadcasted_iota(jnp.int32, (rows, 1), 0) % 8
    um1 = jnp.where(tok == 0, s1_ref[...], ubuf[7:7 + rows, :])
    um2 = jnp.where(tok == 0, s0_ref[...], jnp.where(tok == 1, s1_ref[...], ubuf[6:6 + rows, :]))
    conv_out = cw_ref[0:1, :] * um2 + cw_ref[1:2, :] * um1 + cw_ref[2:3, :] * u
    y_conv = _dot((gate_b * conv_out).astype(BF16), wco_ref[...])

    for g in range(HEADS):
        qg = _rope_group(q[:, g * HEAD_PAD:(g + 1) * HEAD_PAD], qc_ref[...], qa_ref[...], qb_ref[...])
        qlat_ref[:, g] = _dot(qg.astype(BF16), wukt_ref[g]).reshape(n_seq, 8, KV_LORA)
        qpe_ref[:, g] = qg[:, ROPE_LO:ROPE_LO + QK_ROPE].reshape(n_seq, 8, QK_ROPE)
    ckv_ref[...] = _rms_norm(z_ckv, kvg_ref[...])
    kpe_ref[...] = _rope_group(z_cq_kpe[:, C_KPE:C_CKV], kc_ref[...], ka_ref[...], kb_ref[...])
    gcy_ref[...] = (jax.nn.sigmoid(z_gcv) * y_conv).astype(BF16)
    gm_ref[...] = jax.nn.sigmoid(z_gml).astype(BF16)


def _attn_prompt_kernel(qt_ref, k_ref, vt_ref, km_ref, vmt_ref, o_ref, m_ref, acc_ref, *, tq):
    tiles = qt_ref.shape[1] // tq
    assert tiles * tq == k_ref.shape[0]
    for qi in range(tiles):
        rows = slice(qi * tq, (qi + 1) * tq)
        _attn_prompt_tile(qi, qt_ref.at[:, rows], k_ref, vt_ref, km_ref, vmt_ref, o_ref.at[rows], m_ref, acc_ref)


def _attn_prompt_tile(qi, qt_ref, k_ref, vt_ref, km_ref, vmt_ref, o_ref, m_ref, acc_ref):
    tq = qt_ref.shape[1]
    hsl = lambda h: slice(h * HEAD_PAD, (h + 1) * HEAD_PAD)
    vsl = lambda h: slice(h * V_HEAD, (h + 1) * V_HEAD)

    def values_and_ones(v_t):
        return jnp.concatenate([v_t, jnp.ones((16, v_t.shape[1]), BF16)], axis=0)


    diag = pl.ds(qi * tq, tq)
    key_idx = lax.broadcasted_iota(jnp.int32, (tq, tq), 0)
    query_idx = lax.broadcasted_iota(jnp.int32, (tq, tq), 1)
    causal = key_idx <= query_idx
    scores = [_dot(jnp.concatenate([k_ref[diag, hsl(h)], km_ref[:, hsl(h)]], axis=0), qt_ref[hsl(h), :])
              for h in range(HEADS)]
    probs = []
    for h in range(HEADS):
        s_diag = jnp.where(causal, scores[h][0:tq], -jnp.inf)
        s_meta = scores[h][tq:tq + N_META]
        m = jnp.maximum(jnp.max(s_diag, axis=0, keepdims=True), jnp.max(s_meta, axis=0, keepdims=True))
        m_ref[h] = m
        probs.append((jnp.exp2(s_diag - m).astype(BF16), jnp.exp2(s_meta - m).astype(BF16)))
    for h in range(HEADS):
        acc_ref[h] = (_dot(values_and_ones(vt_ref[vsl(h), diag]), probs[h][0])
                      + _dot(values_and_ones(vmt_ref[vsl(h), :]), probs[h][1]))

    def visible_keys(k0, n_keys):
        keys = pl.ds(k0, n_keys)
        scores = [_dot(k_ref[keys, hsl(h)], qt_ref[hsl(h), :]) for h in range(HEADS)]
        alphas, probs = [], []
        for h in range(HEADS):
            m_prev = m_ref[h]
            m_new = jnp.maximum(m_prev, jnp.max(scores[h], axis=0, keepdims=True))
            m_ref[h] = m_new
            alphas.append(jnp.exp2(m_prev - m_new))
            probs.append(jnp.exp2(scores[h] - m_new).astype(BF16))
        for h in range(HEADS):
            acc_ref[h] = alphas[h] * acc_ref[h] + _dot(values_and_ones(vt_ref[vsl(h), keys]), probs[h])

    for i in range(qi // 2):
        visible_keys(i * (2 * tq), 2 * tq)
    if qi % 2 == 1:
        visible_keys((qi - 1) * tq, tq)

    o_t = jnp.concatenate(
        [acc_ref[h, 0:V_HEAD] / acc_ref[h, V_HEAD:V_HEAD + 1] for h in range(HEADS)], axis=0)
    o_ref[...] = o_t.T.astype(BF16)


def _attn_sample_kernel(pt_ref, qlat_ref, qpe_ref, ckvn_ref, kpen_ref, ckv_hbm, kpet_hbm, o_ref,
                        cbuf, kbuf, csem, ksem, *, n_pages):
    step = pl.program_id(0)
    slot = step % 2

    def page_copies(step_, slot_, j, p):
        pg = pt_ref[(step_ * SEQS_PER_STEP + j) * n_pages + p]
        tokens = pl.ds(p * PAGE, PAGE)
        return (pltpu.make_async_copy(ckv_hbm.at[pg], cbuf.at[slot_, j, tokens], csem.at[slot_]),
                pltpu.make_async_copy(kpet_hbm.at[pg], kbuf.at[slot_, j, :, tokens], ksem.at[slot_]))

    def for_each_page(step_, slot_, fn):
        for j in range(SEQS_PER_STEP):
            for p in range(n_pages):
                for copy in page_copies(step_, slot_, j, p):
                    fn(copy)

    @pl.when(step == 0)
    def _():
        for_each_page(0, 0, lambda copy: copy.start())

    @pl.when(step + 1 < pl.num_programs(0))
    def _():
        for_each_page(step + 1, 1 - slot, lambda copy: copy.start())

    for_each_page(step, slot, lambda copy: copy.wait())

    n_rows = HEADS * 8
    chunk_keys = (n_pages // SAMPLE_CHUNKS) * PAGE
    tok = lax.broadcasted_iota(jnp.int32, (n_rows, 8), 0) % 8
    key = lax.broadcasted_iota(jnp.int32, (n_rows, 8), 1)

    qls, qps, states = [], [], []
    for j in range(SEQS_PER_STEP):
        ql = qlat_ref[j].reshape(n_rows, KV_LORA).astype(BF16)
        qp = qpe_ref[j].reshape(n_rows, QK_ROPE).astype(BF16)
        cn = ckvn_ref[j].astype(BF16)
        s_new = _dot_nt(ql, cn) + _dot_nt(qp, kpen_ref[j].astype(BF16))
        s_new = jnp.where(key <= tok, s_new, -jnp.inf)
        m = jnp.max(s_new, axis=-1, keepdims=True)
        p_new = jnp.exp(s_new - m)
        states.append((m, jnp.sum(p_new, axis=-1, keepdims=True), _dot(p_new.astype(BF16), cn)))
        qls.append(jnp.concatenate([ql, jnp.zeros((HEAD_PAD - n_rows, KV_LORA), BF16)], axis=0))
        qps.append(qp)

    def chunk_scores(j, i):
        keys = slice(i * chunk_keys, (i + 1) * chunk_keys)
        c = cbuf[slot, j, keys, :].astype(BF16)
        s = _dot_nt(c, qls[j]).T[0:n_rows] + _dot(qps[j], kbuf[slot, j, :, keys].astype(BF16))
        return c, s

    pending = [chunk_scores(j, 0) for j in range(SEQS_PER_STEP)]
    for i in range(SAMPLE_CHUNKS):
        current = pending
        if i + 1 < SAMPLE_CHUNKS:
            pending = [chunk_scores(j, i + 1) for j in range(SEQS_PER_STEP)]
        for j in range(SEQS_PER_STEP):
            c, s = current[j]
            m, denom, acc = states[j]
            m_new = jnp.maximum(m, jnp.max(s, axis=-1, keepdims=True))
            alpha = jnp.exp(m - m_new)
            p = jnp.exp(s - m_new)
            states[j] = (m_new, alpha * denom + jnp.sum(p, axis=-1, keepdims=True),
                         alpha * acc + _dot(p.astype(BF16), c))
    for j in range(SEQS_PER_STEP):
        _, denom, acc = states[j]
        o_ref[:, j * 8:(j + 1) * 8, :] = (acc / denom).reshape(HEADS, 8, KV_LORA)


def _post_kernel(*refs, sample):
    if sample:
        (olat_ref, wuvp_ref, x_ref, gcy_ref, gm_ref, lng_ref, lnb_ref, womla_ref, wo_ref,
         l1g_ref, l1b_ref, wup_ref, wdn_ref, l2g_ref, l2b_ref, y_ref) = refs
        pairs = []
        for pair in range(HEADS // 2):
            pairs.append(
                _dot(olat_ref[2 * pair].astype(BF16), wuvp_ref[2 * pair])
                + _dot(olat_ref[2 * pair + 1].astype(BF16), wuvp_ref[2 * pair + 1]))
        y_mla = sum(
            _dot(pairs[pair].astype(BF16), womla_ref[pair * HEAD_PAD:(pair + 1) * HEAD_PAD, :])
            for pair in range(HEADS // 2))
    else:
        (o_ref, x_ref, gcy_ref, gm_ref, lng_ref, lnb_ref, womla_ref, wo_ref,
         l1g_ref, l1b_ref, wup_ref, wdn_ref, l2g_ref, l2b_ref, y_ref) = refs
        y_mla = _dot(o_ref[...], womla_ref[...])
    rows = x_ref.shape[0]
    halves = [slice(0, rows // 2), slice(rows // 2, rows)]
    mixes = []
    for r in halves:
        mix_in = gcy_ref[r, :].astype(F32) + gm_ref[r, :].astype(F32) * y_mla[r, :]
        mixes.append(_dot(mix_in.astype(BF16), wo_ref[...]))
    h1s = []
    for r, mix in zip(halves, mixes):
        h = _layer_norm(x_ref[r, :], lng_ref[...], lnb_ref[...])
        h1s.append(_layer_norm(ALPHA * h + mix, l1g_ref[...], l1b_ref[...]))
    h1bs = [h1.astype(BF16) for h1 in h1s]
    ffs = [None, None]
    chunk = D_MODEL
    for c in range(D_FF // chunk):
        acts = [jnp.maximum(_dot(h1b, wup_ref[:, c * chunk:(c + 1) * chunk]), 0.0) for h1b in h1bs]
        for i, a in enumerate(acts):
            part = _dot((a * a).astype(BF16), wdn_ref[c * chunk:(c + 1) * chunk, :])
            ffs[i] = part if ffs[i] is None else ffs[i] + part
    for r, h1, ff in zip(halves, h1s, ffs):
        y_ref[r, :] = _layer_norm(ALPHA * h1 + ff, l2g_ref[...], l2b_ref[...])


def _full(shape):
    n = len(shape)
    return pl.BlockSpec(shape, lambda *_: (0,) * n)


def _resident(shape):
    n = len(shape)
    return pl.BlockSpec(shape, lambda *_: (0,) * n, pipeline_mode=pl.Buffered(1))


def _rope_cos_sin(pos, transposed=False):
    inv_freq = ROPE_THETA ** (-2.0 * jnp.arange(HALF, dtype=F32) / QK_ROPE)
    ang = pos.astype(F32)[:, None] * inv_freq[None, :]
    cos, sin = jnp.cos(ang), jnp.sin(ang)
    return (cos.T, sin.T) if transposed else (cos, sin)


def _rope_tables(pos, scale):
    cos, sin = _rope_cos_sin(pos)
    n = pos.shape[0]
    ones = jnp.ones((n, ROPE_LO), F32)
    z_lo = jnp.zeros((n, ROPE_LO), F32)
    z_hi = jnp.zeros((n, HEAD_PAD - ROPE_LO - QK_ROPE), F32)
    z_half = jnp.zeros((n, HALF), F32)
    c = jnp.concatenate([ones, cos, cos, z_hi + 1.0], axis=1)
    a = jnp.concatenate([z_lo, -sin, z_half, z_hi], axis=1)
    b = jnp.concatenate([z_lo, z_half, sin, z_hi], axis=1)
    return c * scale, a * scale, b * scale


def _pack_weights(w_in, w_uq, w_uk, w_uv):
    offs = [0]
    for s in (Q_LORA, KV_LORA, QK_ROPE, CONV_DIM, CONV_DIM, CONV_DIM, D_MODEL, D_MODEL):
        offs.append(offs[-1] + s)
    pieces = [w_in[:, offs[i]:offs[i + 1]] for i in range(8)]
    kpe_pad = jnp.pad(pieces[2], ((0, 0), (ROPE_LO, HEAD_PAD - ROPE_LO - QK_ROPE)))
    w_pack = jnp.concatenate([pieces[0], kpe_pad, pieces[1]] + pieces[3:], axis=1).astype(BF16)
    head_pad = HEAD_PAD - QK_NOPE - QK_ROPE
    w_uq_p = jnp.pad(w_uq.reshape(Q_LORA, HEADS, QK_NOPE + QK_ROPE), ((0, 0), (0, 0), (0, head_pad)))
    w_uq_p = w_uq_p.reshape(Q_LORA, HEADS * HEAD_PAD).astype(BF16)
    w_uk_p = jnp.pad(w_uk, ((0, 0), (0, 0), (0, HEAD_PAD - QK_NOPE)))
    w_uk_p = w_uk_p.reshape(KV_LORA, HEADS * HEAD_PAD).astype(BF16)
    w_uk_t = jnp.pad(jnp.transpose(w_uk, (1, 2, 0)), ((0, 0), (0, HEAD_PAD - QK_NOPE), (0, 0))).astype(BF16)
    w_uv_t = w_uv.reshape(KV_LORA, HEADS * V_HEAD).T.astype(BF16)
    w_uv_even = jnp.pad(w_uv, ((0, 0), (0, 0), (0, V_HEAD)))
    w_uv_odd = jnp.pad(w_uv, ((0, 0), (0, 0), (V_HEAD, 0)))
    is_even = (jnp.arange(HEADS) % 2 == 0)[None, :, None]
    w_uv_p = jnp.transpose(jnp.where(is_even, w_uv_even, w_uv_odd), (1, 0, 2)).astype(BF16)
    return w_pack, w_uq_p, w_uk_p, w_uk_t, w_uv_t, w_uv_p


def _proj_seq(x, pos, init, shared, tile, ckv_base):
    nb, n, _ = x.shape
    row_offset = ckv_base.shape[1] - n
    nt = n // tile
    q_scale = SM_SCALE * LOG2_E
    q_tabs = tuple(t * q_scale for t in _rope_cos_sin(pos, transposed=True))
    k_tabs = _rope_tables(pos, 1.0)
    row_spec = lambda width: pl.BlockSpec((None, tile, width), lambda t, b: (b, t, 0))
    tab_spec = pl.BlockSpec((tile, HEAD_PAD), lambda t, b: (t, 0))
    tab_t_spec = pl.BlockSpec((HALF, tile), lambda t, b: (0, t))
    ckv_spec = pl.BlockSpec((pl.Element(1), pl.Element(tile), pl.Element(KV_LORA)),
                            lambda t, b: (b, pl.multiple_of(row_offset + t * tile, 8), 0))
    in_specs = ([row_spec(D_MODEL)] + [_resident(a.shape) for a in shared] + [tab_t_spec] * 2 + [tab_spec] * 3
                + [_full(init.shape), pl.BlockSpec(memory_space=pl.ANY)])
    ins = (x, *shared, *q_tabs, *k_tabs, init, ckv_base)
    out_shape = (
        jax.ShapeDtypeStruct((nb, HEADS * HEAD_PAD, n), BF16),
        jax.ShapeDtypeStruct((nb, n, HEADS * HEAD_PAD), BF16),
        jax.ShapeDtypeStruct((nb, HEADS * V_HEAD, n), BF16),
        jax.ShapeDtypeStruct(ckv_base.shape, F32),
        jax.ShapeDtypeStruct((nb, QK_ROPE, n), F32),
        jax.ShapeDtypeStruct((nb, n, D_MODEL), BF16),
        jax.ShapeDtypeStruct((nb, n, D_MODEL), BF16),
        jax.ShapeDtypeStruct((nb, CONV_K - 1, CONV_DIM), F32),
    )
    col_spec = lambda height: pl.BlockSpec((None, height, tile), lambda t, b: (b, 0, t))
    out_specs = (col_spec(HEADS * HEAD_PAD), row_spec(HEADS * HEAD_PAD), col_spec(HEADS * V_HEAD),
                 ckv_spec, col_spec(QK_ROPE), row_spec(D_MODEL), row_spec(D_MODEL),
                 _full((nb, CONV_K - 1, CONV_DIM)))
    return pl.pallas_call(
        functools.partial(_proj_seq_kernel, q_scale=q_scale),
        grid=(nt, nb),
        in_specs=in_specs,
        out_specs=out_specs,
        out_shape=out_shape,
        scratch_shapes=[pltpu.VMEM((tile + 8, CONV_DIM), F32)],
        input_output_aliases={len(ins) - 1: 3},
        compiler_params=pltpu.CompilerParams(
            dimension_semantics=("arbitrary", "arbitrary"), vmem_limit_bytes=VMEM_LIMIT),
        name="proj_seq",
    )(*ins)


def _proj_sample(x, tabs, s0, s1, shared, tile):
    n = x.shape[0]
    n_seq = n // 8
    row = lambda width: pl.BlockSpec((tile, width), lambda i: (i, 0))
    seq4 = lambda width: pl.BlockSpec((tile // 8, HEADS, 8, width), lambda i: (i, 0, 0, 0))
    ins = (x, *shared, *tabs, s0, s1)
    in_specs = ([row(D_MODEL)] + [_full(a.shape) for a in shared] + [row(HEAD_PAD)] * 6
                + [row(CONV_DIM)] * 2)
    out_specs = (seq4(KV_LORA), seq4(QK_ROPE), row(KV_LORA), row(HEAD_PAD), row(D_MODEL), row(D_MODEL),
                 row(CONV_DIM))
    out_shape = (
        jax.ShapeDtypeStruct((n_seq, HEADS, 8, KV_LORA), F32),
        jax.ShapeDtypeStruct((n_seq, HEADS, 8, QK_ROPE), F32),
        jax.ShapeDtypeStruct((n, KV_LORA), F32),
        jax.ShapeDtypeStruct((n, HEAD_PAD), F32),
        jax.ShapeDtypeStruct((n, D_MODEL), BF16),
        jax.ShapeDtypeStruct((n, D_MODEL), BF16),
        jax.ShapeDtypeStruct((n, CONV_DIM), F32),
    )
    return pl.pallas_call(
        _proj_sample_kernel,
        grid=(n // tile,),
        in_specs=in_specs,
        out_specs=out_specs,
        out_shape=out_shape,
        scratch_shapes=[pltpu.VMEM((tile + 8, CONV_DIM), F32)],
        compiler_params=pltpu.CompilerParams(
            dimension_semantics=("arbitrary",), vmem_limit_bytes=VMEM_LIMIT),
        name="proj_sample",
    )(*ins)


def _attn_prompt(q_t, k, v_t, k_meta, v_meta_t, tq, tiles_per_step):
    nb, n, _ = k.shape
    step_rows = tq * tiles_per_step
    return pl.pallas_call(
        functools.partial(_attn_prompt_kernel, tq=tq),
        grid=(nb, n // step_rows),
        in_specs=[
            pl.BlockSpec((None, HEADS * HEAD_PAD, step_rows), lambda b, i: (b, 0, i)),
            pl.BlockSpec((None, n, HEADS * HEAD_PAD), lambda b, i: (b, 0, 0)),
            pl.BlockSpec((None, HEADS * V_HEAD, n), lambda b, i: (b, 0, 0)),
            _full(k_meta.shape),
            _full(v_meta_t.shape),
        ],
        out_specs=pl.BlockSpec((None, step_rows, HEADS * V_HEAD), lambda b, i: (b, i, 0)),
        out_shape=jax.ShapeDtypeStruct((nb, n, HEADS * V_HEAD), BF16),
        scratch_shapes=[pltpu.VMEM((HEADS, 1, tq), F32), pltpu.VMEM((HEADS, V_HEAD + 16, tq), F32)],
        compiler_params=pltpu.CompilerParams(
            dimension_semantics=("arbitrary", "arbitrary"), vmem_limit_bytes=VMEM_LIMIT),
        name="attn_prompt",
    )(q_t, k, v_t, k_meta, v_meta_t)


def _attn_sample(page_table, q_lat, q_pe, ckv_new, kpe_new, ckv_pool, kpe_pool_t):
    n_seq, n_pages = page_table.shape
    seq4 = lambda width: pl.BlockSpec((SEQS_PER_STEP, HEADS, 8, width), lambda s, pt: (s, 0, 0, 0))
    seq3 = lambda width: pl.BlockSpec((SEQS_PER_STEP, 8, width), lambda s, pt: (s, 0, 0))
    hbm = pl.BlockSpec(memory_space=pl.ANY)
    grid_spec = pltpu.PrefetchScalarGridSpec(
        num_scalar_prefetch=1,
        grid=(n_seq // SEQS_PER_STEP,),
        in_specs=[seq4(KV_LORA), seq4(QK_ROPE), seq3(KV_LORA), seq3(QK_ROPE), hbm, hbm],
        out_specs=pl.BlockSpec((HEADS, SEQS_PER_STEP * 8, KV_LORA), lambda s, pt: (0, s, 0)),
        scratch_shapes=[pltpu.VMEM((2, SEQS_PER_STEP, n_pages * PAGE, KV_LORA), F32),
                        pltpu.VMEM((2, SEQS_PER_STEP, QK_ROPE, n_pages * PAGE), F32),
                        pltpu.SemaphoreType.DMA((2,)), pltpu.SemaphoreType.DMA((2,))],
    )
    return pl.pallas_call(
        functools.partial(_attn_sample_kernel, n_pages=n_pages),
        grid_spec=grid_spec,
        out_shape=jax.ShapeDtypeStruct((HEADS, n_seq * 8, KV_LORA), F32),
        compiler_params=pltpu.CompilerParams(
            dimension_semantics=("arbitrary",), vmem_limit_bytes=VMEM_LIMIT),
        name="attn_sample",
    )(page_table.reshape(-1), q_lat, q_pe, ckv_new, kpe_new, ckv_pool, kpe_pool_t)


def _post(front, x, gcy, gm, weights, tile, sample):
    n = x.shape[0]
    row = lambda width: pl.BlockSpec((tile, width), lambda i: (i, 0))
    if sample:
        o_lat, w_uv_p = front
        front_specs = [pl.BlockSpec((HEADS, tile, KV_LORA), lambda i: (0, i, 0)), _full(w_uv_p.shape)]
    else:
        front_specs = [row(HEADS * V_HEAD)]
    in_specs = front_specs + [row(D_MODEL), row(D_MODEL), row(D_MODEL)] + [_resident(w.shape) for w in weights]
    return pl.pallas_call(
        functools.partial(_post_kernel, sample=sample),
        grid=(n // tile,),
        in_specs=in_specs,
        out_specs=row(D_MODEL),
        out_shape=jax.ShapeDtypeStruct((n, D_MODEL), F32),
        compiler_params=pltpu.CompilerParams(
            dimension_semantics=("arbitrary",), vmem_limit_bytes=VMEM_LIMIT),
        name="post_sample" if sample else "post_prompt",
    )(*front, x, gcy, gm, *weights)


def kernel(x_prompt, x_sample, cache_ckv, cache_kpe, state_conv, page_table, meta_tokens, ln_emb_g, ln_emb_b, w_in, q_norm_g, w_uq, kv_norm_g, w_uk, w_uv, w_o_mla, conv_w, w_conv_out, w_o, ln1_g, ln1_b, w_up, w_down, ln2_g, ln2_b):
    assert w_in.shape[0] == DEPTH
    nb, seq, _ = x_prompt.shape
    n_seq, dec_seq, _ = x_sample.shape
    assert dec_seq == 8
    past_len = page_table.shape[1] * PAGE

    row2 = lambda a: a.reshape(1, -1).astype(F32)
    w_pack, w_uq_p, w_uk_p, w_uk_t, w_uv_t, w_uv_p = _pack_weights(w_in[0], w_uq[0], w_uk[0], w_uv[0])
    w_co = w_conv_out[0].astype(BF16)
    lng, lnb = row2(ln_emb_g), row2(ln_emb_b)
    qng, kvg = row2(q_norm_g[0]), row2(kv_norm_g[0])
    shared_seq = (lng, lnb, w_pack, qng, w_uq_p.T, kvg, w_uk_p, w_uv_t, conv_w[0], w_co)
    shared_sample = (lng, lnb, w_pack, qng, w_uq_p, kvg, w_uk_t, conv_w[0], w_co)
    post_w = (lng, lnb, w_o_mla[0].astype(BF16), w_o[0].astype(BF16), row2(ln1_g[0]), row2(ln1_b[0]),
              w_up[0].astype(BF16), w_down[0].astype(BF16), row2(ln2_g[0]), row2(ln2_b[0]))

    zero_state = jnp.zeros((CONV_K - 1, CONV_DIM), F32)
    _, k_meta, vt_meta, ckv_meta, kpet_meta, _, _, conv_meta = _proj_seq(
        meta_tokens[None], jnp.arange(N_META), zero_state, shared_seq, N_META,
        jnp.zeros((1, N_META, KV_LORA), F32))

    q_t, k, v_t, ckv_p, kpet_p, gcy_p, gm_p, conv_p = _proj_seq(
        x_prompt, N_META + jnp.arange(seq), conv_meta[0], shared_seq, DENSE_TILE,
        jnp.pad(jnp.broadcast_to(ckv_meta, (nb, N_META, KV_LORA)), ((0, 0), (0, seq), (0, 0))))
    o_p = _attn_prompt(q_t, k, v_t, k_meta[0], vt_meta[0], QUERY_TILE, QUERY_TILES_PER_STEP)
    n_p = nb * seq
    y_prompt = _post((o_p.reshape(n_p, -1),), x_prompt.reshape(n_p, D_MODEL), gcy_p.reshape(n_p, D_MODEL),
                     gm_p.reshape(n_p, D_MODEL), post_w, DENSE_TILE, sample=False).reshape(nb, seq, D_MODEL)

    n_s = n_seq * dec_seq
    pos_s = past_len + jnp.arange(dec_seq)
    tabs_s = tuple(jnp.tile(t, (n_seq, 1)) for t in _rope_tables(pos_s, SM_SCALE) + _rope_tables(pos_s, 1.0))
    s0 = jnp.repeat(state_conv[0, :, 0], dec_seq, axis=0)
    s1 = jnp.repeat(state_conv[0, :, 1], dec_seq, axis=0)
    xs = x_sample.reshape(n_s, D_MODEL)
    q_lat, q_pe, ckv_s, kpe_s128, gcy_s, gm_s, u_s = _proj_sample(xs, tabs_s, s0, s1, shared_sample, SAMPLE_PROJ_TILE)
    kpe_s = kpe_s128[:, ROPE_LO:ROPE_LO + QK_ROPE]
    o_lat = _attn_sample(page_table, q_lat, q_pe, ckv_s.reshape(n_seq, dec_seq, KV_LORA),
                         kpe_s.reshape(n_seq, dec_seq, QK_ROPE), cache_ckv[0],
                         jnp.swapaxes(cache_kpe[0], 1, 2))
    y_sample = _post((o_lat, w_uv_p), xs, gcy_s, gm_s, post_w, DENSE_TILE, sample=True).reshape(
        n_seq, dec_seq, D_MODEL)

    new_kpe_prompt = jnp.swapaxes(
        jnp.concatenate([jnp.broadcast_to(kpet_meta, (nb, QK_ROPE, N_META)), kpet_p], axis=2), 1, 2)[None]
    new_conv_sample = u_s.reshape(n_seq, dec_seq, CONV_DIM)[:, dec_seq - (CONV_K - 1):][None]
    return (y_prompt, y_sample, ckv_p[None], new_kpe_prompt, conv_p[None],
            ckv_s.reshape(n_seq, dec_seq, KV_LORA)[None], kpe_s.reshape(n_seq, dec_seq, QK_ROPE)[None],
            new_conv_sample)
```
